```python
import math
import jax, jax.numpy as jnp
from jax import lax
import numpy as np

D_MODEL = 2048
BATCH = 2
SEQ = 4096
DEPTH = 1
DEC_BATCH = 32
DEC_SEQ = 64
PAST_LEN = 4096

CHUNK = 64
QBLOCK = 128
MLA_HEADS = 16
Q_LORA = 512
KV_LORA = 512
NOPE_DIM = 128
ROPE_DIM = 64
QK_DIM = NOPE_DIM + ROPE_DIM
V_DIM = 128
ROPE_THETA = 10000.0
ATTN_SCALE = QK_DIM ** -0.5
M_EXPAND = 2
M_INNER = M_EXPAND * D_MODEL
M_HEADDIM = 64
M_HEADS = M_INNER // M_HEADDIM
M_GROUPS = 8
D_STATE = 128
CONV_W = 4
CONV_CH = M_INNER + 2 * M_GROUPS * D_STATE
SSD_CHUNK = CHUNK
N_EXPERTS = 32
TOP_K = 4
D_FF = D_MODEL
SWIGLU_LIMIT = 7.0
SWIGLU_ALPHA = 1.702
MOE_BLOCK = 128
EPS = 1e-6
IN_COLS = Q_LORA + KV_LORA + ROPE_DIM + M_INNER + CONV_CH + M_HEADS + 2 * D_MODEL

kernel_name = 'streaming_mla_ssd_moe_hybrid'


def _rmsnorm(x, g):
    xf = x.astype(jnp.float32)
    xf = xf * lax.rsqrt(jnp.mean(xf * xf, axis=-1, keepdims=True) + EPS)
    return xf.astype(x.dtype) * g


def _gated_rmsnorm(y, z, g):
    b, s, w = y.shape
    u = (y * jax.nn.silu(z)).astype(jnp.float32).reshape(b, s, M_GROUPS, w // M_GROUPS)
    u = u * lax.rsqrt(jnp.mean(u * u, axis=-1, keepdims=True) + EPS)
    return u.reshape(b, s, w).astype(y.dtype) * g


def _rope(x, pos):
    half = x.shape[-1] // 2
    inv = ROPE_THETA ** (-jnp.arange(half, dtype=jnp.float32) / half)
    ang = pos.astype(jnp.float32)[:, None] * inv[None, :]
    cos = jnp.cos(ang)[None, :, None, :]
    sin = jnp.sin(ang)[None, :, None, :]
    x1 = x[..., :half].astype(jnp.float32)
    x2 = x[..., half:].astype(jnp.float32)
    return jnp.concatenate([x1 * cos - x2 * sin, x2 * cos + x1 * sin], axis=-1).astype(x.dtype)


def _split_cols(proj):
    sizes = (Q_LORA, KV_LORA, ROPE_DIM, M_INNER, CONV_CH, M_HEADS, D_MODEL, D_MODEL)
    out, start = [], 0
    for size in sizes:
        out.append(proj[..., start:start + size])
        start += size
    return out


def _mla_attend(q, lat, kr, q_pos, k_pos, w_uk, w_uv, g_kn):
    sk = lat.shape[0]
    k_nope = jnp.einsum('tr,rhd->thd', lat, w_uk)
    k = jnp.concatenate([k_nope, jnp.broadcast_to(kr[:, None, :], (sk, MLA_HEADS, ROPE_DIM))], axis=-1)
    k = _rmsnorm(k, g_kn)
    k_chunk = k_pos // CHUNK

    def block(args):
        qb, qp = args
        sc = jnp.einsum('qhd,khd->hqk', qb, k).astype(jnp.float32) * ATTN_SCALE
        mask = k_chunk[None, :] <= (qp // CHUNK)[:, None]
        sc = jnp.where(mask[None], sc, -jnp.inf)
        p = jax.nn.softmax(sc, axis=-1)
        ol = jnp.einsum('hqk,kr->qhr', p.astype(lat.dtype), lat)
        return jnp.einsum('qhr,rhd->qhd', ol, w_uv)

    sq = q.shape[0]
    if sq > QBLOCK and sq % QBLOCK == 0:
        nb = sq // QBLOCK
        out = lax.map(block, (q.reshape(nb, QBLOCK, MLA_HEADS, QK_DIM), q_pos.reshape(nb, QBLOCK)))
        return out.reshape(sq, MLA_HEADS, V_DIM)
    return block((q, q_pos))


def _mla(cq, ckv, kr, q_pos, k_pos, past_lat, past_kr, g_cq, g_ckv, w_uq, w_uk, w_uv, g_qn, g_kn):
    q = jnp.einsum('bsr,rhd->bshd', _rmsnorm(cq, g_cq), w_uq)
    q = jnp.concatenate([q[..., :NOPE_DIM], _rope(q[..., NOPE_DIM:], q_pos)], axis=-1)
    q = _rmsnorm(q, g_qn)
    lat = _rmsnorm(ckv, g_ckv)
    kr_rot = _rope(kr[:, :, None, :], q_pos)[:, :, 0, :]
    lat_all = lat if past_lat is None else jnp.concatenate([past_lat.astype(lat.dtype), lat], axis=1)
    kr_all = kr_rot if past_kr is None else jnp.concatenate([past_kr.astype(kr_rot.dtype), kr_rot], axis=1)

    def per_seq(args):
        q_b, lat_b, kr_b = args
        return _mla_attend(q_b, lat_b, kr_b, q_pos, k_pos, w_uk, w_uv, g_kn)

    o = lax.map(per_seq, (q, lat_all, kr_all))
    return o.reshape(o.shape[0], o.shape[1], MLA_HEADS * V_DIM), lat, kr_rot


def _causal_conv(u, buf, w, b):
    s = u.shape[1]
    up = jnp.concatenate([buf.astype(u.dtype), u], axis=1)
    y = up[:, 0:s] * w[0]
    for tap in range(1, CONV_W):
        y = y + up[:, tap:tap + s] * w[tap]
    return y + b, up[:, up.shape[1] - (CONV_W - 1):]


def _ssd(xs, dt, a, bm, cm, d_skip, ssm0):
    bsz, s, h, p = xs.shape
    l = min(SSD_CHUNK, s)
    nc = s // l
    g, k, n = M_GROUPS, h // M_GROUPS, D_STATE
    f32 = jnp.float32
    x = xs.astype(f32).reshape(bsz, nc, l, g, k, p)
    dt = dt.astype(f32).reshape(bsz, nc, l, g, k)
    bm = bm.astype(f32).reshape(bsz, nc, l, g, n)
    cm = cm.astype(f32).reshape(bsz, nc, l, g, n)
    cs = jnp.cumsum(dt * a.reshape(g, k), axis=2)
    xdt = x * dt[..., None]
    seg = cs[:, :, :, None] - cs[:, :, None]
    tri = jnp.tril(jnp.ones((l, l), bool))
    decay = jnp.exp(jnp.where(tri[:, :, None, None], seg, -jnp.inf))
    cb = jnp.einsum('bcign,bcjgn->bcijg', cm, bm)
    y_diag = jnp.einsum('bcijgk,bcjgkp->bcigkp', cb[..., None] * decay, xdt)
    w_end = jnp.exp(cs[:, :, -1:] - cs)
    chunk_states = jnp.einsum('bcjgn,bcjgkp->bcgkpn', bm, xdt * w_end[..., None])
    chunk_decay = jnp.exp(cs[:, :, -1])

    def step(st, inp):
        dec, cst = inp
        return st * dec[..., None, None] + cst, st

    init = ssm0.astype(f32).reshape(bsz, g, k, p, n)
    final, prev = lax.scan(step, init, (jnp.moveaxis(chunk_decay, 1, 0), jnp.moveaxis(chunk_states, 1, 0)))
    prev = jnp.moveaxis(prev, 0, 1)
    y_off = jnp.einsum('bcign,bcgkpn->bcigkp', cm, prev) * jnp.exp(cs)[..., None]
    y = y_diag + y_off + x * d_skip.astype(f32).reshape(g, k)[:, :, None]
    return y.reshape(bsz, s, h, p).astype(xs.dtype), final.reshape(bsz, h, p, n).astype(ssm0.dtype)


def _mamba(z, xbc, dt_raw, conv0, ssm0, conv_w, conv_b, dt_bias, a_log, d_skip, g_ssm):
    b, s, _ = z.shape
    u, conv_new = _causal_conv(xbc, conv0, conv_w, conv_b)
    u = jax.nn.silu(u)
    gn = M_GROUPS * D_STATE
    xs = u[..., :M_INNER].reshape(b, s, M_HEADS, M_HEADDIM)
    bm = u[..., M_INNER:M_INNER + gn].reshape(b, s, M_GROUPS, D_STATE)
    cm = u[..., M_INNER + gn:].reshape(b, s, M_GROUPS, D_STATE)
    dt = jax.nn.softplus((dt_raw + dt_bias).astype(jnp.float32))
    a = -jnp.exp(a_log.astype(jnp.float32))
    y, ssm_new = _ssd(xs, dt, a, bm, cm, d_skip, ssm0)
    y = _gated_rmsnorm(y.reshape(b, s, M_INNER), z, g_ssm)
    return y, ssm_new, conv_new


def _moe(h, w_router, b_router, w_gate, b_gate, w_up, b_up, w_down, b_down):
    t, d = h.shape
    logits = (h @ w_router + b_router).astype(jnp.float32)
    top_val, top_idx = lax.top_k(logits, TOP_K)
    probs = jax.nn.softmax(top_val, axis=-1)
    tk = t * TOP_K
    n_blocks = -(-(tk + N_EXPERTS * (MOE_BLOCK - 1)) // MOE_BLOCK)
    n_slots = n_blocks * MOE_BLOCK
    flat_e = top_idx.reshape(tk)
    order = jnp.argsort(flat_e, stable=True)
    sorted_e = flat_e[order]
    sorted_tok = (order // TOP_K).astype(jnp.int32)
    sorted_p = probs.reshape(tk)[order]
    counts = jnp.bincount(flat_e, length=N_EXPERTS)
    padded = (counts + MOE_BLOCK - 1) // MOE_BLOCK * MOE_BLOCK
    pad_end = jnp.cumsum(padded)
    pad_start = pad_end - padded
    start = jnp.cumsum(counts) - counts
    dest = pad_start[sorted_e] + jnp.arange(tk) - start[sorted_e]
    slot_tok = jnp.full((n_slots,), t, jnp.int32).at[dest].set(sorted_tok)
    slot_p = jnp.zeros((n_slots,), jnp.float32).at[dest].set(sorted_p)
    block_e = jnp.minimum(jnp.searchsorted(pad_end, jnp.arange(n_blocks) * MOE_BLOCK, side='right'), N_EXPERTS - 1)
    h_pad = jnp.concatenate([h, jnp.zeros((1, d), h.dtype)], axis=0)
    xb = h_pad[slot_tok].reshape(n_blocks, MOE_BLOCK, d)

    def expert_block(args):
        xblk, e = args
        gate = jnp.minimum(xblk @ w_gate[e] + b_gate[e], SWIGLU_LIMIT)
        up = jnp.clip(xblk @ w_up[e] + b_up[e], -SWIGLU_LIMIT, SWIGLU_LIMIT)
        glu = gate * jax.nn.sigmoid(SWIGLU_ALPHA * gate)
        return ((up + 1.0) * glu) @ w_down[e] + b_down[e]

    yb = lax.map(expert_block, (xb, block_e)).reshape(n_slots, d)
    y = jnp.zeros((t + 1, d), h.dtype).at[slot_tok].add(yb * slot_p[:, None].astype(yb.dtype))
    return y[:t]


def _layer(x, c, q_pos, k_pos, past, w_ada, b_ada, g_norm1, w_in, g_cq, g_ckv, w_uq, w_uk, w_uv, g_qn, g_kn,
           conv_w, conv_b, dt_bias, a_log, d_skip, g_ssm, w_pa, w_pb, w_out, g_norm2, w_router, b_router,
           w_gate, b_gate, w_up, b_up, w_down, b_down):
    b, s, _ = x.shape
    ada = jax.nn.silu(c) @ w_ada + b_ada
    sh1, sc1, gt1, sh2, sc2, gt2 = jnp.split(ada[:, None, :], 6, axis=-1)
    h = _rmsnorm(x, g_norm1) * (1.0 + sc1) + sh1
    cq, ckv, kr, z, xbc, dt_raw, ga, gb = _split_cols(h @ w_in)
    if past is None:
        past_lat, past_kr = None, None
        ssm0 = jnp.zeros((b, M_HEADS, M_HEADDIM, D_STATE), x.dtype)
        conv0 = jnp.zeros((b, CONV_W - 1, CONV_CH), x.dtype)
    else:
        past_lat, past_kr, ssm0, conv0 = past
    a_out, lat_new, kr_new = _mla(cq, ckv, kr, q_pos, k_pos, past_lat, past_kr,
                                  g_cq, g_ckv, w_uq, w_uk, w_uv, g_qn, g_kn)
    m_out, ssm_new, conv_new = _mamba(z, xbc, dt_raw, conv0, ssm0, conv_w, conv_b, dt_bias, a_log, d_skip, g_ssm)
    mixed = jax.nn.sigmoid(ga) * (a_out @ w_pa) + jax.nn.sigmoid(gb) * (m_out @ w_pb)
    x = x + gt1 * (mixed @ w_out)
    h2 = _rmsnorm(x, g_norm2) * (1.0 + sc2) + sh2
    ff = _moe(h2.reshape(b * s, D_MODEL), w_router, b_router, w_gate, b_gate, w_up, b_up, w_down, b_down)
    x = x + gt2 * ff.reshape(b, s, D_MODEL)
    return x, lat_new, kr_new, ssm_new, conv_new


def setup_inputs(seed: int = 0) -> dict:
    key = jax.random.key(seed)
    ks = iter(jax.random.split(key, 48))
    f32 = jnp.float32

    def nrm(shape, scale):
        return scale * jax.random.normal(next(ks), shape, f32)

    def gain(shape):
        return 1.0 + nrm(shape, 0.02)

    L = DEPTH
    dt0 = jnp.exp(jax.random.uniform(next(ks), (L, M_HEADS), f32, math.log(1e-3), math.log(1e-1)))
    dt_bias = dt0 + jnp.log(-jnp.expm1(-dt0))
    a_log = jnp.log(jax.random.uniform(next(ks), (L, M_HEADS), f32, 1.0, 16.0))
    return {
        'x_prompt': nrm((BATCH, SEQ, D_MODEL), 1.0),
        'x_sample': nrm((DEC_BATCH, DEC_SEQ, D_MODEL), 1.0),
        'cache_mla_latent': nrm((L, DEC_BATCH, PAST_LEN, KV_LORA), 1.0),
        'cache_mla_krope': nrm((L, DEC_BATCH, PAST_LEN, ROPE_DIM), 1.0),
        'state_ssm': nrm((L, DEC_BATCH, M_HEADS, M_HEADDIM, D_STATE), 0.1),
        'state_conv': nrm((L, DEC_BATCH, CONV_W - 1, CONV_CH), 1.0),
        'c_prompt': nrm((BATCH, D_MODEL), 1.0),
        'c_sample': nrm((DEC_BATCH, D_MODEL), 1.0),
        'w_ada': nrm((L, D_MODEL, 6 * D_MODEL), 0.5 * D_MODEL ** -0.5),
        'b_ada': nrm((L, 6 * D_MODEL), 0.02),
        'g_norm1': gain((L, D_MODEL)),
        'w_in': nrm((L, D_MODEL, IN_COLS), D_MODEL ** -0.5),
        'g_cq': gain((L, Q_LORA)),
        'g_ckv': gain((L, KV_LORA)),
        'w_uq': nrm((L, Q_LORA, MLA_HEADS, QK_DIM), Q_LORA ** -0.5),
        'w_uk': nrm((L, KV_LORA, MLA_HEADS, NOPE_DIM), KV_LORA ** -0.5),
        'w_uv': nrm((L, KV_LORA, MLA_HEADS, V_DIM), KV_LORA ** -0.5),
        'g_qn': gain((L, QK_DIM)),
        'g_kn': gain((L, QK_DIM)),
        'conv_w': nrm((L, CONV_W, CONV_CH), CONV_W ** -0.5),
        'conv_b': nrm((L, CONV_CH), 0.02),
        'dt_bias': dt_bias,
        'a_log': a_log,
        'd_skip': gain((L, M_HEADS)),
        'g_ssm': gain((L, M_INNER)),
        'w_pa': nrm((L, MLA_HEADS * V_DIM, D_MODEL), (MLA_HEADS * V_DIM) ** -0.5),
        'w_pb': nrm((L, M_INNER, D_MODEL), M_INNER ** -0.5),
        'w_out': nrm((L, D_MODEL, D_MODEL), D_MODEL ** -0.5),
        'g_norm2': gain((L, D_MODEL)),
        'w_router': nrm((L, D_MODEL, N_EXPERTS), D_MODEL ** -0.5),
        'b_router': nrm((L, N_EXPERTS), 0.01),
        'w_gate': nrm((L, N_EXPERTS, D_MODEL, D_FF), D_MODEL ** -0.5),
        'b_gate': nrm((L, N_EXPERTS, D_FF), 0.01),
        'w_up': nrm((L, N_EXPERTS, D_MODEL, D_FF), D_MODEL ** -0.5),
        'b_up': nrm((L, N_EXPERTS, D_FF), 0.01),
        'w_down': nrm((L, N_EXPERTS, D_FF, D_MODEL), D_FF ** -0.5),
        'b_down': nrm((L, N_EXPERTS, D_MODEL), 0.01),
    }


def reference(x_prompt, x_sample, cache_mla_latent, cache_mla_krope, state_ssm, state_conv, c_prompt, c_sample,
              w_ada, b_ada, g_norm1, w_in, g_cq, g_ckv, w_uq, w_uk, w_uv, g_qn, g_kn, conv_w, conv_b, dt_bias,
              a_log, d_skip, g_ssm, w_pa, w_pb, w_out, g_norm2, w_router, b_router, w_gate, b_gate, w_up, b_up,
              w_down, b_down):
    s_p = x_prompt.shape[1]
    s_s = x_sample.shape[1]
    past = cache_mla_latent.shape[2]
    pos_p = jnp.arange(s_p, dtype=jnp.int32)
    pos_sq = past + jnp.arange(s_s, dtype=jnp.int32)
    pos_sk = jnp.arange(past + s_s, dtype=jnp.int32)
    yp, ys = x_prompt, x_sample
    st_p, st_s = [], []
    for l in range(DEPTH):
        lw = [w[l] for w in (w_ada, b_ada, g_norm1, w_in, g_cq, g_ckv, w_uq, w_uk, w_uv, g_qn, g_kn, conv_w,
                             conv_b, dt_bias, a_log, d_skip, g_ssm, w_pa, w_pb, w_out, g_norm2, w_router,
                             b_router, w_gate, b_gate, w_up, b_up, w_down, b_down)]
        yp, lat_p, kr_p, ssm_p, conv_p = _layer(yp, c_prompt, pos_p, pos_p, None, *lw)
        ys, lat_s, kr_s, ssm_s, conv_s = _layer(
            ys, c_sample, pos_sq, pos_sk,
            (cache_mla_latent[l], cache_mla_krope[l], state_ssm[l], state_conv[l]), *lw)
        st_p.append((lat_p, kr_p, ssm_p, conv_p))
        st_s.append((lat_s, kr_s, ssm_s, conv_s))
    new_lat_p = jnp.stack([e[0] for e in st_p])
    new_kr_p = jnp.stack([e[1] for e in st_p])
    new_ssm_p = jnp.stack([e[2] for e in st_p])
    new_conv_p = jnp.stack([e[3] for e in st_p])
    new_lat_s = jnp.stack([e[0] for e in st_s])
    new_kr_s = jnp.stack([e[1] for e in st_s])
    new_ssm_s = jnp.stack([e[2] for e in st_s])
    new_conv_s = jnp.stack([e[3] for e in st_s])
    return (yp, ys, new_lat_p, new_kr_p, new_ssm_p, new_conv_p, new_lat_s, new_kr_s, new_ssm_s, new_conv_s)
```

```python
import functools
import math

import jax
import jax.numpy as jnp
from jax import lax
from jax.experimental import pallas as pl
from jax.experimental.pallas import tpu as pltpu

F32 = jnp.float32
BF16 = jnp.bfloat16
I32 = jnp.int32

CHUNK = 64
NOPE_DIM = 128
ROPE_DIM = 64
QK_DIM = NOPE_DIM + ROPE_DIM
V_DIM = 128
HEAD_PAD = 256
ROPE_THETA = 10000.0
ATTN_SCALE = QK_DIM ** -0.5
M_HEADDIM = 64
D_STATE = 128
CONV_W = 4
TOP_K = 4
SWIGLU_LIMIT = 7.0
SWIGLU_ALPHA = 1.702
EPS = 1e-6

LANES = 128
SUBLANES = 8
VMEM_LIMIT = 56 * 1024 * 1024

MOE_BLOCK = 256
NEG_BIG = -1e30

_NT = (((1,), (1,)), ((), ()))
_TN = (((0,), (0,)), ((), ()))


def _cp(*sem):
    return pltpu.CompilerParams(dimension_semantics=sem, vmem_limit_bytes=VMEM_LIMIT)


def _pick(n, prefs):
    for p in prefs:
        if n % p == 0:
            return p
    return n


def _dot(a, b):
    return jnp.dot(a, b, preferred_element_type=F32)


def _split3(v):
    hi = v.astype(BF16)
    r1 = v - hi.astype(F32)
    mid = r1.astype(BF16)
    lo = (r1 - mid.astype(F32)).astype(BF16)
    return hi, mid, lo


def _silu(x):
    return x * jax.nn.sigmoid(x)


def _softplus(x):
    return jnp.maximum(x, 0.0) + jnp.log1p(jnp.exp(-jnp.abs(x)))


def _ada_kernel(c_ref, w_ref, b_ref, o_ref):
    s = _silu(c_ref[...]).astype(BF16)
    o_ref[...] = _dot(s, w_ref[...].astype(BF16)) + b_ref[...]


def _ada(c_all, w_ada, b_ada):
    r, d = c_all.shape
    n = w_ada.shape[1]
    tn = _pick(n, (1024, 512, 256, 128))
    return pl.pallas_call(
        _ada_kernel,
        out_shape=jax.ShapeDtypeStruct((r, n), F32),
        grid=(n // tn,),
        in_specs=[pl.BlockSpec((r, d), lambda j: (0, 0)),
                  pl.BlockSpec((d, tn), lambda j: (0, j)),
                  pl.BlockSpec((1, tn), lambda j: (0, j))],
        out_specs=pl.BlockSpec((r, tn), lambda j: (0, j)),
        compiler_params=_cp("arbitrary"),
        name="ada",
    )(c_all, w_ada, b_ada.reshape(1, n))


def _norm1_kernel(xp_ref, xs_ref, g_ref, sc_ref, sh_ref, h_ref, xall_ref, *, npb):
    i = pl.program_id(0)
    x = jnp.where(i < npb, xp_ref[...], xs_ref[...])
    xall_ref[...] = x
    xn = x * lax.rsqrt(jnp.mean(x * x, axis=-1, keepdims=True) + EPS)
    h_ref[...] = (xn * g_ref[...] * (1.0 + sc_ref[...]) + sh_ref[...]).astype(BF16)


def _norm1(xp3, xs3, g, sc, sh):
    ncp, _, d = xp3.shape
    ncs = xs3.shape[0]
    nch = ncp + ncs
    gc = _pick(math.gcd(ncp, ncs), (4, 2, 1))
    npb = ncp // gc
    blk = (gc, CHUNK, d)
    mod = pl.BlockSpec((gc, 1, d), lambda i: (i, 0, 0))
    return pl.pallas_call(
        functools.partial(_norm1_kernel, npb=npb),
        out_shape=(jax.ShapeDtypeStruct((nch, CHUNK, d), BF16),
                   jax.ShapeDtypeStruct((nch, CHUNK, d), F32)),
        grid=(nch // gc,),
        in_specs=[pl.BlockSpec(blk, lambda i: (jnp.minimum(i, npb - 1), 0, 0)),
                  pl.BlockSpec(blk, lambda i: (jnp.maximum(i - npb, 0), 0, 0)),
                  pl.BlockSpec((1, 1, d), lambda i: (0, 0, 0)),
                  mod, mod],
        out_specs=(pl.BlockSpec(blk, lambda i: (i, 0, 0)),
                   pl.BlockSpec(blk, lambda i: (i, 0, 0))),
        compiler_params=_cp("arbitrary"),
        name="norm1",
    )(xp3, xs3, g.reshape(1, 1, d), sc, sh)


def _mm_kernel(x_ref, w_ref, o_ref, *, act):
    acc = _dot(x_ref[...], w_ref[...])
    if act == "sigmoid":
        acc = jax.nn.sigmoid(acc)
    o_ref[...] = acc.astype(o_ref.dtype)


def _mm(x, w, out_dtype, name, act=None):
    m, k = x.shape
    n = w.shape[1]
    tm = _pick(m, (1024, 512, 256))
    tn = _pick(n, (1024, 512, 256, 128))
    return pl.pallas_call(
        functools.partial(_mm_kernel, act=act),
        out_shape=jax.ShapeDtypeStruct((m, n), out_dtype),
        grid=(m // tm, n // tn),
        in_specs=[pl.BlockSpec((tm, k), lambda i, j: (i, 0)),
                  pl.BlockSpec((k, tn), lambda i, j: (0, j))],
        out_specs=pl.BlockSpec((tm, tn), lambda i, j: (i, j)),
        compiler_params=_cp("arbitrary", "arbitrary"),
        name=name,
    )(x, w)


def _lat_kernel(h_ref, w_ref, wdt_ref, gcq_ref, gckv_ref, cos_ref, sin_ref, dtb_ref, dtbc_ref,
                cqn_ref, lat_ref, latk_ref, kr_ref, dt_ref, dtt_ref, *, ql, kvl):
    h = h_ref[...]
    acc = _dot(h, w_ref[...])
    cq = acc[:, :ql]
    cqn = cq * lax.rsqrt(jnp.mean(cq * cq, axis=-1, keepdims=True) + EPS) * gcq_ref[...]
    cqn_ref[...] = cqn.astype(BF16)
    ckv = acc[:, ql:ql + kvl]
    lat = ckv * lax.rsqrt(jnp.mean(ckv * ckv, axis=-1, keepdims=True) + EPS) * gckv_ref[...]
    lat_ref[...] = lat
    o = ql + kvl
    kr128 = acc[:, o:o + LANES] * cos_ref[...] + acc[:, o + LANES:o + 2 * LANES] * sin_ref[...]
    kr_ref[...] = kr128[:, :ROPE_DIM]
    latk_ref[:, :kvl] = lat.astype(BF16)
    latk_ref[:, kvl:] = kr128.astype(BF16)
    nh = dt_ref.shape[-1]
    dt_ref[...] = _softplus(acc[:, o + 2 * LANES:o + 2 * LANES + nh] + dtb_ref[...])
    dtt = lax.dot_general(wdt_ref[...], h, _NT, preferred_element_type=F32)
    dtt_ref[...] = _softplus(dtt + dtbc_ref[...])


def _lat(h_all, w_lat, w_dt_t, g_cq, g_ckv, cos128, sin128, dt_bias):
    t, d = h_all.shape
    ql, kvl = g_cq.shape[-1], g_ckv.shape[-1]
    nh = dt_bias.shape[-1]
    nl = w_lat.shape[1]
    tm = _pick(t, (512, 256, 128))
    row = lambda w: pl.BlockSpec((tm, w), lambda i: (i, 0))
    const = lambda a, b: pl.BlockSpec((a, b), lambda i: (0, 0))
    return pl.pallas_call(
        functools.partial(_lat_kernel, ql=ql, kvl=kvl),
        out_shape=(jax.ShapeDtypeStruct((t, ql), BF16),
                   jax.ShapeDtypeStruct((t, kvl), F32),
                   jax.ShapeDtypeStruct((t, kvl + LANES), BF16),
                   jax.ShapeDtypeStruct((t, ROPE_DIM), F32),
                   jax.ShapeDtypeStruct((t, nh), F32),
                   jax.ShapeDtypeStruct((nh, t), F32)),
        grid=(t // tm,),
        in_specs=[row(d), const(d, nl), const(nh, d), const(1, ql), const(1, kvl),
                  row(LANES), row(LANES), const(1, nh), const(nh, 1)],
        out_specs=(row(ql), row(kvl), row(kvl + LANES), row(ROPE_DIM), row(nh),
                   pl.BlockSpec((nh, tm), lambda i: (0, i))),
        compiler_params=_cp("arbitrary"),
        name="latent_proj",
    )(h_all, w_lat, w_dt_t, g_cq.reshape(1, ql), g_ckv.reshape(1, kvl), cos128, sin128,
      dt_bias.reshape(1, nh), dt_bias.reshape(nh, 1))


def _q_kernel(c_ref, wa_ref, wb_ref, cos_ref, sin_ref, gn_ref, gr_ref, o_ref, *, heads):
    c = c_ref[...]
    a = _dot(c, wa_ref[...])
    b = _dot(c, wb_ref[...])
    cos, sin = cos_ref[...], sin_ref[...]
    for h in range(heads):
        nope = a[:, h * HEAD_PAD:h * HEAD_PAD + LANES]
        rope = a[:, h * HEAD_PAD + LANES:(h + 1) * HEAD_PAD] * cos + b[:, h * LANES:(h + 1) * LANES] * sin
        ss = (jnp.sum(nope * nope, axis=-1, keepdims=True)
              + jnp.sum(rope * rope, axis=-1, keepdims=True)) * (1.0 / QK_DIM)
        r = lax.rsqrt(ss + EPS)
        o_ref[:, h * HEAD_PAD:h * HEAD_PAD + LANES] = (nope * r * gn_ref[...]).astype(BF16)
        o_ref[:, h * HEAD_PAD + LANES:(h + 1) * HEAD_PAD] = (rope * r * gr_ref[...]).astype(BF16)


def _qproj(cqn, wq_a, wq_b, cos128, sin128, g_nope, g_rope, heads):
    t, ql = cqn.shape
    tm = _pick(t, (256, 128))
    row = lambda w: pl.BlockSpec((tm, w), lambda i: (i, 0))
    const = lambda a, b: pl.BlockSpec((a, b), lambda i: (0, 0))
    return pl.pallas_call(
        functools.partial(_q_kernel, heads=heads),
        out_shape=jax.ShapeDtypeStruct((t, heads * HEAD_PAD), BF16),
        grid=(t // tm,),
        in_specs=[row(ql), const(ql, heads * HEAD_PAD), const(ql, heads * LANES),
                  row(LANES), row(LANES), const(1, LANES), const(1, LANES)],
        out_specs=row(heads * HEAD_PAD),
        compiler_params=_cp("arbitrary"),
        name="q_proj",
    )(cqn, wq_a, wq_b, cos128, sin128, g_nope, g_rope)


def _kv_kernel(lat_ref, kr_ref, w_ref, k_ref, v_ref, *, heads):
    acc = _dot(lat_ref[...], w_ref[...])
    kr = kr_ref[...].astype(F32)
    kr2 = jnp.sum(kr * kr, axis=-1, keepdims=True)
    for h in range(heads):
        kn = acc[:, h * LANES:(h + 1) * LANES]
        ss = (jnp.sum(kn * kn, axis=-1, keepdims=True) + kr2) * (1.0 / QK_DIM)
        r = lax.rsqrt(ss + EPS)
        k_ref[:, h * HEAD_PAD:h * HEAD_PAD + LANES] = (kn * r).astype(BF16)
        k_ref[:, h * HEAD_PAD + LANES:(h + 1) * HEAD_PAD] = (kr * r).astype(BF16)
    v_ref[...] = acc[:, heads * LANES:].astype(BF16)


def _kvproj(latk, w_kv, tp, heads, kvl):
    tm = _pick(tp, (256, 128))
    return pl.pallas_call(
        functools.partial(_kv_kernel, heads=heads),
        out_shape=(jax.ShapeDtypeStruct((tp, heads * HEAD_PAD), BF16),
                   jax.ShapeDtypeStruct((tp, heads * V_DIM), BF16)),
        grid=(tp // tm,),
        in_specs=[pl.BlockSpec((tm, kvl), lambda i: (i, 0)),
                  pl.BlockSpec((tm, LANES), lambda i: (i, kvl // LANES)),
                  pl.BlockSpec((kvl, 2 * heads * LANES), lambda i: (0, 0))],
        out_specs=(pl.BlockSpec((tm, heads * HEAD_PAD), lambda i: (i, 0)),
                   pl.BlockSpec((tm, heads * V_DIM), lambda i: (i, 0))),
        compiler_params=_cp("arbitrary"),
        name="kv_proj",
    )(latk, latk, w_kv)


def _attn_p_kernel(q_ref, k_ref, v_ref, o_ref, m_ref, l_ref, acc_ref, *, tq):
    qi = pl.program_id(2)
    ki = pl.program_id(3)

    @pl.when(ki == 0)
    def _():
        m_ref[...] = jnp.full(m_ref.shape, -jnp.inf, F32)
        l_ref[...] = jnp.zeros(l_ref.shape, F32)
        acc_ref[...] = jnp.zeros(acc_ref.shape, F32)

    def step(masked):
        s = lax.dot_general(q_ref[...], k_ref[...], _NT, preferred_element_type=F32)
        if masked:
            rc = lax.broadcasted_iota(I32, (tq, tq), 0) // CHUNK
            cc = lax.broadcasted_iota(I32, (tq, tq), 1) // CHUNK
            s = jnp.where(cc <= rc, s, -jnp.inf)
        m_prev = m_ref[...]
        m_new = jnp.maximum(m_prev, jnp.max(s, axis=-1, keepdims=True))
        alpha = jnp.exp(m_prev - m_new)
        p = jnp.exp(s - m_new)
        l_ref[...] = alpha * l_ref[...] + jnp.sum(p, axis=-1, keepdims=True)
        acc_ref[...] = alpha * acc_ref[...] + _dot(p.astype(BF16), v_ref[...])
        m_ref[...] = m_new

    pl.when(ki < qi)(lambda: step(False))
    pl.when(ki == qi)(lambda: step(True))

    @pl.when(ki == pl.num_programs(3) - 1)
    def _():
        o_ref[...] = (acc_ref[...] / l_ref[...]).astype(o_ref.dtype)


def _attn_prompt(q_all, k_p, v_p, batch, seq, heads):
    tq = _pick(seq, (512, 256, 128, 64))
    nq = seq // tq
    kv_idx = lambda b, h, qi, ki: (b * nq + jnp.minimum(ki, qi), h)
    return pl.pallas_call(
        functools.partial(_attn_p_kernel, tq=tq),
        out_shape=jax.ShapeDtypeStruct((batch * seq, heads * V_DIM), BF16),
        grid=(batch, heads, nq, nq),
        in_specs=[pl.BlockSpec((tq, HEAD_PAD), lambda b, h, qi, ki: (b * nq + qi, h)),
                  pl.BlockSpec((tq, HEAD_PAD), kv_idx),
                  pl.BlockSpec((tq, V_DIM), kv_idx)],
        out_specs=pl.BlockSpec((tq, V_DIM), lambda b, h, qi, ki: (b * nq + qi, h)),
        scratch_shapes=[pltpu.VMEM((tq, 1), F32), pltpu.VMEM((tq, 1), F32),
                        pltpu.VMEM((tq, V_DIM), F32)],
        compiler_params=_cp("arbitrary", "arbitrary", "arbitrary", "arbitrary"),
        name="attn_prompt",
    )(q_all, k_p, v_p)


def _attn_s_kernel(pl_ref, pk_ref, nl_ref, q_ref, wukt_ref, wuv_ref, o_ref,
                   qabs_ref, qr_ref, m_ref, l_ref, acc_ref, *, heads, kvl, nkb):
    kb = pl.program_id(1)

    @pl.when(kb == 0)
    def _():
        for h in range(heads):
            qn = q_ref[:, h * HEAD_PAD:h * HEAD_PAD + LANES]
            qabs_ref[h * CHUNK:(h + 1) * CHUNK, :] = _dot(
                qn, wukt_ref[h * LANES:(h + 1) * LANES, :]).astype(BF16)
            qr_ref[h * CHUNK:(h + 1) * CHUNK, :] = q_ref[:, h * HEAD_PAD + LANES:(h + 1) * HEAD_PAD]
        m_ref[...] = jnp.full(m_ref.shape, -jnp.inf, F32)
        l_ref[...] = jnp.zeros(l_ref.shape, F32)
        acc_ref[...] = jnp.zeros(acc_ref.shape, F32)

    def block(xl, krf):
        rtop = lax.dot_general(wukt_ref[...], xl, _NT, preferred_element_type=F32)
        sq = krf * krf
        sq_hi = sq.astype(BF16)
        sq_lo = (sq - sq_hi.astype(F32)).astype(BF16)
        ones = jnp.ones((SUBLANES, ROPE_DIM), BF16)
        kr2 = (lax.dot_general(ones, sq_hi, _NT, preferred_element_type=F32)
               + lax.dot_general(ones, sq_lo, _NT, preferred_element_type=F32))[0:1, :]
        s = (lax.dot_general(qabs_ref[...], xl, _NT, preferred_element_type=F32)
             + lax.dot_general(qr_ref[:, :ROPE_DIM], krf.astype(BF16), _NT,
                               preferred_element_type=F32))
        parts = []
        for h in range(heads):
            rt = rtop[h * LANES:(h + 1) * LANES, :]
            kn2 = jnp.sum(rt * rt, axis=0, keepdims=True)
            r = lax.rsqrt((kn2 + kr2) * (1.0 / QK_DIM) + EPS)
            parts.append(s[h * CHUNK:(h + 1) * CHUNK, :] * r)
        s = jnp.concatenate(parts, axis=0)
        m_prev = m_ref[...]
        m_new = jnp.maximum(m_prev, jnp.max(s, axis=-1, keepdims=True))
        alpha = jnp.exp(m_prev - m_new)
        p = jnp.exp(s - m_new)
        l_ref[...] = alpha * l_ref[...] + jnp.sum(p, axis=-1, keepdims=True)
        acc_ref[...] = alpha * acc_ref[...] + _dot(p.astype(BF16), xl)
        m_ref[...] = m_new

    @pl.when(kb < nkb)
    def _():
        block(pl_ref[0].astype(BF16), pk_ref[0])

    @pl.when(kb == nkb)
    def _():
        block(nl_ref[:, :kvl], nl_ref[:, kvl:kvl + ROPE_DIM].astype(F32))
        o = (acc_ref[...] / l_ref[...]).astype(BF16)
        for h in range(heads):
            o_ref[:, h * V_DIM:(h + 1) * V_DIM] = _dot(
                o[h * CHUNK:(h + 1) * CHUNK, :], wuv_ref[:, h * V_DIM:(h + 1) * V_DIM]).astype(o_ref.dtype)


def _attn_sample(past_lat, past_kr, latk, q_all, w_uk_t, w_uv2, tp, heads):
    bs, past, kvl = past_lat.shape
    tk = _pick(past, (512, 256, 128))
    nkb = past // tk
    c0 = tp // CHUNK
    pidx = lambda b, kb: (b, jnp.minimum(kb, nkb - 1), 0)
    hq = heads * CHUNK
    return pl.pallas_call(
        functools.partial(_attn_s_kernel, heads=heads, kvl=kvl, nkb=nkb),
        out_shape=jax.ShapeDtypeStruct((bs * CHUNK, heads * V_DIM), BF16),
        grid=(bs, nkb + 1),
        in_specs=[pl.BlockSpec((1, tk, kvl), pidx),
                  pl.BlockSpec((1, tk, ROPE_DIM), pidx),
                  pl.BlockSpec((CHUNK, kvl + LANES), lambda b, kb: (c0 + b, 0)),
                  pl.BlockSpec((CHUNK, heads * HEAD_PAD), lambda b, kb: (c0 + b, 0)),
                  pl.BlockSpec((heads * LANES, kvl), lambda b, kb: (0, 0)),
                  pl.BlockSpec((kvl, heads * V_DIM), lambda b, kb: (0, 0))],
        out_specs=pl.BlockSpec((CHUNK, heads * V_DIM), lambda b, kb: (b, 0)),
        scratch_shapes=[pltpu.VMEM((hq, kvl), BF16), pltpu.VMEM((hq, LANES), BF16),
                        pltpu.VMEM((hq, 1), F32), pltpu.VMEM((hq, 1), F32),
                        pltpu.VMEM((hq, kvl), F32)],
        compiler_params=_cp("arbitrary", "arbitrary"),
        name="attn_sample",
    )(past_lat, past_kr, latk, q_all, w_uk_t, w_uv2)


def _ssd_kernel(z_ref, x_ref, dt_ref, dtt_ref, conv0_ref, ssm0_ref, cw_ref, cb_ref, alr_ref, alc_ref,
                dsk_ref, gs_ref, eh_ref, o_ref, st_out_ref, xs_ref, st_ref,
                *, lc, lr, nh, groups):
    c = pl.program_id(1)
    p = M_HEADDIM
    n = D_STATE
    ci = nh * p
    k8 = nh // groups
    gw = k8 * p

    @pl.when(c == 0)
    def _():
        xs_ref[0:SUBLANES, :] = conv0_ref[0]
        st_ref[...] = ssm0_ref[0]

    xs_ref[SUBLANES:SUBLANES + lr, :] = x_ref[...].astype(F32)
    if lc > lr:
        xs_ref[SUBLANES + lr:SUBLANES + lc, :] = jnp.zeros((lc - lr, xs_ref.shape[1]), F32)

    def conv(lo, hi):
        u = xs_ref[SUBLANES - 3:SUBLANES - 3 + lc, lo:hi] * cw_ref[0:1, lo:hi]
        for tap in range(1, CONV_W):
            u = u + xs_ref[SUBLANES - 3 + tap:SUBLANES - 3 + tap + lc, lo:hi] * cw_ref[tap:tap + 1, lo:hi]
        return _silu(u + cb_ref[:, lo:hi])

    dt = dt_ref[0]
    dtt = dtt_ref[0]
    a_row = -jnp.exp(alr_ref[...])
    a_col = -jnp.exp(alc_ref[...])
    ri = lax.broadcasted_iota(I32, (lc, lc), 0)
    cidx = lax.broadcasted_iota(I32, (lc, lc), 1)
    tri = ri >= cidx
    tril = jnp.where(tri, 1.0, 0.0).astype(BF16)
    triu = jnp.where(ri <= cidx, 1.0, 0.0).astype(BF16)
    cs = sum(_dot(tril, piece) for piece in _split3(dt * a_row))
    cst = sum(_dot(piece, triu) for piece in _split3(dtt * a_col))
    exp_cs = jnp.exp(cs)
    w_end = jnp.exp(cs[lc - 1:lc, :] - cs)
    stacked = jnp.concatenate([dt, exp_cs, w_end], axis=0)
    eh = eh_ref[...]
    expanded = sum(_dot(piece, eh) for piece in _split3(stacked))
    dt_e, ecs_e, wend_e = expanded[:lc], expanded[lc:2 * lc], expanded[2 * lc:]
    cdec = jnp.exp(cst[:, lc - 1:lc])
    lane_lo = lax.broadcasted_iota(I32, (lc, LANES), 1) < p

    for g in range(groups):
        gs = slice(g * gw, (g + 1) * gw)
        xg = conv(g * gw, (g + 1) * gw)
        bg = conv(ci + g * n, ci + (g + 1) * n).astype(BF16)
        cg = conv(ci + groups * n + g * n, ci + groups * n + (g + 1) * n).astype(BF16)
        cbm = lax.dot_general(cg, bg, _NT, preferred_element_type=F32)
        xdt = xg * dt_e[:, gs]
        xdt_b = xdt.astype(BF16)
        pairs = []
        for q in range(k8 // 2):
            x2 = xdt_b[:, q * LANES:(q + 1) * LANES]
            ys = []
            for s in range(2):
                h = g * k8 + 2 * q + s
                seg = cs[:, h:h + 1] - cst[h:h + 1, :]
                dec = jnp.exp(jnp.where(tri, seg, -jnp.inf))
                ys.append(_dot((cbm * dec).astype(BF16), x2))
            pairs.append(jnp.where(lane_lo, ys[0], ys[1]))
        y_diag = jnp.concatenate(pairs, axis=1)
        sg = st_ref[g * gw:(g + 1) * gw, :]
        y_off = lax.dot_general(cg, sg.astype(BF16), _NT, preferred_element_type=F32) * ecs_e[:, gs]
        y = y_diag + y_off + xg * dsk_ref[:, gs]
        xw = (xdt * wend_e[:, gs]).astype(BF16)
        upd = lax.dot_general(xw, bg, _TN, preferred_element_type=F32)
        for k in range(k8):
            h = g * k8 + k
            rows = slice(g * gw + k * p, g * gw + (k + 1) * p)
            st_ref[rows, :] = st_ref[rows, :] * cdec[h:h + 1, :] + upd[k * p:(k + 1) * p, :]
        zg = z_ref[:, gs].astype(F32)
        u2 = y[:lr] * _silu(zg)
        ms = jnp.mean(u2 * u2, axis=-1, keepdims=True)
        o_ref[:, gs] = (u2 * lax.rsqrt(ms + EPS) * gs_ref[:, gs]).astype(o_ref.dtype)

    xs_ref[0:SUBLANES, :] = xs_ref[lr:lr + SUBLANES, :]

    @pl.when(c == pl.num_programs(1) - 1)
    def _():
        st_out_ref[0] = st_ref[...]


def _ssd(z_all, xbc_all, dt3, dtt3, conv0p, ssm0, conv_w, conv_b, a_log, dsk, g_ssm, eh,
         row0, nseq, nchunk, lc, lr, groups):
    ci = z_all.shape[1]
    cc = xbc_all.shape[1]
    nh = ci // M_HEADDIM
    rb0 = row0 // lr
    rowblk = lambda w: pl.BlockSpec((lr, w), lambda b, c: (rb0 + b * nchunk + c, 0))
    seq3 = lambda a, b_: pl.BlockSpec((1, a, b_), lambda b, c: (b * nchunk + c, 0, 0))
    perb = lambda a, b_: pl.BlockSpec((1, a, b_), lambda b, c: (b, 0, 0))
    const = lambda a, b_: pl.BlockSpec((a, b_), lambda b, c: (0, 0))
    return pl.pallas_call(
        functools.partial(_ssd_kernel, lc=lc, lr=lr, nh=nh, groups=groups),
        out_shape=(jax.ShapeDtypeStruct((nseq * nchunk * lr, ci), BF16),
                   jax.ShapeDtypeStruct((nseq, ci, D_STATE), F32)),
        grid=(nseq, nchunk),
        in_specs=[rowblk(ci), rowblk(cc), seq3(lc, nh), seq3(nh, lc),
                  perb(SUBLANES, cc), perb(ci, D_STATE),
                  const(CONV_W, cc), const(1, cc), const(1, nh), const(nh, 1),
                  const(1, ci), const(1, ci), const(nh, ci)],
        out_specs=(pl.BlockSpec((lr, ci), lambda b, c: (b * nchunk + c, 0)),
                   perb(ci, D_STATE)),
        scratch_shapes=[pltpu.VMEM((lc + SUBLANES, cc), F32), pltpu.VMEM((ci, D_STATE), F32)],
        compiler_params=_cp("arbitrary", "arbitrary"),
        name="ssd",
    )(z_all, xbc_all, dt3, dtt3, conv0p, ssm0, conv_w, conv_b.reshape(1, cc),
      a_log.reshape(1, nh), a_log.reshape(nh, 1), dsk, g_ssm.reshape(1, ci), eh)


def _mix_kernel(ap_ref, as_ref, mp_ref, ms_ref, ga_ref, gb_ref, wpa_ref, wpb_ref, o_ref, *, npb):
    i = pl.program_id(0)
    a = jnp.where(i < npb, ap_ref[...], as_ref[...])
    m = jnp.where(i < npb, mp_ref[...], ms_ref[...])
    pa = _dot(a, wpa_ref[...])
    pb = _dot(m, wpb_ref[...])
    o_ref[...] = (ga_ref[...].astype(F32) * pa + gb_ref[...].astype(F32) * pb).astype(o_ref.dtype)


def _mix(a_p, a_s, m_p, m_s, gates, w_pa, w_pb):
    tp, hv = a_p.shape
    ts = a_s.shape[0]
    ci = m_p.shape[1]
    d = w_pa.shape[1]
    tm = _pick(math.gcd(tp, ts), (512, 256, 128, 64))
    tn = _pick(d, (512, 256, 128))
    npb = tp // tm
    nj = d // tn
    pidx = lambda i, j: (jnp.minimum(i, npb - 1), 0)
    sidx = lambda i, j: (jnp.maximum(i - npb, 0), 0)
    return pl.pallas_call(
        functools.partial(_mix_kernel, npb=npb),
        out_shape=jax.ShapeDtypeStruct((tp + ts, d), BF16),
        grid=((tp + ts) // tm, nj),
        in_specs=[pl.BlockSpec((tm, hv), pidx), pl.BlockSpec((tm, hv), sidx),
                  pl.BlockSpec((tm, ci), pidx), pl.BlockSpec((tm, ci), sidx),
                  pl.BlockSpec((tm, tn), lambda i, j: (i, j)),
                  pl.BlockSpec((tm, tn), lambda i, j: (i, nj + j)),
                  pl.BlockSpec((hv, tn), lambda i, j: (0, j)),
                  pl.BlockSpec((ci, tn), lambda i, j: (0, j))],
        out_specs=pl.BlockSpec((tm, tn), lambda i, j: (i, j)),
        compiler_params=_cp("arbitrary", "arbitrary"),
        name="branch_mix",
    )(a_p, a_s, m_p, m_s, gates, gates, w_pa, w_pb)


def _post_kernel(mx_ref, wout_ref, x_ref, gt1_ref, sc2_ref, sh2_ref, g2_ref, wrh_ref, wrl_ref, br_ref,
                 x1_ref, h2_ref, ti_ref, tp_ref, *, nexp):
    o = _dot(mx_ref[...], wout_ref[...])
    tm = o.shape[0]
    for ch in range(tm // CHUNK):
        rows = slice(ch * CHUNK, (ch + 1) * CHUNK)
        x1 = x_ref[rows, :] + gt1_ref[ch] * o[rows, :]
        x1_ref[rows, :] = x1
        xn = x1 * lax.rsqrt(jnp.mean(x1 * x1, axis=-1, keepdims=True) + EPS)
        h2_ref[rows, :] = xn * g2_ref[...] * (1.0 + sc2_ref[ch]) + sh2_ref[ch]
    h2 = h2_ref[...]
    hh = h2.astype(BF16)
    hl = (h2 - hh.astype(F32)).astype(BF16)
    logits = _dot(hh, wrh_ref[...]) + _dot(hh, wrl_ref[...]) + _dot(hl, wrh_ref[...]) + br_ref[...]
    lane = lax.broadcasted_iota(I32, logits.shape, 1)
    logits = jnp.where(lane < nexp, logits, NEG_BIG)
    vals, idxs = [], []
    for _ in range(TOP_K):
        m = jnp.max(logits, axis=-1, keepdims=True)
        idx = jnp.min(jnp.where(logits == m, lane, LANES), axis=-1, keepdims=True)
        vals.append(m)
        idxs.append(idx)
        logits = jnp.where(lane == idx, 2.0 * NEG_BIG, logits)
    es = [jnp.exp(v - vals[0]) for v in vals]
    den = es[0]
    for e in es[1:]:
        den = den + e
    ti = jnp.zeros(lane.shape, I32)
    tpv = jnp.zeros(lane.shape, F32)
    for k in range(TOP_K):
        ti = jnp.where(lane == k, idxs[k], ti)
        tpv = jnp.where(lane == k, es[k] / den, tpv)
    ti_ref[...] = ti
    tp_ref[...] = tpv


def _post(mixed, w_out, x_all, gt1, sc2, sh2, g2, wr_hi, wr_lo, br, nexp):
    t, d = mixed.shape
    tm = _pick(t, (256, 128, 64))
    nc = tm // CHUNK
    row = lambda w: pl.BlockSpec((tm, w), lambda i: (i, 0))
    mod = pl.BlockSpec((nc, 1, d), lambda i: (i, 0, 0))
    const = lambda a, b: pl.BlockSpec((a, b), lambda i: (0, 0))
    return pl.pallas_call(
        functools.partial(_post_kernel, nexp=nexp),
        out_shape=(jax.ShapeDtypeStruct((t, d), F32), jax.ShapeDtypeStruct((t, d), F32),
                   jax.ShapeDtypeStruct((t, LANES), I32), jax.ShapeDtypeStruct((t, LANES), F32)),
        grid=(t // tm,),
        in_specs=[row(d), const(d, d), row(d), mod, mod, mod, const(1, d),
                  const(d, LANES), const(d, LANES), const(1, LANES)],
        out_specs=(row(d), row(d), row(LANES), row(LANES)),
        compiler_params=_cp("arbitrary"),
        name="post_mix",
    )(mixed, w_out, x_all, gt1, sc2, sh2, g2.reshape(1, d), wr_hi, wr_lo, br)


def _row_gather(idx_vmem, idx_smem, isem, src_ref, buf_ref, sem, nrows):
    icp = pltpu.make_async_copy(idx_vmem, idx_smem, isem)
    icp.start()
    icp.wait()

    def start(r, carry):
        pltpu.make_async_copy(src_ref.at[pl.ds(idx_smem[r], 1)], buf_ref.at[pl.ds(r, 1)], sem).start()
        return carry

    lax.fori_loop(0, nrows, start, 0)
    pltpu.make_async_copy(src_ref.at[pl.ds(0, nrows)], buf_ref, sem).wait()


def _gather_kernel(nu_ref, idx_ref, src_ref, o_ref, buf_ref, idx_smem, isem, sem):
    b = pl.program_id(0)

    @pl.when(b < nu_ref[0])
    def _():
        _row_gather(idx_ref.at[0, 0], idx_smem, isem, src_ref, buf_ref, sem, buf_ref.shape[0])
        o_ref[...] = buf_ref[...].astype(o_ref.dtype)

    @pl.when(b >= nu_ref[0])
    def _():
        o_ref[...] = jnp.zeros(o_ref.shape, o_ref.dtype)


def _moe_gather(n_used, slot_tok3, h2):
    nb, _, bm = slot_tok3.shape
    d = h2.shape[1]
    return pl.pallas_call(
        _gather_kernel,
        out_shape=jax.ShapeDtypeStruct((nb * bm, d), BF16),
        grid_spec=pltpu.PrefetchScalarGridSpec(
            num_scalar_prefetch=1,
            grid=(nb,),
            in_specs=[pl.BlockSpec((1, 1, bm), lambda b, nu: (b, 0, 0)),
                      pl.BlockSpec(memory_space=pl.ANY)],
            out_specs=pl.BlockSpec((bm, d), lambda b, nu: (b, 0)),
            scratch_shapes=[pltpu.VMEM((bm, d), F32), pltpu.SMEM((bm,), I32),
                            pltpu.SemaphoreType.DMA, pltpu.SemaphoreType.DMA]),
        compiler_params=_cp("arbitrary"),
        name="moe_gather",
    )(n_used, slot_tok3, h2)


def _expert_changed(be_ref, b):
    return jnp.logical_or(b == 0, be_ref[b] != be_ref[jnp.maximum(b - 1, 0)])


def _gateup_kernel(be_ref, nu_ref, x_ref, wg_ref, wu_ref, bg_ref, bu_ref, o_ref, wgb_ref, wub_ref):
    b = pl.program_id(1)
    used = b < nu_ref[0]

    @pl.when(jnp.logical_and(used, _expert_changed(be_ref, b)))
    def _():
        wgb_ref[...] = wg_ref[0].astype(BF16)
        wub_ref[...] = wu_ref[0].astype(BF16)

    @pl.when(used)
    def _():
        x = x_ref[...]
        gate = jnp.minimum(_dot(x, wgb_ref[...]) + bg_ref[0], SWIGLU_LIMIT)
        up = jnp.clip(_dot(x, wub_ref[...]) + bu_ref[0], -SWIGLU_LIMIT, SWIGLU_LIMIT)
        glu = gate * jax.nn.sigmoid(SWIGLU_ALPHA * gate)
        o_ref[...] = ((up + 1.0) * glu).astype(o_ref.dtype)

    @pl.when(jnp.logical_not(used))
    def _():
        o_ref[...] = jnp.zeros(o_ref.shape, o_ref.dtype)


def _moe_gateup(block_e, n_used, xg, w_gate, w_up, b_gate, b_up):
    ns, d = xg.shape
    nexp, _, ff = w_gate.shape
    bm = MOE_BLOCK
    nb = ns // bm
    tf = _pick(ff, (1024, 512, 256, 128))
    wspec = pl.BlockSpec((1, d, tf), lambda c, b, be, nu: (be[b], 0, c))
    bspec = pl.BlockSpec((1, 1, tf), lambda c, b, be, nu: (be[b], 0, c))
    return pl.pallas_call(
        _gateup_kernel,
        out_shape=jax.ShapeDtypeStruct((ns, ff), BF16),
        grid_spec=pltpu.PrefetchScalarGridSpec(
            num_scalar_prefetch=2,
            grid=(ff // tf, nb),
            in_specs=[pl.BlockSpec((bm, d), lambda c, b, be, nu: (jnp.minimum(b, nu[0] - 1), 0)),
                      wspec, wspec, bspec, bspec],
            out_specs=pl.BlockSpec((bm, tf), lambda c, b, be, nu: (b, c)),
            scratch_shapes=[pltpu.VMEM((d, tf), BF16), pltpu.VMEM((d, tf), BF16)]),
        compiler_params=_cp("arbitrary", "arbitrary"),
        name="moe_gateup",
    )(block_e, n_used, xg, w_gate, w_up, b_gate.reshape(nexp, 1, ff), b_up.reshape(nexp, 1, ff))


def _down_kernel(be_ref, nu_ref, g_ref, wd_ref, bd_ref, p_ref, o_ref, wdb_ref):
    b = pl.program_id(1)
    used = b < nu_ref[0]

    @pl.when(jnp.logical_and(used, _expert_changed(be_ref, b)))
    def _():
        wdb_ref[...] = wd_ref[0].astype(BF16)

    @pl.when(used)
    def _():
        o_ref[...] = (_dot(g_ref[...], wdb_ref[...]) + bd_ref[0]) * p_ref[...]

    @pl.when(jnp.logical_not(used))
    def _():
        o_ref[...] = jnp.zeros(o_ref.shape, o_ref.dtype)


def _moe_down(block_e, n_used, glu, w_down, b_down, slot_p):
    ns, ff = glu.shape
    nexp, _, d = w_down.shape
    bm = MOE_BLOCK
    nb = ns // bm
    tn = _pick(d, (1024, 512, 256, 128))
    return pl.pallas_call(
        _down_kernel,
        out_shape=jax.ShapeDtypeStruct((ns, d), F32),
        grid_spec=pltpu.PrefetchScalarGridSpec(
            num_scalar_prefetch=2,
            grid=(d // tn, nb),
            in_specs=[pl.BlockSpec((bm, ff), lambda c, b, be, nu: (jnp.minimum(b, nu[0] - 1), 0)),
                      pl.BlockSpec((1, ff, tn), lambda c, b, be, nu: (be[b], 0, c)),
                      pl.BlockSpec((1, 1, tn), lambda c, b, be, nu: (be[b], 0, c)),
                      pl.BlockSpec((bm, 1), lambda c, b, be, nu: (jnp.minimum(b, nu[0] - 1), 0))],
            out_specs=pl.BlockSpec((bm, tn), lambda c, b, be, nu: (b, c)),
            scratch_shapes=[pltpu.VMEM((ff, tn), BF16)]),
        compiler_params=_cp("arbitrary", "arbitrary"),
        name="moe_down",
    )(block_e, n_used, glu, w_down, b_down.reshape(nexp, 1, d), slot_p)


def _combine_kernel(idx_ref, yb_ref, x1_ref, gt2_ref, o_ref, buf_ref, idx_smem, isem, sem, *, tc):
    _row_gather(idx_ref.at[0, 0], idx_smem, isem, yb_ref, buf_ref, sem, TOP_K * tc)
    ff = buf_ref[0:tc, :]
    for k in range(1, TOP_K):
        ff = ff + buf_ref[k * tc:(k + 1) * tc, :]
    for ch in range(tc // CHUNK):
        rows = slice(ch * CHUNK, (ch + 1) * CHUNK)
        o_ref[rows, :] = x1_ref[rows, :] + gt2_ref[ch] * ff[rows, :]


def _moe_combine(inv3, yb, x1, gt2, row0, nrows, tc):
    d = yb.shape[1]
    nc = tc // CHUNK
    rb0 = row0 // tc
    return pl.pallas_call(
        functools.partial(_combine_kernel, tc=tc),
        out_shape=jax.ShapeDtypeStruct((nrows, d), F32),
        grid=(nrows // tc,),
        in_specs=[pl.BlockSpec((1, 1, TOP_K * tc), lambda i: (rb0 + i, 0, 0)),
                  pl.BlockSpec(memory_space=pl.ANY),
                  pl.BlockSpec((tc, d), lambda i: (rb0 + i, 0)),
                  pl.BlockSpec((nc, 1, d), lambda i: (rb0 + i, 0, 0))],
        out_specs=pl.BlockSpec((tc, d), lambda i: (i, 0)),
        scratch_shapes=[pltpu.VMEM((TOP_K * tc, d), F32), pltpu.SMEM((TOP_K * tc,), I32),
                        pltpu.SemaphoreType.DMA, pltpu.SemaphoreType.DMA],
        compiler_params=_cp("arbitrary"),
        name="moe_combine",
    )(inv3, yb, x1, gt2)


def _route(top_idx, top_p, nexp, bm):
    t = top_idx.shape[0]
    tk = t * TOP_K
    n_blocks = -(-(tk + nexp * (bm - 1)) // bm)
    n_slots = n_blocks * bm
    flat_e = top_idx.reshape(tk)
    order = jnp.argsort(flat_e, stable=True)
    sorted_e = flat_e[order]
    counts = jnp.bincount(flat_e, length=nexp)
    padded = (counts + bm - 1) // bm * bm
    pad_end = jnp.cumsum(padded)
    pad_start = pad_end - padded
    start = jnp.cumsum(counts) - counts
    dest = (pad_start[sorted_e] + jnp.arange(tk) - start[sorted_e]).astype(I32)
    slot_tok = jnp.zeros((n_slots,), I32).at[dest].set((order // TOP_K).astype(I32))
    slot_p = jnp.zeros((n_slots,), F32).at[dest].set(top_p.reshape(tk)[order])
    n_used = (pad_end[-1] // bm).astype(I32)
    blk = jnp.minimum(jnp.arange(n_blocks), n_used - 1)
    block_e = jnp.minimum(jnp.searchsorted(pad_end, blk * bm, side="right"), nexp - 1).astype(I32)
    inv = jnp.zeros((tk,), I32).at[order].set(dest).reshape(t, TOP_K)
    return slot_tok, slot_p, block_e, n_used.reshape(1), inv


def _rope_tables(pos):
    half = ROPE_DIM // 2
    inv = ROPE_THETA ** (-jnp.arange(half, dtype=F32) / half)
    ang = pos.astype(F32)[:, None] * inv[None, :]
    z = jnp.zeros((pos.shape[0], LANES - ROPE_DIM), F32)
    cos, sin = jnp.cos(ang), jnp.sin(ang)
    return jnp.concatenate([cos, cos, z], axis=1), jnp.concatenate([sin, sin, z], axis=1)


def _rot_half_cols(w):
    half = ROPE_DIM // 2
    return jnp.concatenate([-w[..., half:], w[..., :half]], axis=-1)


def _layer(x_prompt, x_sample, past_lat, past_kr, ssm_s0, conv_s0, c_prompt, c_sample,
           w_ada, b_ada, g_norm1, w_in, g_cq, g_ckv, w_uq, w_uk, w_uv, g_qn, g_kn, conv_w, conv_b, dt_bias,
           a_log, d_skip, g_ssm, w_pa, w_pb, w_out, g_norm2, w_router, b_router, w_gate, b_gate, w_up, b_up,
           w_down, b_down):
    bp, sp, d = x_prompt.shape
    bs, ss, _ = x_sample.shape
    assert ss == CHUNK and sp % CHUNK == 0
    past = past_lat.shape[1]
    ql, kvl = g_cq.shape[-1], g_ckv.shape[-1]
    heads = w_uq.shape[1]
    ci = g_ssm.shape[-1]
    cc = conv_w.shape[-1]
    nh = ci // M_HEADDIM
    groups = (cc - ci) // (2 * D_STATE)
    nexp = w_router.shape[-1]
    tp, ts = bp * sp, bs * ss
    t = tp + ts
    ncp, ncs = tp // CHUNK, ts // CHUNK

    ada = _ada(jnp.concatenate([c_prompt, c_sample], axis=0), w_ada, b_ada)
    per_chunk = jnp.concatenate([jnp.repeat(ada[:bp], sp // CHUNK, axis=0), ada[bp:]], axis=0)
    sh1, sc1, gt1, sh2, sc2, gt2 = [m[:, None, :] for m in jnp.split(per_chunk, 6, axis=-1)]

    h3, x3 = _norm1(x_prompt.reshape(ncp, CHUNK, d), x_sample.reshape(ncs, CHUNK, d), g_norm1, sc1, sh1)
    h_all = h3.reshape(t, d)
    x_all = x3.reshape(t, d)

    o = 0
    w_cq = w_in[:, o:o + ql]; o += ql
    w_ckv = w_in[:, o:o + kvl]; o += kvl
    w_kr = w_in[:, o:o + ROPE_DIM]; o += ROPE_DIM
    w_z = w_in[:, o:o + ci]; o += ci
    w_xbc = w_in[:, o:o + cc]; o += cc
    w_dt = w_in[:, o:o + nh]; o += nh
    w_gab = w_in[:, o:o + 2 * d]
    zc = lambda n_: jnp.zeros((d, n_), F32)
    w_lat = jnp.concatenate([w_cq, w_ckv, w_kr, zc(LANES - ROPE_DIM), _rot_half_cols(w_kr),
                             zc(LANES - ROPE_DIM), w_dt, zc(LANES - nh)], axis=1).astype(BF16)

    pos = jnp.concatenate([jnp.tile(jnp.arange(sp), bp), jnp.tile(past + jnp.arange(ss), bs)])
    cos128, sin128 = _rope_tables(pos)

    cqn, lat_all, latk, kr_all, dt_all, dtt_all = _lat(
        h_all, w_lat, w_dt.T.astype(BF16), g_cq, g_ckv, cos128, sin128, dt_bias)
    z_all = _mm(h_all, w_z.astype(BF16), BF16, "proj_z")
    xbc_all = _mm(h_all, w_xbc.astype(BF16), BF16, "proj_xbc")
    gates = _mm(h_all, w_gab.astype(BF16), BF16, "proj_gates", act="sigmoid")

    tail = CONV_W - 1
    h_tail = jnp.concatenate([h3[:ncp].reshape(bp, sp, d)[:, sp - tail:, :].reshape(bp * tail, d),
                              h3[ncp:][:, CHUNK - tail:, :].reshape(bs * tail, d)], axis=0)
    conv_tail = _mm(h_tail, w_xbc.astype(BF16), F32, "proj_conv_tail")
    conv_p = conv_tail[:bp * tail].reshape(bp, tail, cc)
    conv_s = conv_tail[bp * tail:].reshape(bs, tail, cc)

    wq_a = jnp.concatenate([w_uq, jnp.zeros((ql, heads, HEAD_PAD - QK_DIM), F32)], axis=-1)
    wq_a = wq_a.reshape(ql, heads * HEAD_PAD).astype(BF16)
    wq_b = jnp.concatenate([_rot_half_cols(w_uq[..., NOPE_DIM:]),
                            jnp.zeros((ql, heads, LANES - ROPE_DIM), F32)], axis=-1)
    wq_b = wq_b.reshape(ql, heads * LANES).astype(BF16)
    gq = g_qn * g_kn * ATTN_SCALE
    g_nope = gq[:NOPE_DIM].reshape(1, LANES)
    g_rope = jnp.concatenate([gq[NOPE_DIM:], jnp.zeros((LANES - ROPE_DIM,), F32)]).reshape(1, LANES)
    q_all = _qproj(cqn, wq_a, wq_b, cos128, sin128, g_nope, g_rope, heads)

    w_uk2 = w_uk.reshape(kvl, heads * NOPE_DIM)
    w_uv2 = w_uv.reshape(kvl, heads * V_DIM).astype(BF16)
    w_kv = jnp.concatenate([w_uk2.astype(BF16), w_uv2], axis=1)
    k_p, v_p = _kvproj(latk, w_kv, tp, heads, kvl)
    a_p = _attn_prompt(q_all, k_p, v_p, bp, sp, heads)
    a_s = _attn_sample(past_lat, past_kr, latk, q_all, w_uk2.T.astype(BF16), w_uv2, tp, heads)

    eh = jnp.repeat(jnp.eye(nh, dtype=BF16), M_HEADDIM, axis=1)
    dsk = jnp.repeat(d_skip, M_HEADDIM).reshape(1, ci)
    lcp = _pick(sp, (256, 128))
    assert sp % lcp == 0 and lcp % LANES == 0
    ncq = sp // lcp
    dt3_p = dt_all[:tp].reshape(bp * ncq, lcp, nh)
    dtt3_p = dtt_all[:, :tp].reshape(nh, bp * ncq, lcp).transpose(1, 0, 2)
    zpad = lambda a, axis: jnp.concatenate([a, jnp.zeros_like(a)], axis=axis)
    lcs = 2 * CHUNK
    dt3_s = zpad(dt_all[tp:].reshape(bs, CHUNK, nh), 1)
    dtt3_s = zpad(dtt_all[:, tp:].reshape(nh, bs, CHUNK).transpose(1, 0, 2), 2)
    pad_conv = lambda c0: jnp.concatenate(
        [jnp.zeros((c0.shape[0], SUBLANES - tail, cc), F32), c0], axis=1)
    m_p, ssm_p = _ssd(z_all, xbc_all, dt3_p, dtt3_p, jnp.zeros((bp, SUBLANES, cc), F32),
                      jnp.zeros((bp, ci, D_STATE), F32), conv_w, conv_b, a_log, dsk, g_ssm, eh,
                      0, bp, ncq, lcp, lcp, groups)
    m_s, ssm_s = _ssd(z_all, xbc_all, dt3_s, dtt3_s, pad_conv(conv_s0),
                      ssm_s0.reshape(bs, ci, D_STATE), conv_w, conv_b, a_log, dsk, g_ssm, eh,
                      tp, bs, 1, lcs, CHUNK, groups)

    mixed = _mix(a_p, a_s, m_p, m_s, gates, w_pa.astype(BF16), w_pb.astype(BF16))
    wr = jnp.concatenate([w_router, jnp.zeros((d, LANES - nexp), F32)], axis=1)
    wr_hi = wr.astype(BF16)
    wr_lo = (wr - wr_hi.astype(F32)).astype(BF16)
    br = jnp.concatenate([b_router, jnp.zeros((LANES - nexp,), F32)]).reshape(1, LANES)
    x1, h2, ti, tpr = _post(mixed, w_out.astype(BF16), x_all, gt1, sc2, sh2, g_norm2, wr_hi, wr_lo, br, nexp)

    slot_tok, slot_p, block_e, n_used, inv = _route(ti[:, :TOP_K], tpr[:, :TOP_K], nexp, MOE_BLOCK)
    nb = slot_tok.shape[0] // MOE_BLOCK
    xg = _moe_gather(n_used, slot_tok.reshape(nb, 1, MOE_BLOCK), h2)
    glu = _moe_gateup(block_e, n_used, xg, w_gate, w_up, b_gate, b_up)
    yb = _moe_down(block_e, n_used, glu, w_down, b_down, slot_p.reshape(-1, 1))
    tc = _pick(math.gcd(tp, ts), (128, 64))
    inv3 = inv.reshape(t // tc, tc, TOP_K).transpose(0, 2, 1).reshape(t // tc, 1, TOP_K * tc)
    y_p = _moe_combine(inv3, yb, x1, gt2, 0, tp, tc)
    y_s = _moe_combine(inv3, yb, x1, gt2, tp, ts, tc)

    return (y_p.reshape(bp, sp, d), y_s.reshape(bs, ss, d),
            lat_all[:tp].reshape(bp, sp, kvl), kr_all[:tp].reshape(bp, sp, ROPE_DIM),
            ssm_p.reshape(bp, nh, M_HEADDIM, D_STATE), conv_p,
            lat_all[tp:].reshape(bs, ss, kvl), kr_all[tp:].reshape(bs, ss, ROPE_DIM),
            ssm_s.reshape(bs, nh, M_HEADDIM, D_STATE), conv_s)


def kernel(x_prompt, x_sample, cache_mla_latent, cache_mla_krope, state_ssm, state_conv, c_prompt, c_sample,
           w_ada, b_ada, g_norm1, w_in, g_cq, g_ckv, w_uq, w_uk, w_uv, g_qn, g_kn, conv_w, conv_b, dt_bias,
           a_log, d_skip, g_ssm, w_pa, w_pb, w_out, g_norm2, w_router, b_router, w_gate, b_gate, w_up, b_up,
           w_down, b_down):
    depth = w_ada.shape[0]
    assert depth == 1, "single-layer encoder"
    weights = (w_ada, b_ada, g_norm1, w_in, g_cq, g_ckv, w_uq, w_uk, w_uv, g_qn, g_kn, conv_w, conv_b, dt_bias,
               a_log, d_skip, g_ssm, w_pa, w_pb, w_out, g_norm2, w_router, b_router, w_gate, b_gate, w_up, b_up,
               w_down, b_down)
    outs = _layer(x_prompt, x_sample, cache_mla_latent[0], cache_mla_krope[0], state_ssm[0], state_conv[0],
                  c_prompt, c_sample, *[w[0] for w in weights])
    y_p, y_s = outs[0], outs[1]
    return (y_p, y_s) + tuple(o[None] for o in outs[2:6]) + tuple(o[None] for o in outs[6:])
```

```python
import functools
import math

import jax
import jax.numpy as jnp
from jax import lax
from jax.experimental import pallas as pl
from jax.experimental.pallas import tpu as pltpu

F32 = jnp.float32
BF16 = jnp.bfloat16
I32 = jnp.int32

CHUNK = 64
NOPE_DIM = 128
ROPE_DIM = 64
QK_DIM = NOPE_DIM + ROPE_DIM
V_DIM = 128
HEAD_PAD = 256
ROPE_THETA = 10000.0
ATTN_SCALE = QK_DIM ** -0.5
M_HEADDIM = 64
D_STATE = 128
CONV_W = 4
TOP_K = 4
SWIGLU_LIMIT = 7.0
SWIGLU_ALPHA = 1.702
EPS = 1e-6

LANES = 128
SUBLANES = 8
VMEM_LIMIT = 56 * 1024 * 1024

MOE_BLOCK = 256
NEG_BIG = -1e30

_NT = (((1,), (1,)), ((), ()))
_TN = (((0,), (0,)), ((), ()))


def _cp(*sem):
    return pltpu.CompilerParams(dimension_semantics=sem, vmem_limit_bytes=VMEM_LIMIT)


def _pick(n, prefs):
    for p in prefs:
        if n % p == 0:
            return p
    return n


def _dot(a, b):
    return jnp.dot(a, b, preferred_element_type=F32)


def _split3(v):
    hi = v.astype(BF16)
    r1 = v - hi.astype(F32)
    mid = r1.astype(BF16)
    lo = (r1 - mid.astype(F32)).astype(BF16)
    return hi, mid, lo


def _silu(x):
    return x * jax.nn.sigmoid(x)


def _softplus(x):
    return jnp.maximum(x, 0.0) + jnp.log1p(jnp.exp(-jnp.abs(x)))


def _ada_kernel(c_ref, w_ref, b_ref, o_ref):
    s = _silu(c_ref[...]).astype(BF16)
    o_ref[...] = _dot(s, w_ref[...].astype(BF16)) + b_ref[...]


def _ada(c_all, w_ada, b_ada):
    r, d = c_all.shape
    n = w_ada.shape[1]
    tn = _pick(n, (1024, 512, 256, 128))
    return pl.pallas_call(
        _ada_kernel,
        out_shape=jax.ShapeDtypeStruct((r, n), F32),
        grid=(n // tn,),
        in_specs=[pl.BlockSpec((r, d), lambda j: (0, 0)),
                  pl.BlockSpec((d, tn), lambda j: (0, j)),
                  pl.BlockSpec((1, tn), lambda j: (0, j))],
        out_specs=pl.BlockSpec((r, tn), lambda j: (0, j)),
        compiler_params=_cp("arbitrary"),
        name="ada",
    )(c_all, w_ada, b_ada.reshape(1, n))


def _norm1_kernel(xp_ref, xs_ref, g_ref, sc_ref, sh_ref, h_ref, xall_ref, *, npb):
    i = pl.program_id(0)
    x = jnp.where(i < npb, xp_ref[...], xs_ref[...])
    xall_ref[...] = x
    xn = x * lax.rsqrt(jnp.mean(x * x, axis=-1, keepdims=True) + EPS)
    h_ref[...] = (xn * g_ref[...] * (1.0 + sc_ref[...]) + sh_ref[...]).astype(BF16)


def _norm1(xp3, xs3, g, sc, sh):
    ncp, _, d = xp3.shape
    ncs = xs3.shape[0]
    nch = ncp + ncs
    gc = _pick(math.gcd(ncp, ncs), (4, 2, 1))
    npb = ncp // gc
    blk = (gc, CHUNK, d)
    mod = pl.BlockSpec((gc, 1, d), lambda i: (i, 0, 0))
    return pl.pallas_call(
        functools.partial(_norm1_kernel, npb=npb),
        out_shape=(jax.ShapeDtypeStruct((nch, CHUNK, d), BF16),
                   jax.ShapeDtypeStruct((nch, CHUNK, d), F32)),
        grid=(nch // gc,),
        in_specs=[pl.BlockSpec(blk, lambda i: (jnp.minimum(i, npb - 1), 0, 0)),
                  pl.BlockSpec(blk, lambda i: (jnp.maximum(i - npb, 0), 0, 0)),
                  pl.BlockSpec((1, 1, d), lambda i: (0, 0, 0)),
                  mod, mod],
        out_specs=(pl.BlockSpec(blk, lambda i: (i, 0, 0)),
                   pl.BlockSpec(blk, lambda i: (i, 0, 0))),
        compiler_params=_cp("arbitrary"),
        name="norm1",
    )(xp3, xs3, g.reshape(1, 1, d), sc, sh)


def _mm_kernel(x_ref, w_ref, o_ref, *, act):
    acc = _dot(x_ref[...], w_ref[...])
    if act == "sigmoid":
        acc = jax.nn.sigmoid(acc)
    o_ref[...] = acc.astype(o_ref.dtype)


def _mm(x, w, out_dtype, name, act=None):
    m, k = x.shape
    n = w.shape[1]
    tm = _pick(m, (1024, 512, 256))
    tn = _pick(n, (1024, 512, 256, 128))
    return pl.pallas_call(
        functools.partial(_mm_kernel, act=act),
        out_shape=jax.ShapeDtypeStruct((m, n), out_dtype),
        grid=(m // tm, n // tn),
        in_specs=[pl.BlockSpec((tm, k), lambda i, j: (i, 0)),
                  pl.BlockSpec((k, tn), lambda i, j: (0, j))],
        out_specs=pl.BlockSpec((tm, tn), lambda i, j: (i, j)),
        compiler_params=_cp("arbitrary", "arbitrary"),
        name=name,
    )(x, w)


def _lat_kernel(h_ref, w_ref, wdt_ref, gcq_ref, gckv_ref, cos_ref, sin_ref, dtb_ref, dtbc_ref,
                cqn_ref, lat_ref, latk_ref, kr_ref, dt_ref, dtt_ref, *, ql, kvl):
    h = h_ref[...]
    acc = _dot(h, w_ref[...])
    cq = acc[:, :ql]
    cqn = cq * lax.rsqrt(jnp.mean(cq * cq, axis=-1, keepdims=True) + EPS) * gcq_ref[...]
    cqn_ref[...] = cqn.astype(BF16)
    ckv = acc[:, ql:ql + kvl]
    lat = ckv * lax.rsqrt(jnp.mean(ckv * ckv, axis=-1, keepdims=True) + EPS) * gckv_ref[...]
    lat_ref[...] = lat
    o = ql + kvl
    kr128 = acc[:, o:o + LANES] * cos_ref[...] + acc[:, o + LANES:o + 2 * LANES] * sin_ref[...]
    kr_ref[...] = kr128[:, :ROPE_DIM]
    latk_ref[:, :kvl] = lat.astype(BF16)
    latk_ref[:, kvl:] = kr128.astype(BF16)
    nh = dt_ref.shape[-1]
    dt_ref[...] = _softplus(acc[:, o + 2 * LANES:o + 2 * LANES + nh] + dtb_ref[...])
    dtt = lax.dot_general(wdt_ref[...], h, _NT, preferred_element_type=F32)
    dtt_ref[...] = _softplus(dtt + dtbc_ref[...])


def _lat(h_all, w_lat, w_dt_t, g_cq, g_ckv, cos128, sin128, dt_bias):
    t, d = h_all.shape
    ql, kvl = g_cq.shape[-1], g_ckv.shape[-1]
    nh = dt_bias.shape[-1]
    nl = w_lat.shape[1]
    tm = _pick(t, (512, 256, 128))
    row = lambda w: pl.BlockSpec((tm, w), lambda i: (i, 0))
    const = lambda a, b: pl.BlockSpec((a, b), lambda i: (0, 0))
    return pl.pallas_call(
        functools.partial(_lat_kernel, ql=ql, kvl=kvl),
        out_shape=(jax.ShapeDtypeStruct((t, ql), BF16),
                   jax.ShapeDtypeStruct((t, kvl), F32),
                   jax.ShapeDtypeStruct((t, kvl + LANES), BF16),
                   jax.ShapeDtypeStruct((t, ROPE_DIM), F32),
                   jax.ShapeDtypeStruct((t, nh), F32),
                   jax.ShapeDtypeStruct((nh, t), F32)),
        grid=(t // tm,),
        in_specs=[row(d), const(d, nl), const(nh, d), const(1, ql), const(1, kvl),
                  row(LANES), row(LANES), const(1, nh), const(nh, 1)],
        out_specs=(row(ql), row(kvl), row(kvl + LANES), row(ROPE_DIM), row(nh),
                   pl.BlockSpec((nh, tm), lambda i: (0, i))),
        compiler_params=_cp("arbitrary"),
        name="latent_proj",
    )(h_all, w_lat, w_dt_t, g_cq.reshape(1, ql), g_ckv.reshape(1, kvl), cos128, sin128,
      dt_bias.reshape(1, nh), dt_bias.reshape(nh, 1))


def _q_kernel(c_ref, wa_ref, wb_ref, cos_ref, sin_ref, gn_ref, gr_ref, o_ref, *, heads):
    c = c_ref[...]
    a = _dot(c, wa_ref[...])
    b = _dot(c, wb_ref[...])
    cos, sin = cos_ref[...], sin_ref[...]
    for h in range(heads):
        nope = a[:, h * HEAD_PAD:h * HEAD_PAD + LANES]
        rope = a[:, h * HEAD_PAD + LANES:(h + 1) * HEAD_PAD] * cos + b[:, h * LANES:(h + 1) * LANES] * sin
        ss = (jnp.sum(nope * nope, axis=-1, keepdims=True)
              + jnp.sum(rope * rope, axis=-1, keepdims=True)) * (1.0 / QK_DIM)
        r = lax.rsqrt(ss + EPS)
        o_ref[:, h * HEAD_PAD:h * HEAD_PAD + LANES] = (nope * r * gn_ref[...]).astype(BF16)
        o_ref[:, h * HEAD_PAD + LANES:(h + 1) * HEAD_PAD] = (rope * r * gr_ref[...]).astype(BF16)


def _qproj(cqn, wq_a, wq_b, cos128, sin128, g_nope, g_rope, heads):
    t, ql = cqn.shape
    tm = _pick(t, (256, 128))
    row = lambda w: pl.BlockSpec((tm, w), lambda i: (i, 0))
    const = lambda a, b: pl.BlockSpec((a, b), lambda i: (0, 0))
    return pl.pallas_call(
        functools.partial(_q_kernel, heads=heads),
        out_shape=jax.ShapeDtypeStruct((t, heads * HEAD_PAD), BF16),
        grid=(t // tm,),
        in_specs=[row(ql), const(ql, heads * HEAD_PAD), const(ql, heads * LANES),
                  row(LANES), row(LANES), const(1, LANES), const(1, LANES)],
        out_specs=row(heads * HEAD_PAD),
        compiler_params=_cp("arbitrary"),
        name="q_proj",
    )(cqn, wq_a, wq_b, cos128, sin128, g_nope, g_rope)


def _kv_kernel(lat_ref, kr_ref, w_ref, k_ref, v_ref, *, heads):
    acc = _dot(lat_ref[...], w_ref[...])
    kr = kr_ref[...].astype(F32)
    kr2 = jnp.sum(kr * kr, axis=-1, keepdims=True)
    for h in range(heads):
        kn = acc[:, h * LANES:(h + 1) * LANES]
        ss = (jnp.sum(kn * kn, axis=-1, keepdims=True) + kr2) * (1.0 / QK_DIM)
        r = lax.rsqrt(ss + EPS)
        k_ref[:, h * HEAD_PAD:h * HEAD_PAD + LANES] = (kn * r).astype(BF16)
        k_ref[:, h * HEAD_PAD + LANES:(h + 1) * HEAD_PAD] = (kr * r).astype(BF16)
    v_ref[...] = acc[:, heads * LANES:].astype(BF16)


def _kvproj(latk, w_kv, tp, heads, kvl):
    tm = _pick(tp, (256, 128))
    return pl.pallas_call(
        functools.partial(_kv_kernel, heads=heads),
        out_shape=(jax.ShapeDtypeStruct((tp, heads * HEAD_PAD), BF16),
                   jax.ShapeDtypeStruct((tp, heads * V_DIM), BF16)),
        grid=(tp // tm,),
        in_specs=[pl.BlockSpec((tm, kvl), lambda i: (i, 0)),
                  pl.BlockSpec((tm, LANES), lambda i: (i, kvl // LANES)),
                  pl.BlockSpec((kvl, 2 * heads * LANES), lambda i: (0, 0))],
        out_specs=(pl.BlockSpec((tm, heads * HEAD_PAD), lambda i: (i, 0)),
                   pl.BlockSpec((tm, heads * V_DIM), lambda i: (i, 0))),
        compiler_params=_cp("arbitrary"),
        name="kv_proj",
    )(latk, latk, w_kv)


def _attn_p_kernel(q_ref, k_ref, v_ref, o_ref, m_ref, l_ref, acc_ref, *, tq, hg):
    qi = pl.program_id(2)
    m_ref[...] = jnp.full(m_ref.shape, -jnp.inf, F32)
    l_ref[...] = jnp.zeros(l_ref.shape, F32)
    acc_ref[...] = jnp.zeros(acc_ref.shape, F32)

    def tile(j, masked):
        ks = pl.multiple_of(j * tq, tq)
        for g in range(hg):
            q = q_ref[:, g * HEAD_PAD:(g + 1) * HEAD_PAD]
            k = k_ref[pl.ds(ks, tq), g * HEAD_PAD:(g + 1) * HEAD_PAD]
            v = v_ref[pl.ds(ks, tq), g * V_DIM:(g + 1) * V_DIM]
            s = lax.dot_general(q, k, _NT, preferred_element_type=F32)
            if masked:
                rc = lax.broadcasted_iota(I32, (tq, tq), 0) // CHUNK
                cc = lax.broadcasted_iota(I32, (tq, tq), 1) // CHUNK
                s = jnp.where(cc <= rc, s, -jnp.inf)
            m_prev = m_ref[g]
            m_new = jnp.maximum(m_prev, jnp.max(s, axis=-1, keepdims=True))
            alpha = jnp.exp2(m_prev - m_new)
            p = jnp.exp2(s - jnp.tile(m_new, (1, tq // LANES)))
            l_ref[g] = alpha * l_ref[g] + jnp.sum(p, axis=-1, keepdims=True)
            acc_ref[g] = alpha * acc_ref[g] + _dot(p.astype(BF16), v)
            m_ref[g] = m_new

    def body(j, carry):
        tile(j, False)
        return carry

    lax.fori_loop(0, qi, body, 0)
    tile(qi, True)
    for g in range(hg):
        o_ref[:, g * V_DIM:(g + 1) * V_DIM] = (acc_ref[g] / l_ref[g]).astype(o_ref.dtype)


def _attn_prompt(q_all, k_p, v_p, batch, seq, heads):
    tq = _pick(seq, (512, 256, 128, 64))
    nq = seq // tq
    hg = _pick(heads, (2, 1))
    return pl.pallas_call(
        functools.partial(_attn_p_kernel, tq=tq, hg=hg),
        out_shape=jax.ShapeDtypeStruct((batch * seq, heads * V_DIM), BF16),
        grid=(batch, heads // hg, nq),
        in_specs=[pl.BlockSpec((tq, hg * HEAD_PAD), lambda b, h, qi: (b * nq + qi, h)),
                  pl.BlockSpec((seq, hg * HEAD_PAD), lambda b, h, qi: (b, h)),
                  pl.BlockSpec((seq, hg * V_DIM), lambda b, h, qi: (b, h))],
        out_specs=pl.BlockSpec((tq, hg * V_DIM), lambda b, h, qi: (b * nq + qi, h)),
        scratch_shapes=[pltpu.VMEM((hg, tq, LANES), F32), pltpu.VMEM((hg, tq, LANES), F32),
                        pltpu.VMEM((hg, tq, V_DIM), F32)],
        compiler_params=_cp("arbitrary", "arbitrary", "arbitrary"),
        name="attn_prompt",
    )(q_all, k_p, v_p)


def _attn_s_kernel(pl_ref, pk_ref, nl_ref, q_ref, wukt_ref, wuv_ref, o_ref,
                   qabs_ref, qr_ref, m_ref, l_ref, acc_ref, *, heads, kvl, nkb):
    kb = pl.program_id(1)

    @pl.when(kb == 0)
    def _():
        for h in range(heads):
            qn = q_ref[:, h * HEAD_PAD:h * HEAD_PAD + LANES]
            qabs_ref[h * CHUNK:(h + 1) * CHUNK, :] = _dot(
                qn, wukt_ref[h * LANES:(h + 1) * LANES, :]).astype(BF16)
            qr_ref[h * CHUNK:(h + 1) * CHUNK, :] = q_ref[:, h * HEAD_PAD + LANES:(h + 1) * HEAD_PAD]
        m_ref[...] = jnp.full(m_ref.shape, -jnp.inf, F32)
        l_ref[...] = jnp.zeros(l_ref.shape, F32)
        acc_ref[...] = jnp.zeros(acc_ref.shape, F32)

    def block(xl, krf):
        rtop = lax.dot_general(wukt_ref[...], xl, _NT, preferred_element_type=F32)
        sq = krf * krf
        sq_hi = sq.astype(BF16)
        sq_lo = (sq - sq_hi.astype(F32)).astype(BF16)
        ones = jnp.ones((SUBLANES, ROPE_DIM), BF16)
        kr2 = (lax.dot_general(ones, sq_hi, _NT, preferred_element_type=F32)
               + lax.dot_general(ones, sq_lo, _NT, preferred_element_type=F32))[0:1, :]
        s = (lax.dot_general(qabs_ref[...], xl, _NT, preferred_element_type=F32)
             + lax.dot_general(qr_ref[:, :ROPE_DIM], krf.astype(BF16), _NT,
                               preferred_element_type=F32))
        parts = []
        for h in range(heads):
            rt = rtop[h * LANES:(h + 1) * LANES, :]
            kn2 = jnp.sum(rt * rt, axis=0, keepdims=True)
            r = lax.rsqrt((kn2 + kr2) * (1.0 / QK_DIM) + EPS)
            parts.append(s[h * CHUNK:(h + 1) * CHUNK, :] * r)
        s = jnp.concatenate(parts, axis=0)
        n = s.shape[1]
        m_prev = m_ref[...]
        m_new = jnp.maximum(m_prev, jnp.max(s, axis=-1, keepdims=True))
        alpha = jnp.exp2(m_prev - m_new)
        m_wide = jnp.tile(m_new, (1, n // LANES)) if n >= LANES else m_new[:, :n]
        p = jnp.exp2(s - m_wide)
        l_ref[...] = alpha * l_ref[...] + jnp.sum(p, axis=-1, keepdims=True)
        acc_ref[...] = jnp.tile(alpha, (1, kvl // LANES)) * acc_ref[...] + _dot(p.astype(BF16), xl)
        m_ref[...] = m_new

    @pl.when(kb < nkb)
    def _():
        block(pl_ref[0].astype(BF16), pk_ref[0])

    @pl.when(kb == nkb)
    def _():
        block(nl_ref[:, :kvl], nl_ref[:, kvl:kvl + ROPE_DIM].astype(F32))
        o = (acc_ref[...] / jnp.tile(l_ref[...], (1, kvl // LANES))).astype(BF16)
        for h in range(heads):
            o_ref[:, h * V_DIM:(h + 1) * V_DIM] = _dot(
                o[h * CHUNK:(h + 1) * CHUNK, :], wuv_ref[:, h * V_DIM:(h + 1) * V_DIM]).astype(o_ref.dtype)


def _attn_sample(past_lat, past_kr, latk, q_all, w_uk_t, w_uv2, tp, heads):
    bs, past, kvl = past_lat.shape
    tk = _pick(past, (512, 256, 128))
    nkb = past // tk
    c0 = tp // CHUNK
    pidx = lambda b, kb: (b, jnp.minimum(kb, nkb - 1), 0)
    hq = heads * CHUNK
    return pl.pallas_call(
        functools.partial(_attn_s_kernel, heads=heads, kvl=kvl, nkb=nkb),
        out_shape=jax.ShapeDtypeStruct((bs * CHUNK, heads * V_DIM), BF16),
        grid=(bs, nkb + 1),
        in_specs=[pl.BlockSpec((1, tk, kvl), pidx),
                  pl.BlockSpec((1, tk, ROPE_DIM), pidx),
                  pl.BlockSpec((CHUNK, kvl + LANES), lambda b, kb: (c0 + b, 0)),
                  pl.BlockSpec((CHUNK, heads * HEAD_PAD), lambda b, kb: (c0 + b, 0)),
                  pl.BlockSpec((heads * LANES, kvl), lambda b, kb: (0, 0)),
                  pl.BlockSpec((kvl, heads * V_DIM), lambda b, kb: (0, 0))],
        out_specs=pl.BlockSpec((CHUNK, heads * V_DIM), lambda b, kb: (b, 0)),
        scratch_shapes=[pltpu.VMEM((hq, kvl), BF16), pltpu.VMEM((hq, LANES), BF16),
                        pltpu.VMEM((hq, LANES), F32), pltpu.VMEM((hq, LANES), F32),
                        pltpu.VMEM((hq, kvl), F32)],
        compiler_params=_cp("arbitrary", "arbitrary"),
        name="attn_sample",
    )(past_lat, past_kr, latk, q_all, w_uk_t, w_uv2)


def _ssd_kernel(z_ref, x_ref, dt_ref, dtt_ref, conv0_ref, ssm0_ref, cw_ref, cb_ref, alr_ref, alc_ref,
                dsk_ref, gs_ref, eh_ref, o_ref, st_out_ref, xs_ref, st_ref,
                *, lc, lr, nh, groups):
    c = pl.program_id(1)
    p = M_HEADDIM
    n = D_STATE
    ci = nh * p
    k8 = nh // groups
    gw = k8 * p

    @pl.when(c == 0)
    def _():
        xs_ref[0:SUBLANES, :] = conv0_ref[0]
        st_ref[...] = ssm0_ref[0]

    xs_ref[SUBLANES:SUBLANES + lr, :] = x_ref[...].astype(F32)
    if lc > lr:
        xs_ref[SUBLANES + lr:SUBLANES + lc, :] = jnp.zeros((lc - lr, xs_ref.shape[1]), F32)

    def conv(lo, hi):
        u = xs_ref[SUBLANES - 3:SUBLANES - 3 + lc, lo:hi] * cw_ref[0:1, lo:hi]
        for tap in range(1, CONV_W):
            u = u + xs_ref[SUBLANES - 3 + tap:SUBLANES - 3 + tap + lc, lo:hi] * cw_ref[tap:tap + 1, lo:hi]
        return _silu(u + cb_ref[:, lo:hi])

    dt = dt_ref[0]
    dtt = dtt_ref[0]
    a_row = -jnp.exp(alr_ref[...])
    a_col = -jnp.exp(alc_ref[...])
    ri = lax.broadcasted_iota(I32, (lc, lc), 0)
    cidx = lax.broadcasted_iota(I32, (lc, lc), 1)
    tri = ri >= cidx
    tril = jnp.where(tri, 1.0, 0.0).astype(BF16)
    triu = jnp.where(ri <= cidx, 1.0, 0.0).astype(BF16)
    cs = sum(_dot(tril, piece) for piece in _split3(dt * a_row))
    cst = sum(_dot(piece, triu) for piece in _split3(dtt * a_col))
    exp_cs = jnp.exp(cs)
    w_end = jnp.exp(cs[lc - 1:lc, :] - cs)
    stacked = jnp.concatenate([dt, exp_cs, w_end], axis=0)
    eh = eh_ref[...]
    expanded = sum(_dot(piece, eh) for piece in _split3(stacked))
    dt_e, ecs_e, wend_e = expanded[:lc], expanded[lc:2 * lc], expanded[2 * lc:]
    cdec = jnp.exp(cst[:, lc - 1:lc])
    lane_lo = lax.broadcasted_iota(I32, (lc, LANES), 1) < p

    for g in range(groups):
        gs = slice(g * gw, (g + 1) * gw)
        xg = conv(g * gw, (g + 1) * gw)
        bg = conv(ci + g * n, ci + (g + 1) * n).astype(BF16)
        cg = conv(ci + groups * n + g * n, ci + groups * n + (g + 1) * n).astype(BF16)
        cbm = lax.dot_general(cg, bg, _NT, preferred_element_type=F32)
        xdt = xg * dt_e[:, gs]
        xdt_b = xdt.astype(BF16)
        pairs = []
        for q in range(k8 // 2):
            x2 = xdt_b[:, q * LANES:(q + 1) * LANES]
            ys = []
            for s in range(2):
                h = g * k8 + 2 * q + s
                seg = cs[:, h:h + 1] - cst[h:h + 1, :]
                dec = jnp.exp(jnp.where(tri, seg, -jnp.inf))
                ys.append(_dot((cbm * dec).astype(BF16), x2))
            pairs.append(jnp.where(lane_lo, ys[0], ys[1]))
        y_diag = jnp.concatenate(pairs, axis=1)
        sg = st_ref[g * gw:(g + 1) * gw, :]
        y_off = lax.dot_general(cg, sg.astype(BF16), _NT, preferred_element_type=F32) * ecs_e[:, gs]
        y = y_diag + y_off + xg * dsk_ref[:, gs]
        xw = (xdt * wend_e[:, gs]).astype(BF16)
        upd = lax.dot_general(xw, bg, _TN, preferred_element_type=F32)
        for k in range(k8):
            h = g * k8 + k
            rows = slice(g * gw + k * p, g * gw + (k + 1) * p)
            st_ref[rows, :] = st_ref[rows, :] * cdec[h:h + 1, :] + upd[k * p:(k + 1) * p, :]
        zg = z_ref[:, gs].astype(F32)
        u2 = y[:lr] * _silu(zg)
        ms = jnp.mean(u2 * u2, axis=-1, keepdims=True)
        o_ref[:, gs] = (u2 * lax.rsqrt(ms + EPS) * gs_ref[:, gs]).astype(o_ref.dtype)

    xs_ref[0:SUBLANES, :] = xs_ref[lr:lr + SUBLANES, :]

    @pl.when(c == pl.num_programs(1) - 1)
    def _():
        st_out_ref[0] = st_ref[...]


def _ssd(z_all, xbc_all, dt3, dtt3, conv0p, ssm0, conv_w, conv_b, a_log, dsk, g_ssm, eh,
         row0, nseq, nchunk, lc, lr, groups):
    ci = z_all.shape[1]
    cc = xbc_all.shape[1]
    nh = ci // M_HEADDIM
    rb0 = row0 // lr
    rowblk = lambda w: pl.BlockSpec((lr, w), lambda b, c: (rb0 + b * nchunk + c, 0))
    seq3 = lambda a, b_: pl.BlockSpec((1, a, b_), lambda b, c: (b * nchunk + c, 0, 0))
    perb = lambda a, b_: pl.BlockSpec((1, a, b_), lambda b, c: (b, 0, 0))
    const = lambda a, b_: pl.BlockSpec((a, b_), lambda b, c: (0, 0))
    return pl.pallas_call(
        functools.partial(_ssd_kernel, lc=lc, lr=lr, nh=nh, groups=groups),
        out_shape=(jax.ShapeDtypeStruct((nseq * nchunk * lr, ci), BF16),
                   jax.ShapeDtypeStruct((nseq, ci, D_STATE), F32)),
        grid=(nseq, nchunk),
        in_specs=[rowblk(ci), rowblk(cc), seq3(lc, nh), seq3(nh, lc),
                  perb(SUBLANES, cc), perb(ci, D_STATE),
                  const(CONV_W, cc), const(1, cc), const(1, nh), const(nh, 1),
                  const(1, ci), const(1, ci), const(nh, ci)],
        out_specs=(pl.BlockSpec((lr, ci), lambda b, c: (b * nchunk + c, 0)),
                   perb(ci, D_STATE)),
        scratch_shapes=[pltpu.VMEM((lc + SUBLANES, cc), F32), pltpu.VMEM((ci, D_STATE), F32)],
        compiler_params=_cp("arbitrary", "arbitrary"),
        name="ssd",
    )(z_all, xbc_all, dt3, dtt3, conv0p, ssm0, conv_w, conv_b.reshape(1, cc),
      a_log.reshape(1, nh), a_log.reshape(nh, 1), dsk, g_ssm.reshape(1, ci), eh)


def _mix_kernel(ap_ref, as_ref, mp_ref, ms_ref, ga_ref, gb_ref, wpa_ref, wpb_ref, o_ref, *, npb):
    i = pl.program_id(0)
    a = jnp.where(i < npb, ap_ref[...], as_ref[...])
    m = jnp.where(i < npb, mp_ref[...], ms_ref[...])
    pa = _dot(a, wpa_ref[...])
    pb = _dot(m, wpb_ref[...])
    o_ref[...] = (ga_ref[...].astype(F32) * pa + gb_ref[...].astype(F32) * pb).astype(o_ref.dtype)


def _mix(a_p, a_s, m_p, m_s, gates, w_pa, w_pb):
    tp, hv = a_p.shape
    ts = a_s.shape[0]
    ci = m_p.shape[1]
    d = w_pa.shape[1]
    tm = _pick(math.gcd(tp, ts), (512, 256, 128, 64))
    tn = _pick(d, (512, 256, 128))
    npb = tp // tm
    nj = d // tn
    pidx = lambda i, j: (jnp.minimum(i, npb - 1), 0)
    sidx = lambda i, j: (jnp.maximum(i - npb, 0), 0)
    return pl.pallas_call(
        functools.partial(_mix_kernel, npb=npb),
        out_shape=jax.ShapeDtypeStruct((tp + ts, d), BF16),
        grid=((tp + ts) // tm, nj),
        in_specs=[pl.BlockSpec((tm, hv), pidx), pl.BlockSpec((tm, hv), sidx),
                  pl.BlockSpec((tm, ci), pidx), pl.BlockSpec((tm, ci), sidx),
                  pl.BlockSpec((tm, tn), lambda i, j: (i, j)),
                  pl.BlockSpec((tm, tn), lambda i, j: (i, nj + j)),
                  pl.BlockSpec((hv, tn), lambda i, j: (0, j)),
                  pl.BlockSpec((ci, tn), lambda i, j: (0, j))],
        out_specs=pl.BlockSpec((tm, tn), lambda i, j: (i, j)),
        compiler_params=_cp("arbitrary", "arbitrary"),
        name="branch_mix",
    )(a_p, a_s, m_p, m_s, gates, gates, w_pa, w_pb)


def _post_kernel(mx_ref, wout_ref, x_ref, gt1_ref, sc2_ref, sh2_ref, g2_ref, wrh_ref, wrl_ref, br_ref,
                 x1_ref, h2_ref, ti_ref, tp_ref, *, nexp):
    o = _dot(mx_ref[...], wout_ref[...])
    tm = o.shape[0]
    for ch in range(tm // CHUNK):
        rows = slice(ch * CHUNK, (ch + 1) * CHUNK)
        x1 = x_ref[rows, :] + gt1_ref[ch] * o[rows, :]
        x1_ref[rows, :] = x1
        xn = x1 * lax.rsqrt(jnp.mean(x1 * x1, axis=-1, keepdims=True) + EPS)
        h2_ref[rows, :] = xn * g2_ref[...] * (1.0 + sc2_ref[ch]) + sh2_ref[ch]
    h2 = h2_ref[...]
    hh = h2.astype(BF16)
    hl = (h2 - hh.astype(F32)).astype(BF16)
    logits = _dot(hh, wrh_ref[...]) + _dot(hh, wrl_ref[...]) + _dot(hl, wrh_ref[...]) + br_ref[...]
    lane = lax.broadcasted_iota(I32, logits.shape, 1)
    logits = jnp.where(lane < nexp, logits, NEG_BIG)
    vals, idxs = [], []
    for _ in range(TOP_K):
        m = jnp.max(logits, axis=-1, keepdims=True)
        idx = jnp.min(jnp.where(logits == m, lane, LANES), axis=-1, keepdims=True)
        vals.append(m)
        idxs.append(idx)
        logits = jnp.where(lane == idx, 2.0 * NEG_BIG, logits)
    es = [jnp.exp(v - vals[0]) for v in vals]
    den = es[0]
    for e in es[1:]:
        den = den + e
    ti = jnp.zeros(lane.shape, I32)
    tpv = jnp.zeros(lane.shape, F32)
    for k in range(TOP_K):
        ti = jnp.where(lane == k, idxs[k], ti)
        tpv = jnp.where(lane == k, es[k] / den, tpv)
    ti_ref[...] = ti
    tp_ref[...] = tpv


def _post(mixed, w_out, x_all, gt1, sc2, sh2, g2, wr_hi, wr_lo, br, nexp):
    t, d = mixed.shape
    tm = _pick(t, (256, 128, 64))
    nc = tm // CHUNK
    row = lambda w: pl.BlockSpec((tm, w), lambda i: (i, 0))
    mod = pl.BlockSpec((nc, 1, d), lambda i: (i, 0, 0))
    const = lambda a, b: pl.BlockSpec((a, b), lambda i: (0, 0))
    return pl.pallas_call(
        functools.partial(_post_kernel, nexp=nexp),
        out_shape=(jax.ShapeDtypeStruct((t, d), F32), jax.ShapeDtypeStruct((t, d), F32),
                   jax.ShapeDtypeStruct((t, LANES), I32), jax.ShapeDtypeStruct((t, LANES), F32)),
        grid=(t // tm,),
        in_specs=[row(d), const(d, d), row(d), mod, mod, mod, const(1, d),
                  const(d, LANES), const(d, LANES), const(1, LANES)],
        out_specs=(row(d), row(d), row(LANES), row(LANES)),
        compiler_params=_cp("arbitrary"),
        name="post_mix",
    )(mixed, w_out, x_all, gt1, sc2, sh2, g2.reshape(1, d), wr_hi, wr_lo, br)


def _slots_kernel(ti_ref, dest_ref, meta_ref, run_ref, *, bm, tb):
    ph = pl.program_id(0)
    i = pl.program_id(1)
    eid = lax.broadcasted_iota(I32, (LANES, tb), 0)
    onehots = [jnp.where(eid == ti_ref[k:k + 1, :], 1.0, 0.0) for k in range(TOP_K)]
    osum = onehots[0]
    for oh in onehots[1:]:
        osum = osum + oh
    blk_cnt = jnp.sum(osum, axis=1, keepdims=True)

    @pl.when(jnp.logical_and(ph == 0, i == 0))
    def _():
        run_ref[...] = jnp.zeros(run_ref.shape, F32)

    @pl.when(ph == 0)
    def _():
        run_ref[...] = run_ref[...] + blk_cnt

    @pl.when(jnp.logical_and(ph == 1, i == 0))
    def _():
        cnt = run_ref[...]
        nblk = jnp.ceil(cnt * (1.0 / bm))
        r = lax.broadcasted_iota(I32, (LANES, LANES), 0)
        c = lax.broadcasted_iota(I32, (LANES, LANES), 1)
        lstrict = jnp.where(c < r, 1.0, 0.0).astype(BF16)
        start_blk = _dot(lstrict, jnp.broadcast_to(nblk, (LANES, LANES)).astype(BF16))
        lane = lax.broadcasted_iota(I32, (LANES, LANES), 1)
        meta_ref[...] = jnp.where(lane == 0, cnt, jnp.where(lane == 1, start_blk, 0.0))
        run_ref[...] = start_blk[:, 0:1] * float(bm)

    @pl.when(ph == 1)
    def _():
        r = lax.broadcasted_iota(I32, (tb, tb), 0)
        c = lax.broadcasted_iota(I32, (tb, tb), 1)
        ustrict = jnp.where(r < c, 1.0, 0.0).astype(BF16)
        base = run_ref[...] + _dot(osum.astype(BF16), ustrict)
        for k in range(TOP_K):
            dest_ref[0, k:k + 1, :] = jnp.sum(onehots[k] * base, axis=0, keepdims=True).astype(I32)
        dest_ref[0, TOP_K:, :] = jnp.zeros((SUBLANES - TOP_K, tb), I32)
        run_ref[...] = run_ref[...] + blk_cnt


def _moe_slots(ti_t, bm, tb):
    t = ti_t.shape[1]
    nt = t // tb
    return pl.pallas_call(
        functools.partial(_slots_kernel, bm=bm, tb=tb),
        out_shape=(jax.ShapeDtypeStruct((nt, SUBLANES, tb), I32),
                   jax.ShapeDtypeStruct((LANES, LANES), F32)),
        grid=(2, nt),
        in_specs=[pl.BlockSpec((SUBLANES, tb), lambda ph, i: (0, i))],
        out_specs=(pl.BlockSpec((1, SUBLANES, tb), lambda ph, i: (i * ph, 0, 0)),
                   pl.BlockSpec((LANES, LANES), lambda ph, i: (0, 0))),
        scratch_shapes=[pltpu.VMEM((LANES, 1), F32)],
        compiler_params=_cp("arbitrary", "arbitrary"),
        name="moe_slots",
    )(ti_t)


def _dispatch_kernel(zs_ref, idx_ref, h_ref, xg_ref, zbuf_ref, idx_smem, isem, sem, *, nz, bm, tb):
    i = pl.program_id(0)

    def zero_fill(e):
        return pltpu.make_async_copy(zbuf_ref, xg_ref.at[pl.ds(pl.multiple_of(zs_ref[e], bm), bm)], sem)

    @pl.when(i == 0)
    def _():
        zbuf_ref[...] = jnp.zeros(zbuf_ref.shape, zbuf_ref.dtype)
        for e in range(nz):
            pl.when(zs_ref[e] >= 0)(lambda e=e: zero_fill(e).start())
        for e in range(nz):
            pl.when(zs_ref[e] >= 0)(lambda e=e: zero_fill(e).wait())

    icp = pltpu.make_async_copy(idx_ref.at[0], idx_smem, isem)
    icp.start()
    icp.wait()

    def start(r, carry):
        for k in range(TOP_K):
            pltpu.make_async_copy(h_ref.at[pl.ds(r, 1)], xg_ref.at[pl.ds(idx_smem[k, r], 1)], sem).start()
        return carry

    lax.fori_loop(0, tb, start, 0)
    for k in range(TOP_K):
        pltpu.make_async_copy(h_ref, xg_ref.at[pl.ds(0, tb)], sem).wait()


def _moe_dispatch(zstart, dest3, h2, n_slots, bm):
    nt, _, tb = dest3.shape
    d = h2.shape[1]
    return pl.pallas_call(
        functools.partial(_dispatch_kernel, nz=zstart.shape[0], bm=bm, tb=tb),
        out_shape=jax.ShapeDtypeStruct((n_slots, d), F32),
        grid_spec=pltpu.PrefetchScalarGridSpec(
            num_scalar_prefetch=1,
            grid=(nt,),
            in_specs=[pl.BlockSpec((1, SUBLANES, tb), lambda i, zs: (i, 0, 0)),
                      pl.BlockSpec((tb, d), lambda i, zs: (i, 0))],
            out_specs=pl.BlockSpec(memory_space=pl.ANY),
            scratch_shapes=[pltpu.VMEM((bm, d), F32), pltpu.SMEM((SUBLANES, tb), I32),
                            pltpu.SemaphoreType.DMA, pltpu.SemaphoreType.DMA]),
        compiler_params=_cp("arbitrary"),
        name="moe_dispatch",
    )(zstart, dest3, h2)


def _expert_changed(be_ref, b):
    return jnp.logical_or(b == 0, be_ref[b] != be_ref[jnp.maximum(b - 1, 0)])


def _gateup_kernel(be_ref, nu_ref, x_ref, wg_ref, wu_ref, bg_ref, bu_ref, o_ref, wgb_ref, wub_ref):
    b = pl.program_id(1)
    used = b < nu_ref[0]

    @pl.when(jnp.logical_and(used, _expert_changed(be_ref, b)))
    def _():
        wgb_ref[...] = wg_ref[0].astype(BF16)
        wub_ref[...] = wu_ref[0].astype(BF16)

    @pl.when(used)
    def _():
        x = x_ref[...].astype(BF16)
        gate = jnp.minimum(_dot(x, wgb_ref[...]) + bg_ref[0], SWIGLU_LIMIT)
        up = jnp.clip(_dot(x, wub_ref[...]) + bu_ref[0], -SWIGLU_LIMIT, SWIGLU_LIMIT)
        glu = gate * jax.nn.sigmoid(SWIGLU_ALPHA * gate)
        o_ref[...] = ((up + 1.0) * glu).astype(o_ref.dtype)

    @pl.when(jnp.logical_not(used))
    def _():
        o_ref[...] = jnp.zeros(o_ref.shape, o_ref.dtype)


def _moe_gateup(block_e, n_used, xg, w_gate, w_up, b_gate, b_up):
    ns, d = xg.shape
    nexp, _, ff = w_gate.shape
    bm = MOE_BLOCK
    nb = ns // bm
    tf = _pick(ff, (1024, 512, 256, 128))
    wspec = pl.BlockSpec((1, d, tf), lambda c, b, be, nu: (be[b], 0, c))
    bspec = pl.BlockSpec((1, 1, tf), lambda c, b, be, nu: (be[b], 0, c))
    return pl.pallas_call(
        _gateup_kernel,
        out_shape=jax.ShapeDtypeStruct((ns, ff), BF16),
        grid_spec=pltpu.PrefetchScalarGridSpec(
            num_scalar_prefetch=2,
            grid=(ff // tf, nb),
            in_specs=[pl.BlockSpec((bm, d), lambda c, b, be, nu: (jnp.minimum(b, nu[0] - 1), 0)),
                      wspec, wspec, bspec, bspec],
            out_specs=pl.BlockSpec((bm, tf), lambda c, b, be, nu: (b, c)),
            scratch_shapes=[pltpu.VMEM((d, tf), BF16), pltpu.VMEM((d, tf), BF16)]),
        compiler_params=_cp("arbitrary", "arbitrary"),
        name="moe_gateup",
    )(block_e, n_used, xg, w_gate, w_up, b_gate.reshape(nexp, 1, ff), b_up.reshape(nexp, 1, ff))


def _down_kernel(be_ref, nu_ref, g_ref, wd_ref, bd_ref, o_ref, wdb_ref):
    b = pl.program_id(1)
    used = b < nu_ref[0]

    @pl.when(jnp.logical_and(used, _expert_changed(be_ref, b)))
    def _():
        wdb_ref[...] = wd_ref[0].astype(BF16)

    @pl.when(used)
    def _():
        o_ref[...] = _dot(g_ref[...], wdb_ref[...]) + bd_ref[0]

    @pl.when(jnp.logical_not(used))
    def _():
        o_ref[...] = jnp.zeros(o_ref.shape, o_ref.dtype)


def _moe_down(block_e, n_used, glu, w_down, b_down):
    ns, ff = glu.shape
    nexp, _, d = w_down.shape
    bm = MOE_BLOCK
    nb = ns // bm
    tn = _pick(d, (1024, 512, 256, 128))
    return pl.pallas_call(
        _down_kernel,
        out_shape=jax.ShapeDtypeStruct((ns, d), F32),
        grid_spec=pltpu.PrefetchScalarGridSpec(
            num_scalar_prefetch=2,
            grid=(d // tn, nb),
            in_specs=[pl.BlockSpec((bm, ff), lambda c, b, be, nu: (jnp.minimum(b, nu[0] - 1), 0)),
                      pl.BlockSpec((1, ff, tn), lambda c, b, be, nu: (be[b], 0, c)),
                      pl.BlockSpec((1, 1, tn), lambda c, b, be, nu: (be[b], 0, c))],
            out_specs=pl.BlockSpec((bm, tn), lambda c, b, be, nu: (b, c)),
            scratch_shapes=[pltpu.VMEM((ff, tn), BF16)]),
        compiler_params=_cp("arbitrary", "arbitrary"),
        name="moe_down",
    )(block_e, n_used, glu, w_down, b_down.reshape(nexp, 1, d))


def _combine_kernel(idx_ref, yb_ref, x1_ref, gt2_ref, p_ref, o_ref, buf_ref, idx_smem, isem, sem, *, tc):
    icp = pltpu.make_async_copy(idx_ref.at[0], idx_smem, isem)
    icp.start()
    icp.wait()

    def start(r, carry):
        for k in range(TOP_K):
            pltpu.make_async_copy(yb_ref.at[pl.ds(idx_smem[k, r], 1)],
                                  buf_ref.at[pl.ds(k * tc + r, 1)], sem).start()
        return carry

    lax.fori_loop(0, tc, start, 0)
    pltpu.make_async_copy(yb_ref.at[pl.ds(0, TOP_K * tc)], buf_ref, sem).wait()
    ff = p_ref[:, 0:1] * buf_ref[0:tc, :]
    for k in range(1, TOP_K):
        ff = ff + p_ref[:, k:k + 1] * buf_ref[k * tc:(k + 1) * tc, :]
    for ch in range(tc // CHUNK):
        rows = slice(ch * CHUNK, (ch + 1) * CHUNK)
        o_ref[rows, :] = x1_ref[rows, :] + gt2_ref[ch] * ff[rows, :]


def _moe_combine(dest3, yb, x1, gt2, top_p, row0, nrows):
    d = yb.shape[1]
    tc = dest3.shape[2]
    nc = tc // CHUNK
    rb0 = row0 // tc
    return pl.pallas_call(
        functools.partial(_combine_kernel, tc=tc),
        out_shape=jax.ShapeDtypeStruct((nrows, d), F32),
        grid=(nrows // tc,),
        in_specs=[pl.BlockSpec((1, SUBLANES, tc), lambda i: (rb0 + i, 0, 0)),
                  pl.BlockSpec(memory_space=pl.ANY),
                  pl.BlockSpec((tc, d), lambda i: (rb0 + i, 0)),
                  pl.BlockSpec((nc, 1, d), lambda i: (rb0 + i, 0, 0)),
                  pl.BlockSpec((tc, LANES), lambda i: (rb0 + i, 0))],
        out_specs=pl.BlockSpec((tc, d), lambda i: (i, 0)),
        scratch_shapes=[pltpu.VMEM((TOP_K * tc, d), F32), pltpu.SMEM((SUBLANES, tc), I32),
                        pltpu.SemaphoreType.DMA, pltpu.SemaphoreType.DMA],
        compiler_params=_cp("arbitrary"),
        name="moe_combine",
    )(dest3, yb, x1, gt2, top_p)


def _block_layout(meta, nexp, bm, n_blocks):
    counts = meta[:nexp, 0]
    start_blk = meta[:nexp, 1].astype(I32)
    nblk = jnp.ceil(counts * (1.0 / bm)).astype(I32)
    end_blk = start_blk + nblk
    n_used = jnp.sum(nblk)
    blk = jnp.minimum(jnp.arange(n_blocks), n_used - 1)
    block_e = jnp.minimum(jnp.searchsorted(end_blk, blk, side="right"), nexp - 1).astype(I32)
    last = jnp.where(nblk > 0, (end_blk - 1) * bm, -1)
    spare = n_used + jnp.arange(nexp)
    spare = jnp.where(spare < n_blocks, spare * bm, -1)
    zstart = jnp.concatenate([last, spare]).astype(I32)
    return block_e, n_used.reshape(1).astype(I32), zstart


def _rope_tables(pos):
    half = ROPE_DIM // 2
    inv = ROPE_THETA ** (-jnp.arange(half, dtype=F32) / half)
    ang = pos.astype(F32)[:, None] * inv[None, :]
    z = jnp.zeros((pos.shape[0], LANES - ROPE_DIM), F32)
    cos, sin = jnp.cos(ang), jnp.sin(ang)
    return jnp.concatenate([cos, cos, z], axis=1), jnp.concatenate([sin, sin, z], axis=1)


def _rot_half_cols(w):
    half = ROPE_DIM // 2
    return jnp.concatenate([-w[..., half:], w[..., :half]], axis=-1)


def _layer(x_prompt, x_sample, past_lat, past_kr, ssm_s0, conv_s0, c_prompt, c_sample,
           w_ada, b_ada, g_norm1, w_in, g_cq, g_ckv, w_uq, w_uk, w_uv, g_qn, g_kn, conv_w, conv_b, dt_bias,
           a_log, d_skip, g_ssm, w_pa, w_pb, w_out, g_norm2, w_router, b_router, w_gate, b_gate, w_up, b_up,
           w_down, b_down):
    bp, sp, d = x_prompt.shape
    bs, ss, _ = x_sample.shape
    assert ss == CHUNK and sp % CHUNK == 0
    past = past_lat.shape[1]
    ql, kvl = g_cq.shape[-1], g_ckv.shape[-1]
    heads = w_uq.shape[1]
    ci = g_ssm.shape[-1]
    cc = conv_w.shape[-1]
    nh = ci // M_HEADDIM
    groups = (cc - ci) // (2 * D_STATE)
    nexp = w_router.shape[-1]
    tp, ts = bp * sp, bs * ss
    t = tp + ts
    ncp, ncs = tp // CHUNK, ts // CHUNK

    ada = _ada(jnp.concatenate([c_prompt, c_sample], axis=0), w_ada, b_ada)
    per_chunk = jnp.concatenate([jnp.repeat(ada[:bp], sp // CHUNK, axis=0), ada[bp:]], axis=0)
    sh1, sc1, gt1, sh2, sc2, gt2 = [m[:, None, :] for m in jnp.split(per_chunk, 6, axis=-1)]

    h3, x3 = _norm1(x_prompt.reshape(ncp, CHUNK, d), x_sample.reshape(ncs, CHUNK, d), g_norm1, sc1, sh1)
    h_all = h3.reshape(t, d)
    x_all = x3.reshape(t, d)

    o = 0
    w_cq = w_in[:, o:o + ql]; o += ql
    w_ckv = w_in[:, o:o + kvl]; o += kvl
    w_kr = w_in[:, o:o + ROPE_DIM]; o += ROPE_DIM
    w_z = w_in[:, o:o + ci]; o += ci
    w_xbc = w_in[:, o:o + cc]; o += cc
    w_dt = w_in[:, o:o + nh]; o += nh
    w_gab = w_in[:, o:o + 2 * d]
    zc = lambda n_: jnp.zeros((d, n_), F32)
    w_lat = jnp.concatenate([w_cq, w_ckv, w_kr, zc(LANES - ROPE_DIM), _rot_half_cols(w_kr),
                             zc(LANES - ROPE_DIM), w_dt, zc(LANES - nh)], axis=1).astype(BF16)

    pos = jnp.concatenate([jnp.tile(jnp.arange(sp), bp), jnp.tile(past + jnp.arange(ss), bs)])
    cos128, sin128 = _rope_tables(pos)

    cqn, lat_all, latk, kr_all, dt_all, dtt_all = _lat(
        h_all, w_lat, w_dt.T.astype(BF16), g_cq, g_ckv, cos128, sin128, dt_bias)
    z_all = _mm(h_all, w_z.astype(BF16), BF16, "proj_z")
    xbc_all = _mm(h_all, w_xbc.astype(BF16), BF16, "proj_xbc")
    gates = _mm(h_all, w_gab.astype(BF16), BF16, "proj_gates", act="sigmoid")

    tail = CONV_W - 1
    h_tail = jnp.concatenate([h3[:ncp].reshape(bp, sp, d)[:, sp - tail:, :].reshape(bp * tail, d),
                              h3[ncp:][:, CHUNK - tail:, :].reshape(bs * tail, d)], axis=0)
    conv_tail = _mm(h_tail, w_xbc.astype(BF16), F32, "proj_conv_tail")
    conv_p = conv_tail[:bp * tail].reshape(bp, tail, cc)
    conv_s = conv_tail[bp * tail:].reshape(bs, tail, cc)

    wq_a = jnp.concatenate([w_uq, jnp.zeros((ql, heads, HEAD_PAD - QK_DIM), F32)], axis=-1)
    wq_a = wq_a.reshape(ql, heads * HEAD_PAD).astype(BF16)
    wq_b = jnp.concatenate([_rot_half_cols(w_uq[..., NOPE_DIM:]),
                            jnp.zeros((ql, heads, LANES - ROPE_DIM), F32)], axis=-1)
    wq_b = wq_b.reshape(ql, heads * LANES).astype(BF16)
    gq = g_qn * g_kn * (ATTN_SCALE * math.log2(math.e))
    g_nope = gq[:NOPE_DIM].reshape(1, LANES)
    g_rope = jnp.concatenate([gq[NOPE_DIM:], jnp.zeros((LANES - ROPE_DIM,), F32)]).reshape(1, LANES)
    q_all = _qproj(cqn, wq_a, wq_b, cos128, sin128, g_nope, g_rope, heads)

    w_uk2 = w_uk.reshape(kvl, heads * NOPE_DIM)
    w_uv2 = w_uv.reshape(kvl, heads * V_DIM).astype(BF16)
    w_kv = jnp.concatenate([w_uk2.astype(BF16), w_uv2], axis=1)
    k_p, v_p = _kvproj(latk, w_kv, tp, heads, kvl)
    a_p = _attn_prompt(q_all, k_p, v_p, bp, sp, heads)
    a_s = _attn_sample(past_lat, past_kr, latk, q_all, w_uk2.T.astype(BF16), w_uv2, tp, heads)

    eh = jnp.repeat(jnp.eye(nh, dtype=BF16), M_HEADDIM, axis=1)
    dsk = jnp.repeat(d_skip, M_HEADDIM).reshape(1, ci)
    lcp = _pick(sp, (256, 128))
    assert sp % lcp == 0 and lcp % LANES == 0
    ncq = sp // lcp
    dt3_p = dt_all[:tp].reshape(bp * ncq, lcp, nh)
    dtt3_p = dtt_all[:, :tp].reshape(nh, bp * ncq, lcp).transpose(1, 0, 2)
    zpad = lambda a, axis: jnp.concatenate([a, jnp.zeros_like(a)], axis=axis)
    lcs = 2 * CHUNK
    dt3_s = zpad(dt_all[tp:].reshape(bs, CHUNK, nh), 1)
    dtt3_s = zpad(dtt_all[:, tp:].reshape(nh, bs, CHUNK).transpose(1, 0, 2), 2)
    pad_conv = lambda c0: jnp.concatenate(
        [jnp.zeros((c0.shape[0], SUBLANES - tail, cc), F32), c0], axis=1)
    m_p, ssm_p = _ssd(z_all, xbc_all, dt3_p, dtt3_p, jnp.zeros((bp, SUBLANES, cc), F32),
                      jnp.zeros((bp, ci, D_STATE), F32), conv_w, conv_b, a_log, dsk, g_ssm, eh,
                      0, bp, ncq, lcp, lcp, groups)
    m_s, ssm_s = _ssd(z_all, xbc_all, dt3_s, dtt3_s, pad_conv(conv_s0),
                      ssm_s0.reshape(bs, ci, D_STATE), conv_w, conv_b, a_log, dsk, g_ssm, eh,
                      tp, bs, 1, lcs, CHUNK, groups)

    mixed = _mix(a_p, a_s, m_p, m_s, gates, w_pa.astype(BF16), w_pb.astype(BF16))
    wr = jnp.concatenate([w_router, jnp.zeros((d, LANES - nexp), F32)], axis=1)
    wr_hi = wr.astype(BF16)
    wr_lo = (wr - wr_hi.astype(F32)).astype(BF16)
    br = jnp.concatenate([b_router, jnp.zeros((LANES - nexp,), F32)]).reshape(1, LANES)
    x1, h2, ti, tpr = _post(mixed, w_out.astype(BF16), x_all, gt1, sc2, sh2, g_norm2, wr_hi, wr_lo, br, nexp)

    assert nexp <= LANES
    bm = MOE_BLOCK
    n_blocks = -(-(t * TOP_K + nexp * (bm - 1)) // bm)
    tb = _pick(math.gcd(tp, ts), (128,))
    dest3, meta = _moe_slots(ti[:, :SUBLANES].T, bm, tb)
    block_e, n_used, zstart = _block_layout(meta, nexp, bm, n_blocks)
    xg = _moe_dispatch(zstart, dest3, h2, n_blocks * bm, bm)
    glu = _moe_gateup(block_e, n_used, xg, w_gate, w_up, b_gate, b_up)
    yb = _moe_down(block_e, n_used, glu, w_down, b_down)
    y_p = _moe_combine(dest3, yb, x1, gt2, tpr, 0, tp)
    y_s = _moe_combine(dest3, yb, x1, gt2, tpr, tp, ts)

    return (y_p.reshape(bp, sp, d), y_s.reshape(bs, ss, d),
            lat_all[:tp].reshape(bp, sp, kvl), kr_all[:tp].reshape(bp, sp, ROPE_DIM),
            ssm_p.reshape(bp, nh, M_HEADDIM, D_STATE), conv_p,
            lat_all[tp:].reshape(bs, ss, kvl), kr_all[tp:].reshape(bs, ss, ROPE_DIM),
            ssm_s.reshape(bs, nh, M_HEADDIM, D_STATE), conv_s)


def kernel(x_prompt, x_sample, cache_mla_latent, cache_mla_krope, state_ssm, state_conv, c_prompt, c_sample,
           w_ada, b_ada, g_norm1, w_in, g_cq, g_ckv, w_uq, w_uk, w_uv, g_qn, g_kn, conv_w, conv_b, dt_bias,
           a_log, d_skip, g_ssm, w_pa, w_pb, w_out, g_norm2, w_router, b_router, w_gate, b_gate, w_up, b_up,
           w_down, b_down):
    depth = w_ada.shape[0]
    assert depth == 1, "single-layer encoder"
    weights = (w_ada, b_ada, g_norm1, w_in, g_cq, g_ckv, w_uq, w_uk, w_uv, g_qn, g_kn, conv_w, conv_b, dt_bias,
               a_log, d_skip, g_ssm, w_pa, w_pb, w_out, g_norm2, w_router, b_router, w_gate, b_gate, w_up, b_up,
               w_down, b_down)
    outs = _layer(x_prompt, x_sample, cache_mla_latent[0], cache_mla_krope[0], state_ssm[0], state_conv[0],
                  c_prompt, c_sample, *[w[0] for w in weights])
    y_p, y_s = outs[0], outs[1]
    return (y_p, y_s) + tuple(o[None] for o in outs[2:6]) + tuple(o[None] for o in outs[6:])
```

```python
import functools
import math

import jax
import jax.numpy as jnp
from jax import lax
from jax.experimental import pallas as pl
from jax.experimental.pallas import tpu as pltpu

F32 = jnp.float32
BF16 = jnp.bfloat16
I32 = jnp.int32

CHUNK = 64
NOPE_DIM = 128
ROPE_DIM = 64
QK_DIM = NOPE_DIM + ROPE_DIM
V_DIM = 128
HEAD_PAD = 256
ROPE_THETA = 10000.0
ATTN_SCALE = QK_DIM ** -0.5
M_HEADDIM = 64
D_STATE = 128
CONV_W = 4
TOP_K = 4
SWIGLU_LIMIT = 7.0
SWIGLU_ALPHA = 1.702
EPS = 1e-6

LANES = 128
SUBLANES = 8
VMEM_LIMIT = 56 * 1024 * 1024

MOE_BLOCK = 256
NEG_BIG = -1e30

_NT = (((1,), (1,)), ((), ()))
_TN = (((0,), (0,)), ((), ()))


def _cp(*sem):
    return pltpu.CompilerParams(dimension_semantics=sem, vmem_limit_bytes=VMEM_LIMIT)


def _pick(n, prefs):
    for p in prefs:
        if n % p == 0:
            return p
    return n


def _dot(a, b):
    return jnp.dot(a, b, preferred_element_type=F32)


def _split3(v):
    hi = v.astype(BF16)
    r1 = v - hi.astype(F32)
    mid = r1.astype(BF16)
    lo = (r1 - mid.astype(F32)).astype(BF16)
    return hi, mid, lo


def _silu(x):
    return x * jax.nn.sigmoid(x)


def _softplus(x):
    return jnp.maximum(x, 0.0) + jnp.log1p(jnp.exp(-jnp.abs(x)))


def _ada_kernel(c_ref, w_ref, b_ref, o_ref):
    s = _silu(c_ref[...]).astype(BF16)
    o_ref[...] = _dot(s, w_ref[...].astype(BF16)) + b_ref[...]


def _ada(c_all, w_ada, b_ada):
    r, d = c_all.shape
    n = w_ada.shape[1]
    tn = _pick(n, (1024, 512, 256, 128))
    return pl.pallas_call(
        _ada_kernel,
        out_shape=jax.ShapeDtypeStruct((r, n), F32),
        grid=(n // tn,),
        in_specs=[pl.BlockSpec((r, d), lambda j: (0, 0)),
                  pl.BlockSpec((d, tn), lambda j: (0, j)),
                  pl.BlockSpec((1, tn), lambda j: (0, j))],
        out_specs=pl.BlockSpec((r, tn), lambda j: (0, j)),
        compiler_params=_cp("arbitrary"),
        name="ada",
    )(c_all, w_ada, b_ada.reshape(1, n))


def _norm1_kernel(xp_ref, xs_ref, g_ref, sc_ref, sh_ref, h_ref, xall_ref, *, npb):
    i = pl.program_id(0)
    x = jnp.where(i < npb, xp_ref[...], xs_ref[...])
    xall_ref[...] = x
    xn = x * lax.rsqrt(jnp.mean(x * x, axis=-1, keepdims=True) + EPS)
    h_ref[...] = (xn * g_ref[...] * (1.0 + sc_ref[...]) + sh_ref[...]).astype(BF16)


def _norm1(xp3, xs3, g, sc, sh):
    ncp, _, d = xp3.shape
    ncs = xs3.shape[0]
    nch = ncp + ncs
    gc = _pick(math.gcd(ncp, ncs), (4, 2, 1))
    npb = ncp // gc
    blk = (gc, CHUNK, d)
    mod = pl.BlockSpec((gc, 1, d), lambda i: (i, 0, 0))
    return pl.pallas_call(
        functools.partial(_norm1_kernel, npb=npb),
        out_shape=(jax.ShapeDtypeStruct((nch, CHUNK, d), BF16),
                   jax.ShapeDtypeStruct((nch, CHUNK, d), F32)),
        grid=(nch // gc,),
        in_specs=[pl.BlockSpec(blk, lambda i: (jnp.minimum(i, npb - 1), 0, 0)),
                  pl.BlockSpec(blk, lambda i: (jnp.maximum(i - npb, 0), 0, 0)),
                  pl.BlockSpec((1, 1, d), lambda i: (0, 0, 0)),
                  mod, mod],
        out_specs=(pl.BlockSpec(blk, lambda i: (i, 0, 0)),
                   pl.BlockSpec(blk, lambda i: (i, 0, 0))),
        compiler_params=_cp("arbitrary"),
        name="norm1",
    )(xp3, xs3, g.reshape(1, 1, d), sc, sh)


def _mm_kernel(x_ref, w_ref, o_ref, *, act):
    acc = _dot(x_ref[...], w_ref[...])
    if act == "sigmoid":
        acc = jax.nn.sigmoid(acc)
    o_ref[...] = acc.astype(o_ref.dtype)


def _mm(x, w, out_dtype, name, act=None):
    m, k = x.shape
    n = w.shape[1]
    tm = _pick(m, (1024, 512, 256))
    tn = _pick(n, (1024, 512, 256, 128))
    return pl.pallas_call(
        functools.partial(_mm_kernel, act=act),
        out_shape=jax.ShapeDtypeStruct((m, n), out_dtype),
        grid=(m // tm, n // tn),
        in_specs=[pl.BlockSpec((tm, k), lambda i, j: (i, 0)),
                  pl.BlockSpec((k, tn), lambda i, j: (0, j))],
        out_specs=pl.BlockSpec((tm, tn), lambda i, j: (i, j)),
        compiler_params=_cp("arbitrary", "arbitrary"),
        name=name,
    )(x, w)


def _lat_kernel(h_ref, w_ref, wdt_ref, gcq_ref, gckv_ref, cos_ref, sin_ref, dtb_ref, dtbc_ref,
                cqn_ref, lat_ref, latk_ref, kr_ref, dt_ref, dtt_ref, *, ql, kvl):
    h = h_ref[...]
    acc = _dot(h, w_ref[...])
    cq = acc[:, :ql]
    cqn = cq * lax.rsqrt(jnp.mean(cq * cq, axis=-1, keepdims=True) + EPS) * gcq_ref[...]
    cqn_ref[...] = cqn.astype(BF16)
    ckv = acc[:, ql:ql + kvl]
    lat = ckv * lax.rsqrt(jnp.mean(ckv * ckv, axis=-1, keepdims=True) + EPS) * gckv_ref[...]
    lat_ref[...] = lat
    o = ql + kvl
    kr128 = acc[:, o:o + LANES] * cos_ref[...] + acc[:, o + LANES:o + 2 * LANES] * sin_ref[...]
    kr_ref[...] = kr128[:, :ROPE_DIM]
    latk_ref[:, :kvl] = lat.astype(BF16)
    latk_ref[:, kvl:] = kr128.astype(BF16)
    nh = dt_ref.shape[-1]
    dt_ref[...] = _softplus(acc[:, o + 2 * LANES:o + 2 * LANES + nh] + dtb_ref[...])
    dtt = lax.dot_general(wdt_ref[...], h, _NT, preferred_element_type=F32)
    dtt_ref[...] = _softplus(dtt + dtbc_ref[...])


def _lat(h_all, w_lat, w_dt_t, g_cq, g_ckv, cos128, sin128, dt_bias):
    t, d = h_all.shape
    ql, kvl = g_cq.shape[-1], g_ckv.shape[-1]
    nh = dt_bias.shape[-1]
    nl = w_lat.shape[1]
    tm = _pick(t, (512, 256, 128))
    row = lambda w: pl.BlockSpec((tm, w), lambda i: (i, 0))
    const = lambda a, b: pl.BlockSpec((a, b), lambda i: (0, 0))
    return pl.pallas_call(
        functools.partial(_lat_kernel, ql=ql, kvl=kvl),
        out_shape=(jax.ShapeDtypeStruct((t, ql), BF16),
                   jax.ShapeDtypeStruct((t, kvl), F32),
                   jax.ShapeDtypeStruct((t, kvl + LANES), BF16),
                   jax.ShapeDtypeStruct((t, ROPE_DIM), F32),
                   jax.ShapeDtypeStruct((t, nh), F32),
                   jax.ShapeDtypeStruct((nh, t), F32)),
        grid=(t // tm,),
        in_specs=[row(d), const(d, nl), const(nh, d), const(1, ql), const(1, kvl),
                  row(LANES), row(LANES), const(1, nh), const(nh, 1)],
        out_specs=(row(ql), row(kvl), row(kvl + LANES), row(ROPE_DIM), row(nh),
                   pl.BlockSpec((nh, tm), lambda i: (0, i))),
        compiler_params=_cp("arbitrary"),
        name="latent_proj",
    )(h_all, w_lat, w_dt_t, g_cq.reshape(1, ql), g_ckv.reshape(1, kvl), cos128, sin128,
      dt_bias.reshape(1, nh), dt_bias.reshape(nh, 1))


def _q_kernel(c_ref, wa_ref, wb_ref, cos_ref, sin_ref, gn_ref, gr_ref, o_ref, *, heads):
    c = c_ref[...]
    a = _dot(c, wa_ref[...])
    b = _dot(c, wb_ref[...])
    cos, sin = cos_ref[...], sin_ref[...]
    for h in range(heads):
        nope = a[:, h * HEAD_PAD:h * HEAD_PAD + LANES]
        rope = a[:, h * HEAD_PAD + LANES:(h + 1) * HEAD_PAD] * cos + b[:, h * LANES:(h + 1) * LANES] * sin
        ss = (jnp.sum(nope * nope, axis=-1, keepdims=True)
              + jnp.sum(rope * rope, axis=-1, keepdims=True)) * (1.0 / QK_DIM)
        r = lax.rsqrt(ss + EPS)
        o_ref[:, h * HEAD_PAD:h * HEAD_PAD + LANES] = (nope * r * gn_ref[...]).astype(BF16)
        o_ref[:, h * HEAD_PAD + LANES:(h + 1) * HEAD_PAD] = (rope * r * gr_ref[...]).astype(BF16)


def _qproj(cqn, wq_a, wq_b, cos128, sin128, g_nope, g_rope, heads):
    t, ql = cqn.shape
    tm = _pick(t, (256, 128))
    row = lambda w: pl.BlockSpec((tm, w), lambda i: (i, 0))
    const = lambda a, b: pl.BlockSpec((a, b), lambda i: (0, 0))
    return pl.pallas_call(
        functools.partial(_q_kernel, heads=heads),
        out_shape=jax.ShapeDtypeStruct((t, heads * HEAD_PAD), BF16),
        grid=(t // tm,),
        in_specs=[row(ql), const(ql, heads * HEAD_PAD), const(ql, heads * LANES),
                  row(LANES), row(LANES), const(1, LANES), const(1, LANES)],
        out_specs=row(heads * HEAD_PAD),
        compiler_params=_cp("arbitrary"),
        name="q_proj",
    )(cqn, wq_a, wq_b, cos128, sin128, g_nope, g_rope)


def _kv_kernel(lat_ref, kr_ref, w_ref, k_ref, v_ref, *, heads):
    acc = _dot(lat_ref[...], w_ref[...])
    kr = kr_ref[...].astype(F32)
    kr2 = jnp.sum(kr * kr, axis=-1, keepdims=True)
    for h in range(heads):
        kn = acc[:, h * LANES:(h + 1) * LANES]
        ss = (jnp.sum(kn * kn, axis=-1, keepdims=True) + kr2) * (1.0 / QK_DIM)
        r = lax.rsqrt(ss + EPS)
        k_ref[:, h * HEAD_PAD:h * HEAD_PAD + LANES] = (kn * r).astype(BF16)
        k_ref[:, h * HEAD_PAD + LANES:(h + 1) * HEAD_PAD] = (kr * r).astype(BF16)
    v_ref[...] = acc[:, heads * LANES:].astype(BF16)


def _kvproj(latk, w_kv, tp, heads, kvl):
    tm = _pick(tp, (256, 128))
    return pl.pallas_call(
        functools.partial(_kv_kernel, heads=heads),
        out_shape=(jax.ShapeDtypeStruct((tp, heads * HEAD_PAD), BF16),
                   jax.ShapeDtypeStruct((tp, heads * V_DIM), BF16)),
        grid=(tp // tm,),
        in_specs=[pl.BlockSpec((tm, kvl), lambda i: (i, 0)),
                  pl.BlockSpec((tm, LANES), lambda i: (i, kvl // LANES)),
                  pl.BlockSpec((kvl, 2 * heads * LANES), lambda i: (0, 0))],
        out_specs=(pl.BlockSpec((tm, heads * HEAD_PAD), lambda i: (i, 0)),
                   pl.BlockSpec((tm, heads * V_DIM), lambda i: (i, 0))),
        compiler_params=_cp("arbitrary"),
        name="kv_proj",
    )(latk, latk, w_kv)


def _attn_p_kernel(q_ref, k_ref, v_ref, o_ref, m_ref, l_ref, acc_ref, *, tq, hg):
    qi = pl.program_id(2)
    m_ref[...] = jnp.full(m_ref.shape, -jnp.inf, F32)
    l_ref[...] = jnp.zeros(l_ref.shape, F32)
    acc_ref[...] = jnp.zeros(acc_ref.shape, F32)

    def tile(j, masked):
        ks = pl.multiple_of(j * tq, tq)
        for g in range(hg):
            q = q_ref[:, g * HEAD_PAD:(g + 1) * HEAD_PAD]
            k = k_ref[pl.ds(ks, tq), g * HEAD_PAD:(g + 1) * HEAD_PAD]
            v = v_ref[pl.ds(ks, tq), g * V_DIM:(g + 1) * V_DIM]
            s = lax.dot_general(q, k, _NT, preferred_element_type=F32)
            if masked:
                rc = lax.broadcasted_iota(I32, (tq, tq), 0) // CHUNK
                cc = lax.broadcasted_iota(I32, (tq, tq), 1) // CHUNK
                s = jnp.where(cc <= rc, s, -jnp.inf)
            m_prev = m_ref[g]
            m_new = jnp.maximum(m_prev, jnp.max(s, axis=-1, keepdims=True))
            alpha = jnp.exp2(m_prev - m_new)
            p = jnp.exp2(s - jnp.tile(m_new, (1, tq // LANES)))
            l_ref[g] = alpha * l_ref[g] + jnp.sum(p, axis=-1, keepdims=True)
            acc_ref[g] = alpha * acc_ref[g] + _dot(p.astype(BF16), v)
            m_ref[g] = m_new

    def body(j, carry):
        tile(j, False)
        return carry

    lax.fori_loop(0, qi, body, 0)
    tile(qi, True)
    for g in range(hg):
        o_ref[:, g * V_DIM:(g + 1) * V_DIM] = (acc_ref[g] / l_ref[g]).astype(o_ref.dtype)


def _attn_prompt(q_all, k_p, v_p, batch, seq, heads):
    tq = _pick(seq, (512, 256, 128, 64))
    nq = seq // tq
    hg = _pick(heads, (2, 1))
    return pl.pallas_call(
        functools.partial(_attn_p_kernel, tq=tq, hg=hg),
        out_shape=jax.ShapeDtypeStruct((batch * seq, heads * V_DIM), BF16),
        grid=(batch, heads // hg, nq),
        in_specs=[pl.BlockSpec((tq, hg * HEAD_PAD), lambda b, h, qi: (b * nq + qi, h)),
                  pl.BlockSpec((seq, hg * HEAD_PAD), lambda b, h, qi: (b, h)),
                  pl.BlockSpec((seq, hg * V_DIM), lambda b, h, qi: (b, h))],
        out_specs=pl.BlockSpec((tq, hg * V_DIM), lambda b, h, qi: (b * nq + qi, h)),
        scratch_shapes=[pltpu.VMEM((hg, tq, LANES), F32), pltpu.VMEM((hg, tq, LANES), F32),
                        pltpu.VMEM((hg, tq, V_DIM), F32)],
        compiler_params=_cp("arbitrary", "arbitrary", "arbitrary"),
        name="attn_prompt",
    )(q_all, k_p, v_p)


def _attn_s_kernel(pl_ref, pk_ref, nl_ref, q_ref, wukt_ref, wuv_ref, o_ref,
                   qabs_ref, qr_ref, m_ref, l_ref, acc_ref, *, heads, kvl, nkb):
    kb = pl.program_id(1)

    @pl.when(kb == 0)
    def _():
        for h in range(heads):
            qn = q_ref[:, h * HEAD_PAD:h * HEAD_PAD + LANES]
            qabs_ref[h * CHUNK:(h + 1) * CHUNK, :] = _dot(
                qn, wukt_ref[h * LANES:(h + 1) * LANES, :]).astype(BF16)
            qr_ref[h * CHUNK:(h + 1) * CHUNK, :] = q_ref[:, h * HEAD_PAD + LANES:(h + 1) * HEAD_PAD]
        m_ref[...] = jnp.full(m_ref.shape, -jnp.inf, F32)
        l_ref[...] = jnp.zeros(l_ref.shape, F32)
        acc_ref[...] = jnp.zeros(acc_ref.shape, F32)

    def block(xl, krf):
        rtop = lax.dot_general(wukt_ref[...], xl, _NT, preferred_element_type=F32)
        sq = krf * krf
        sq_hi = sq.astype(BF16)
        sq_lo = (sq - sq_hi.astype(F32)).astype(BF16)
        ones = jnp.ones((SUBLANES, ROPE_DIM), BF16)
        kr2 = (lax.dot_general(ones, sq_hi, _NT, preferred_element_type=F32)
               + lax.dot_general(ones, sq_lo, _NT, preferred_element_type=F32))[0:1, :]
        s = (lax.dot_general(qabs_ref[...], xl, _NT, preferred_element_type=F32)
             + lax.dot_general(qr_ref[:, :ROPE_DIM], krf.astype(BF16), _NT,
                               preferred_element_type=F32))
        parts = []
        for h in range(heads):
            rt = rtop[h * LANES:(h + 1) * LANES, :]
            kn2 = jnp.sum(rt * rt, axis=0, keepdims=True)
            r = lax.rsqrt((kn2 + kr2) * (1.0 / QK_DIM) + EPS)
            parts.append(s[h * CHUNK:(h + 1) * CHUNK, :] * r)
        s = jnp.concatenate(parts, axis=0)
        n = s.shape[1]
        m_prev = m_ref[...]
        m_new = jnp.maximum(m_prev, jnp.max(s, axis=-1, keepdims=True))
        alpha = jnp.exp2(m_prev - m_new)
        m_wide = jnp.tile(m_new, (1, n // LANES)) if n >= LANES else m_new[:, :n]
        p = jnp.exp2(s - m_wide)
        l_ref[...] = alpha * l_ref[...] + jnp.sum(p, axis=-1, keepdims=True)
        acc_ref[...] = jnp.tile(alpha, (1, kvl // LANES)) * acc_ref[...] + _dot(p.astype(BF16), xl)
        m_ref[...] = m_new

    @pl.when(kb < nkb)
    def _():
        block(pl_ref[0].astype(BF16), pk_ref[0])

    @pl.when(kb == nkb)
    def _():
        block(nl_ref[:, :kvl], nl_ref[:, kvl:kvl + ROPE_DIM].astype(F32))
        o = (acc_ref[...] / jnp.tile(l_ref[...], (1, kvl // LANES))).astype(BF16)
        for h in range(heads):
            o_ref[:, h * V_DIM:(h + 1) * V_DIM] = _dot(
                o[h * CHUNK:(h + 1) * CHUNK, :], wuv_ref[:, h * V_DIM:(h + 1) * V_DIM]).astype(o_ref.dtype)


def _attn_sample(past_lat, past_kr, latk, q_all, w_uk_t, w_uv2, tp, heads):
    bs, past, kvl = past_lat.shape
    tk = _pick(past, (512, 256, 128))
    nkb = past // tk
    c0 = tp // CHUNK
    pidx = lambda b, kb: (b, jnp.minimum(kb, nkb - 1), 0)
    hq = heads * CHUNK
    return pl.pallas_call(
        functools.partial(_attn_s_kernel, heads=heads, kvl=kvl, nkb=nkb),
        out_shape=jax.ShapeDtypeStruct((bs * CHUNK, heads * V_DIM), BF16),
        grid=(bs, nkb + 1),
        in_specs=[pl.BlockSpec((1, tk, kvl), pidx),
                  pl.BlockSpec((1, tk, ROPE_DIM), pidx),
                  pl.BlockSpec((CHUNK, kvl + LANES), lambda b, kb: (c0 + b, 0)),
                  pl.BlockSpec((CHUNK, heads * HEAD_PAD), lambda b, kb: (c0 + b, 0)),
                  pl.BlockSpec((heads * LANES, kvl), lambda b, kb: (0, 0)),
                  pl.BlockSpec((kvl, heads * V_DIM), lambda b, kb: (0, 0))],
        out_specs=pl.BlockSpec((CHUNK, heads * V_DIM), lambda b, kb: (b, 0)),
        scratch_shapes=[pltpu.VMEM((hq, kvl), BF16), pltpu.VMEM((hq, LANES), BF16),
                        pltpu.VMEM((hq, LANES), F32), pltpu.VMEM((hq, LANES), F32),
                        pltpu.VMEM((hq, kvl), F32)],
        compiler_params=_cp("arbitrary", "arbitrary"),
        name="attn_sample",
    )(past_lat, past_kr, latk, q_all, w_uk_t, w_uv2)


def _ssd_kernel(z_ref, x_ref, dt_ref, dtt_ref, conv0_ref, ssm0_ref, cw_ref, cb_ref, alr_ref, alc_ref,
                dsk_ref, gs_ref, eh_ref, o_ref, st_out_ref, xs_ref, st_ref,
                *, lc, lr, nh, groups):
    c = pl.program_id(1)
    p = M_HEADDIM
    n = D_STATE
    ci = nh * p
    k8 = nh // groups
    gw = k8 * p

    @pl.when(c == 0)
    def _():
        xs_ref[0:SUBLANES, :] = conv0_ref[0]
        st_ref[...] = ssm0_ref[0]

    xs_ref[SUBLANES:SUBLANES + lr, :] = x_ref[...].astype(F32)
    if lc > lr:
        xs_ref[SUBLANES + lr:SUBLANES + lc, :] = jnp.zeros((lc - lr, xs_ref.shape[1]), F32)

    def conv(lo, hi):
        u = xs_ref[SUBLANES - 3:SUBLANES - 3 + lc, lo:hi] * cw_ref[0:1, lo:hi]
        for tap in range(1, CONV_W):
            u = u + xs_ref[SUBLANES - 3 + tap:SUBLANES - 3 + tap + lc, lo:hi] * cw_ref[tap:tap + 1, lo:hi]
        return _silu(u + cb_ref[:, lo:hi])

    dt = dt_ref[0]
    dtt = dtt_ref[0]
    a_row = -jnp.exp(alr_ref[...])
    a_col = -jnp.exp(alc_ref[...])
    ri = lax.broadcasted_iota(I32, (lc, lc), 0)
    cidx = lax.broadcasted_iota(I32, (lc, lc), 1)
    tri = ri >= cidx
    tril = jnp.where(tri, 1.0, 0.0).astype(BF16)
    triu = jnp.where(ri <= cidx, 1.0, 0.0).astype(BF16)
    cs = sum(_dot(tril, piece) for piece in _split3(dt * a_row))
    cst = sum(_dot(piece, triu) for piece in _split3(dtt * a_col))
    exp_cs = jnp.exp(cs)
    w_end = jnp.exp(cs[lc - 1:lc, :] - cs)
    stacked = jnp.concatenate([dt, exp_cs, w_end], axis=0)
    eh = eh_ref[...]
    expanded = sum(_dot(piece, eh) for piece in _split3(stacked))
    dt_e, ecs_e, wend_e = expanded[:lc], expanded[lc:2 * lc], expanded[2 * lc:]
    cdec = jnp.exp(cst[:, lc - 1:lc])
    lane_lo = lax.broadcasted_iota(I32, (lc, LANES), 1) < p

    for g in range(groups):
        gs = slice(g * gw, (g + 1) * gw)
        xg = conv(g * gw, (g + 1) * gw)
        bg = conv(ci + g * n, ci + (g + 1) * n).astype(BF16)
        cg = conv(ci + groups * n + g * n, ci + groups * n + (g + 1) * n).astype(BF16)
        cbm = lax.dot_general(cg, bg, _NT, preferred_element_type=F32)
        xdt = xg * dt_e[:, gs]
        xdt_b = xdt.astype(BF16)
        pairs = []
        for q in range(k8 // 2):
            x2 = xdt_b[:, q * LANES:(q + 1) * LANES]
            ys = []
            for s in range(2):
                h = g * k8 + 2 * q + s
                seg = cs[:, h:h + 1] - cst[h:h + 1, :]
                dec = jnp.exp(jnp.where(tri, seg, -jnp.inf))
                ys.append(_dot((cbm * dec).astype(BF16), x2))
            pairs.append(jnp.where(lane_lo, ys[0], ys[1]))
        y_diag = jnp.concatenate(pairs, axis=1)
        sg = st_ref[g * gw:(g + 1) * gw, :]
        y_off = lax.dot_general(cg, sg.astype(BF16), _NT, preferred_element_type=F32) * ecs_e[:, gs]
        y = y_diag + y_off + xg * dsk_ref[:, gs]
        xw = (xdt * wend_e[:, gs]).astype(BF16)
        upd = lax.dot_general(xw, bg, _TN, preferred_element_type=F32)
        for k in range(k8):
            h = g * k8 + k
            rows = slice(g * gw + k * p, g * gw + (k + 1) * p)
            st_ref[rows, :] = st_ref[rows, :] * cdec[h:h + 1, :] + upd[k * p:(k + 1) * p, :]
        zg = z_ref[:, gs].astype(F32)
        u2 = y[:lr] * _silu(zg)
        ms = jnp.mean(u2 * u2, axis=-1, keepdims=True)
        o_ref[:, gs] = (u2 * lax.rsqrt(ms + EPS) * gs_ref[:, gs]).astype(o_ref.dtype)

    xs_ref[0:SUBLANES, :] = xs_ref[lr:lr + SUBLANES, :]

    @pl.when(c == pl.num_programs(1) - 1)
    def _():
        st_out_ref[0] = st_ref[...]


def _ssd(z_all, xbc_all, dt3, dtt3, conv0p, ssm0, conv_w, conv_b, a_log, dsk, g_ssm, eh,
         row0, nseq, nchunk, lc, lr, groups):
    ci = z_all.shape[1]
    cc = xbc_all.shape[1]
    nh = ci // M_HEADDIM
    rb0 = row0 // lr
    rowblk = lambda w: pl.BlockSpec((lr, w), lambda b, c: (rb0 + b * nchunk + c, 0))
    seq3 = lambda a, b_: pl.BlockSpec((1, a, b_), lambda b, c: (b * nchunk + c, 0, 0))
    perb = lambda a, b_: pl.BlockSpec((1, a, b_), lambda b, c: (b, 0, 0))
    const = lambda a, b_: pl.BlockSpec((a, b_), lambda b, c: (0, 0))
    return pl.pallas_call(
        functools.partial(_ssd_kernel, lc=lc, lr=lr, nh=nh, groups=groups),
        out_shape=(jax.ShapeDtypeStruct((nseq * nchunk * lr, ci), BF16),
                   jax.ShapeDtypeStruct((nseq, ci, D_STATE), F32)),
        grid=(nseq, nchunk),
        in_specs=[rowblk(ci), rowblk(cc), seq3(lc, nh), seq3(nh, lc),
                  perb(SUBLANES, cc), perb(ci, D_STATE),
                  const(CONV_W, cc), const(1, cc), const(1, nh), const(nh, 1),
                  const(1, ci), const(1, ci), const(nh, ci)],
        out_specs=(pl.BlockSpec((lr, ci), lambda b, c: (b * nchunk + c, 0)),
                   perb(ci, D_STATE)),
        scratch_shapes=[pltpu.VMEM((lc + SUBLANES, cc), F32), pltpu.VMEM((ci, D_STATE), F32)],
        compiler_params=_cp("arbitrary", "arbitrary"),
        name="ssd",
    )(z_all, xbc_all, dt3, dtt3, conv0p, ssm0, conv_w, conv_b.reshape(1, cc),
      a_log.reshape(1, nh), a_log.reshape(nh, 1), dsk, g_ssm.reshape(1, ci), eh)


def _mix_kernel(ap_ref, as_ref, mp_ref, ms_ref, ga_ref, gb_ref, wpa_ref, wpb_ref, o_ref, *, npb):
    i = pl.program_id(0)
    a = jnp.where(i < npb, ap_ref[...], as_ref[...])
    m = jnp.where(i < npb, mp_ref[...], ms_ref[...])
    pa = _dot(a, wpa_ref[...])
    pb = _dot(m, wpb_ref[...])
    o_ref[...] = (ga_ref[...].astype(F32) * pa + gb_ref[...].astype(F32) * pb).astype(o_ref.dtype)


def _mix(a_p, a_s, m_p, m_s, gates, w_pa, w_pb):
    tp, hv = a_p.shape
    ts = a_s.shape[0]
    ci = m_p.shape[1]
    d = w_pa.shape[1]
    tm = _pick(math.gcd(tp, ts), (512, 256, 128, 64))
    tn = _pick(d, (512, 256, 128))
    npb = tp // tm
    nj = d // tn
    pidx = lambda i, j: (jnp.minimum(i, npb - 1), 0)
    sidx = lambda i, j: (jnp.maximum(i - npb, 0), 0)
    return pl.pallas_call(
        functools.partial(_mix_kernel, npb=npb),
        out_shape=jax.ShapeDtypeStruct((tp + ts, d), BF16),
        grid=((tp + ts) // tm, nj),
        in_specs=[pl.BlockSpec((tm, hv), pidx), pl.BlockSpec((tm, hv), sidx),
                  pl.BlockSpec((tm, ci), pidx), pl.BlockSpec((tm, ci), sidx),
                  pl.BlockSpec((tm, tn), lambda i, j: (i, j)),
                  pl.BlockSpec((tm, tn), lambda i, j: (i, nj + j)),
                  pl.BlockSpec((hv, tn), lambda i, j: (0, j)),
                  pl.BlockSpec((ci, tn), lambda i, j: (0, j))],
        out_specs=pl.BlockSpec((tm, tn), lambda i, j: (i, j)),
        compiler_params=_cp("arbitrary", "arbitrary"),
        name="branch_mix",
    )(a_p, a_s, m_p, m_s, gates, gates, w_pa, w_pb)


def _pack_bf16_pairs(x):
    n = x.shape[1] // 2
    lo = pltpu.bitcast(x[:, :n].astype(BF16).astype(F32), jnp.uint32)
    hi = pltpu.bitcast(x[:, n:].astype(BF16).astype(F32), jnp.uint32)
    return hi | (lo >> 16)


def _unpack_bf16_pairs(u):
    lo = pltpu.bitcast(u << 16, F32)
    hi = pltpu.bitcast(u & jnp.uint32(0xFFFF0000), F32)
    return jnp.concatenate([lo, hi], axis=1)


def _post_kernel(mx_ref, wout_ref, x_ref, gt1_ref, sc2_ref, sh2_ref, g2_ref, wrh_ref, wrl_ref, br_ref,
                 x1_ref, h2p_ref, ti_ref, tp_ref, h2_ref, *, nexp):
    o = _dot(mx_ref[...], wout_ref[...])
    tm = o.shape[0]
    for ch in range(tm // CHUNK):
        rows = slice(ch * CHUNK, (ch + 1) * CHUNK)
        x1 = x_ref[rows, :] + gt1_ref[ch] * o[rows, :]
        x1_ref[rows, :] = x1
        xn = x1 * lax.rsqrt(jnp.mean(x1 * x1, axis=-1, keepdims=True) + EPS)
        h2_ref[rows, :] = xn * g2_ref[...] * (1.0 + sc2_ref[ch]) + sh2_ref[ch]
    h2 = h2_ref[...]
    h2p_ref[...] = _pack_bf16_pairs(h2)
    hh = h2.astype(BF16)
    hl = (h2 - hh.astype(F32)).astype(BF16)
    logits = _dot(hh, wrh_ref[...]) + _dot(hh, wrl_ref[...]) + _dot(hl, wrh_ref[...]) + br_ref[...]
    lane = lax.broadcasted_iota(I32, logits.shape, 1)
    logits = jnp.where(lane < nexp, logits, NEG_BIG)
    vals, idxs = [], []
    for _ in range(TOP_K):
        m = jnp.max(logits, axis=-1, keepdims=True)
        idx = jnp.min(jnp.where(logits == m, lane, LANES), axis=-1, keepdims=True)
        vals.append(m)
        idxs.append(idx)
        logits = jnp.where(lane == idx, 2.0 * NEG_BIG, logits)
    es = [jnp.exp(v - vals[0]) for v in vals]
    den = es[0]
    for e in es[1:]:
        den = den + e
    ti = jnp.zeros(lane.shape, I32)
    tpv = jnp.zeros(lane.shape, F32)
    for k in range(TOP_K):
        ti = jnp.where(lane == k, idxs[k], ti)
        tpv = jnp.where(lane == k, es[k] / den, tpv)
    ti_ref[...] = ti
    tp_ref[...] = tpv


def _post(mixed, w_out, x_all, gt1, sc2, sh2, g2, wr_hi, wr_lo, br, nexp):
    t, d = mixed.shape
    tm = _pick(t, (256, 128, 64))
    nc = tm // CHUNK
    row = lambda w: pl.BlockSpec((tm, w), lambda i: (i, 0))
    mod = pl.BlockSpec((nc, 1, d), lambda i: (i, 0, 0))
    const = lambda a, b: pl.BlockSpec((a, b), lambda i: (0, 0))
    return pl.pallas_call(
        functools.partial(_post_kernel, nexp=nexp),
        out_shape=(jax.ShapeDtypeStruct((t, d), F32), jax.ShapeDtypeStruct((t, d // 2), jnp.uint32),
                   jax.ShapeDtypeStruct((t, LANES), I32), jax.ShapeDtypeStruct((t, LANES), F32)),
        grid=(t // tm,),
        in_specs=[row(d), const(d, d), row(d), mod, mod, mod, const(1, d),
                  const(d, LANES), const(d, LANES), const(1, LANES)],
        out_specs=(row(d), row(d // 2), row(LANES), row(LANES)),
        scratch_shapes=[pltpu.VMEM((tm, d), F32)],
        compiler_params=_cp("arbitrary"),
        name="post_mix",
    )(mixed, w_out, x_all, gt1, sc2, sh2, g2.reshape(1, d), wr_hi, wr_lo, br)


def _slots_kernel(ti_ref, dest_ref, meta_ref, run_ref, *, bm, tb):
    ph = pl.program_id(0)
    i = pl.program_id(1)
    eid = lax.broadcasted_iota(I32, (LANES, tb), 0)
    onehots = [jnp.where(eid == ti_ref[k:k + 1, :], 1.0, 0.0) for k in range(TOP_K)]
    osum = onehots[0]
    for oh in onehots[1:]:
        osum = osum + oh
    blk_cnt = jnp.sum(osum, axis=1, keepdims=True)

    @pl.when(jnp.logical_and(ph == 0, i == 0))
    def _():
        run_ref[...] = jnp.zeros(run_ref.shape, F32)

    @pl.when(ph == 0)
    def _():
        run_ref[...] = run_ref[...] + blk_cnt

    @pl.when(jnp.logical_and(ph == 1, i == 0))
    def _():
        cnt = run_ref[...]
        nblk = jnp.ceil(cnt * (1.0 / bm))
        r = lax.broadcasted_iota(I32, (LANES, LANES), 0)
        c = lax.broadcasted_iota(I32, (LANES, LANES), 1)
        lstrict = jnp.where(c < r, 1.0, 0.0).astype(BF16)
        start_blk = _dot(lstrict, jnp.broadcast_to(nblk, (LANES, LANES)).astype(BF16))
        lane = lax.broadcasted_iota(I32, (LANES, LANES), 1)
        meta_ref[...] = jnp.where(lane == 0, cnt, jnp.where(lane == 1, start_blk, 0.0))
        run_ref[...] = start_blk[:, 0:1] * float(bm)

    @pl.when(ph == 1)
    def _():
        r = lax.broadcasted_iota(I32, (tb, tb), 0)
        c = lax.broadcasted_iota(I32, (tb, tb), 1)
        ustrict = jnp.where(r < c, 1.0, 0.0).astype(BF16)
        base = run_ref[...] + _dot(osum.astype(BF16), ustrict)
        for k in range(TOP_K):
            dest_ref[0, k:k + 1, :] = jnp.sum(onehots[k] * base, axis=0, keepdims=True).astype(I32)
        dest_ref[0, TOP_K:, :] = jnp.zeros((SUBLANES - TOP_K, tb), I32)
        run_ref[...] = run_ref[...] + blk_cnt


def _moe_slots(ti_t, bm, tb):
    t = ti_t.shape[1]
    nt = t // tb
    return pl.pallas_call(
        functools.partial(_slots_kernel, bm=bm, tb=tb),
        out_shape=(jax.ShapeDtypeStruct((nt, SUBLANES, tb), I32),
                   jax.ShapeDtypeStruct((LANES, LANES), F32)),
        grid=(2, nt),
        in_specs=[pl.BlockSpec((SUBLANES, tb), lambda ph, i: (0, i))],
        out_specs=(pl.BlockSpec((1, SUBLANES, tb), lambda ph, i: (i * ph, 0, 0)),
                   pl.BlockSpec((LANES, LANES), lambda ph, i: (0, 0))),
        scratch_shapes=[pltpu.VMEM((LANES, 1), F32)],
        compiler_params=_cp("arbitrary", "arbitrary"),
        name="moe_slots",
    )(ti_t)


def _dispatch_kernel(zs_ref, idx_ref, h_ref, xg_ref, zbuf_ref, idx_smem, isem, sem, *, nz, bm, tb):
    i = pl.program_id(0)

    def zero_fill(e):
        return pltpu.make_async_copy(zbuf_ref, xg_ref.at[pl.ds(pl.multiple_of(zs_ref[e], bm), bm)], sem)

    @pl.when(i == 0)
    def _():
        zbuf_ref[...] = jnp.zeros(zbuf_ref.shape, zbuf_ref.dtype)
        for e in range(nz):
            pl.when(zs_ref[e] >= 0)(lambda e=e: zero_fill(e).start())
        for e in range(nz):
            pl.when(zs_ref[e] >= 0)(lambda e=e: zero_fill(e).wait())

    icp = pltpu.make_async_copy(idx_ref.at[0], idx_smem, isem)
    icp.start()
    icp.wait()

    def start(r, carry):
        for k in range(TOP_K):
            pltpu.make_async_copy(h_ref.at[pl.ds(r, 1)], xg_ref.at[pl.ds(idx_smem[k, r], 1)], sem).start()
        return carry

    lax.fori_loop(0, tb, start, 0)
    for k in range(TOP_K):
        pltpu.make_async_copy(h_ref, xg_ref.at[pl.ds(0, tb)], sem).wait()


def _moe_dispatch(zstart, dest3, h2, n_slots, bm):
    nt, _, tb = dest3.shape
    d = h2.shape[1]
    return pl.pallas_call(
        functools.partial(_dispatch_kernel, nz=zstart.shape[0], bm=bm, tb=tb),
        out_shape=jax.ShapeDtypeStruct((n_slots, d), h2.dtype),
        grid_spec=pltpu.PrefetchScalarGridSpec(
            num_scalar_prefetch=1,
            grid=(nt,),
            in_specs=[pl.BlockSpec((1, SUBLANES, tb), lambda i, zs: (i, 0, 0)),
                      pl.BlockSpec((tb, d), lambda i, zs: (i, 0))],
            out_specs=pl.BlockSpec(memory_space=pl.ANY),
            scratch_shapes=[pltpu.VMEM((bm, d), h2.dtype), pltpu.SMEM((SUBLANES, tb), I32),
                            pltpu.SemaphoreType.DMA, pltpu.SemaphoreType.DMA]),
        compiler_params=_cp("arbitrary"),
        name="moe_dispatch",
    )(zstart, dest3, h2)


def _weight_stream(w_refs, wbuf_ref, wb_refs, sem, cnt_ref, be_ref, nx_ref, used, tw):
    c = pl.program_id(0)
    b = pl.program_id(1)
    nc = pl.num_programs(0)

    def fetch(e, cc, slot):
        col = pl.ds(pl.multiple_of(cc * tw, tw), tw)
        return [pltpu.make_async_copy(w.at[e, :, col], wbuf_ref.at[slot, m], sem.at[slot])
                for m, w in enumerate(w_refs)]

    @pl.when(jnp.logical_and(c == 0, b == 0))
    def _():
        cnt_ref[0] = 0
        for cp in fetch(be_ref[0], 0, 0):
            cp.start()

    changed = jnp.logical_or(b == 0, be_ref[b] != be_ref[jnp.maximum(b - 1, 0)])

    @pl.when(jnp.logical_and(used, changed))
    def _():
        slot = cnt_ref[0] & 1
        for cp in fetch(be_ref[b], c, slot):
            cp.wait()
        for m, wb in enumerate(wb_refs):
            wb[...] = wbuf_ref[slot, m].astype(BF16)
        nxt = nx_ref[b]
        same_chunk = nxt >= 0
        nxt_e = jnp.where(same_chunk, nxt, be_ref[0])
        nxt_c = jnp.where(same_chunk, c, c + 1)

        @pl.when(jnp.logical_or(same_chunk, c + 1 < nc))
        def _():
            for cp in fetch(nxt_e, nxt_c, 1 - slot):
                cp.start()

        cnt_ref[0] = cnt_ref[0] + 1


def _gateup_kernel(be_ref, nx_ref, nu_ref, x_ref, bg_ref, bu_ref, wg_ref, wu_ref, o_ref,
                   wbuf_ref, wgb_ref, wub_ref, sem, cnt_ref, *, tf):
    used = pl.program_id(1) < nu_ref[0]
    _weight_stream([wg_ref, wu_ref], wbuf_ref, [wgb_ref, wub_ref], sem, cnt_ref, be_ref, nx_ref, used, tf)

    @pl.when(used)
    def _():
        x = _unpack_bf16_pairs(x_ref[...]).astype(BF16)
        gate = jnp.minimum(_dot(x, wgb_ref[...]) + bg_ref[0], SWIGLU_LIMIT)
        up = jnp.clip(_dot(x, wub_ref[...]) + bu_ref[0], -SWIGLU_LIMIT, SWIGLU_LIMIT)
        glu = gate * jax.nn.sigmoid(SWIGLU_ALPHA * gate)
        o_ref[...] = ((up + 1.0) * glu).astype(o_ref.dtype)

    @pl.when(jnp.logical_not(used))
    def _():
        o_ref[...] = jnp.zeros(o_ref.shape, o_ref.dtype)


def _moe_gateup(block_e, next_e, n_used, xg, w_gate, w_up, b_gate, b_up):
    ns, dh = xg.shape
    nexp, d, ff = w_gate.shape
    assert d == 2 * dh
    bm = MOE_BLOCK
    nb = ns // bm
    tf = _pick(ff, (1024, 512, 256, 128))
    bspec = pl.BlockSpec((1, 1, tf), lambda c, b, be, nx, nu: (be[b], 0, c))
    hbm = pl.BlockSpec(memory_space=pl.ANY)
    return pl.pallas_call(
        functools.partial(_gateup_kernel, tf=tf),
        out_shape=jax.ShapeDtypeStruct((ns, ff), BF16),
        grid_spec=pltpu.PrefetchScalarGridSpec(
            num_scalar_prefetch=3,
            grid=(ff // tf, nb),
            in_specs=[pl.BlockSpec((bm, dh), lambda c, b, be, nx, nu: (jnp.minimum(b, nu[0] - 1), 0)),
                      bspec, bspec, hbm, hbm],
            out_specs=pl.BlockSpec((bm, tf), lambda c, b, be, nx, nu: (b, c)),
            scratch_shapes=[pltpu.VMEM((2, 2, d, tf), F32), pltpu.VMEM((d, tf), BF16),
                            pltpu.VMEM((d, tf), BF16), pltpu.SemaphoreType.DMA((2,)),
                            pltpu.SMEM((1,), I32)]),
        compiler_params=_cp("arbitrary", "arbitrary"),
        name="moe_gateup",
    )(block_e, next_e, n_used, xg, b_gate.reshape(nexp, 1, ff), b_up.reshape(nexp, 1, ff), w_gate, w_up)


def _down_kernel(be_ref, nx_ref, nu_ref, g_ref, bd_ref, wd_ref, o_ref, wbuf_ref, wdb_ref, sem, cnt_ref, *, tn):
    used = pl.program_id(1) < nu_ref[0]
    _weight_stream([wd_ref], wbuf_ref, [wdb_ref], sem, cnt_ref, be_ref, nx_ref, used, tn)

    @pl.when(used)
    def _():
        o_ref[...] = _pack_bf16_pairs(_dot(g_ref[...], wdb_ref[...]) + bd_ref[0])

    @pl.when(jnp.logical_not(used))
    def _():
        o_ref[...] = jnp.zeros(o_ref.shape, o_ref.dtype)


def _moe_down(block_e, next_e, n_used, glu, w_down, b_down):
    ns, ff = glu.shape
    nexp, _, d = w_down.shape
    bm = MOE_BLOCK
    nb = ns // bm
    tn = d
    return pl.pallas_call(
        functools.partial(_down_kernel, tn=tn),
        out_shape=jax.ShapeDtypeStruct((ns, d // 2), jnp.uint32),
        grid_spec=pltpu.PrefetchScalarGridSpec(
            num_scalar_prefetch=3,
            grid=(d // tn, nb),
            in_specs=[pl.BlockSpec((bm, ff), lambda c, b, be, nx, nu: (jnp.minimum(b, nu[0] - 1), 0)),
                      pl.BlockSpec((1, 1, tn), lambda c, b, be, nx, nu: (be[b], 0, c)),
                      pl.BlockSpec(memory_space=pl.ANY)],
            out_specs=pl.BlockSpec((bm, tn // 2), lambda c, b, be, nx, nu: (b, c)),
            scratch_shapes=[pltpu.VMEM((2, 1, ff, tn), F32), pltpu.VMEM((ff, tn), BF16),
                            pltpu.SemaphoreType.DMA((2,)), pltpu.SMEM((1,), I32)]),
        compiler_params=_cp("arbitrary", "arbitrary"),
        name="moe_down",
    )(block_e, next_e, n_used, glu, b_down.reshape(nexp, 1, d), w_down)


def _combine_kernel(idx_ref, yb_ref, x1_ref, gt2_ref, p_ref, o_ref, buf_ref, idx_smem, isem, sem, *, tc):
    icp = pltpu.make_async_copy(idx_ref.at[0], idx_smem, isem)
    icp.start()
    icp.wait()

    def start(r, carry):
        for k in range(TOP_K):
            pltpu.make_async_copy(yb_ref.at[pl.ds(idx_smem[k, r], 1)],
                                  buf_ref.at[pl.ds(k * tc + r, 1)], sem).start()
        return carry

    lax.fori_loop(0, tc, start, 0)
    pltpu.make_async_copy(yb_ref.at[pl.ds(0, TOP_K * tc)], buf_ref, sem).wait()
    ff = p_ref[:, 0:1] * _unpack_bf16_pairs(buf_ref[0:tc, :])
    for k in range(1, TOP_K):
        ff = ff + p_ref[:, k:k + 1] * _unpack_bf16_pairs(buf_ref[k * tc:(k + 1) * tc, :])
    for ch in range(tc // CHUNK):
        rows = slice(ch * CHUNK, (ch + 1) * CHUNK)
        o_ref[rows, :] = x1_ref[rows, :] + gt2_ref[ch] * ff[rows, :]


def _moe_combine(dest3, yb, x1, gt2, top_p, row0, nrows):
    d = 2 * yb.shape[1]
    tc = dest3.shape[2]
    nc = tc // CHUNK
    rb0 = row0 // tc
    return pl.pallas_call(
        functools.partial(_combine_kernel, tc=tc),
        out_shape=jax.ShapeDtypeStruct((nrows, d), F32),
        grid=(nrows // tc,),
        in_specs=[pl.BlockSpec((1, SUBLANES, tc), lambda i: (rb0 + i, 0, 0)),
                  pl.BlockSpec(memory_space=pl.ANY),
                  pl.BlockSpec((tc, d), lambda i: (rb0 + i, 0)),
                  pl.BlockSpec((nc, 1, d), lambda i: (rb0 + i, 0, 0)),
                  pl.BlockSpec((tc, LANES), lambda i: (rb0 + i, 0))],
        out_specs=pl.BlockSpec((tc, d), lambda i: (i, 0)),
        scratch_shapes=[pltpu.VMEM((TOP_K * tc, d // 2), jnp.uint32), pltpu.SMEM((SUBLANES, tc), I32),
                        pltpu.SemaphoreType.DMA, pltpu.SemaphoreType.DMA],
        compiler_params=_cp("arbitrary"),
        name="moe_combine",
    )(dest3, yb, x1, gt2, top_p)


def _block_layout(meta, nexp, bm, n_blocks):
    counts = meta[:nexp, 0]
    start_blk = meta[:nexp, 1].astype(I32)
    nblk = jnp.ceil(counts * (1.0 / bm)).astype(I32)
    end_blk = start_blk + nblk
    n_used = jnp.sum(nblk)
    blk = jnp.minimum(jnp.arange(n_blocks), n_used - 1)
    block_e = jnp.minimum(jnp.sum(end_blk[None, :] <= blk[:, None], axis=1), nexp - 1).astype(I32)
    onehot_e = block_e[:, None] == jnp.arange(nexp)[None, :]
    run_end = jnp.sum(jnp.where(onehot_e, end_blk[None, :], 0), axis=1)
    at_end = run_end[:, None] == jnp.arange(n_blocks)[None, :]
    next_e = jnp.where(run_end < n_used, jnp.sum(jnp.where(at_end, block_e[None, :], 0), axis=1), -1).astype(I32)
    last = jnp.where(nblk > 0, (end_blk - 1) * bm, -1)
    spare = n_used + jnp.arange(nexp)
    spare = jnp.where(spare < n_blocks, spare * bm, -1)
    zstart = jnp.concatenate([last, spare]).astype(I32)
    return block_e, next_e, n_used.reshape(1).astype(I32), zstart


def _rope_tables(pos):
    half = ROPE_DIM // 2
    inv = ROPE_THETA ** (-jnp.arange(half, dtype=F32) / half)
    ang = pos.astype(F32)[:, None] * inv[None, :]
    z = jnp.zeros((pos.shape[0], LANES - ROPE_DIM), F32)
    cos, sin = jnp.cos(ang), jnp.sin(ang)
    return jnp.concatenate([cos, cos, z], axis=1), jnp.concatenate([sin, sin, z], axis=1)


def _rot_half_cols(w):
    half = ROPE_DIM // 2
    return jnp.concatenate([-w[..., half:], w[..., :half]], axis=-1)


def _layer(x_prompt, x_sample, past_lat, past_kr, ssm_s0, conv_s0, c_prompt, c_sample,
           w_ada, b_ada, g_norm1, w_in, g_cq, g_ckv, w_uq, w_uk, w_uv, g_qn, g_kn, conv_w, conv_b, dt_bias,
           a_log, d_skip, g_ssm, w_pa, w_pb, w_out, g_norm2, w_router, b_router, w_gate, b_gate, w_up, b_up,
           w_down, b_down):
    bp, sp, d = x_prompt.shape
    bs, ss, _ = x_sample.shape
    assert ss == CHUNK and sp % CHUNK == 0
    past = past_lat.shape[1]
    ql, kvl = g_cq.shape[-1], g_ckv.shape[-1]
    heads = w_uq.shape[1]
    ci = g_ssm.shape[-1]
    cc = conv_w.shape[-1]
    nh = ci // M_HEADDIM
    groups = (cc - ci) // (2 * D_STATE)
    nexp = w_router.shape[-1]
    tp, ts = bp * sp, bs * ss
    t = tp + ts
    ncp, ncs = tp // CHUNK, ts // CHUNK

    ada = _ada(jnp.concatenate([c_prompt, c_sample], axis=0), w_ada, b_ada)
    per_chunk = jnp.concatenate([jnp.repeat(ada[:bp], sp // CHUNK, axis=0), ada[bp:]], axis=0)
    sh1, sc1, gt1, sh2, sc2, gt2 = [m[:, None, :] for m in jnp.split(per_chunk, 6, axis=-1)]

    h3, x3 = _norm1(x_prompt.reshape(ncp, CHUNK, d), x_sample.reshape(ncs, CHUNK, d), g_norm1, sc1, sh1)
    h_all = h3.reshape(t, d)
    x_all = x3.reshape(t, d)

    o = 0
    w_cq = w_in[:, o:o + ql]; o += ql
    w_ckv = w_in[:, o:o + kvl]; o += kvl
    w_kr = w_in[:, o:o + ROPE_DIM]; o += ROPE_DIM
    w_z = w_in[:, o:o + ci]; o += ci
    w_xbc = w_in[:, o:o + cc]; o += cc
    w_dt = w_in[:, o:o + nh]; o += nh
    w_gab = w_in[:, o:o + 2 * d]
    zc = lambda n_: jnp.zeros((d, n_), F32)
    w_lat = jnp.concatenate([w_cq, w_ckv, w_kr, zc(LANES - ROPE_DIM), _rot_half_cols(w_kr),
                             zc(LANES - ROPE_DIM), w_dt, zc(LANES - nh)], axis=1).astype(BF16)

    pos = jnp.concatenate([jnp.tile(jnp.arange(sp), bp), jnp.tile(past + jnp.arange(ss), bs)])
    cos128, sin128 = _rope_tables(pos)

    cqn, lat_all, latk, kr_all, dt_all, dtt_all = _lat(
        h_all, w_lat, w_dt.T.astype(BF16), g_cq, g_ckv, cos128, sin128, dt_bias)
    z_all = _mm(h_all, w_z.astype(BF16), BF16, "proj_z")
    xbc_all = _mm(h_all, w_xbc.astype(BF16), BF16, "proj_xbc")
    gates = _mm(h_all, w_gab.astype(BF16), BF16, "proj_gates", act="sigmoid")

    tail = CONV_W - 1
    h_tail = jnp.concatenate([h3[:ncp].reshape(bp, sp, d)[:, sp - tail:, :].reshape(bp * tail, d),
                              h3[ncp:][:, CHUNK - tail:, :].reshape(bs * tail, d)], axis=0)
    conv_tail = _mm(h_tail, w_xbc.astype(BF16), F32, "proj_conv_tail")
    conv_p = conv_tail[:bp * tail].reshape(bp, tail, cc)
    conv_s = conv_tail[bp * tail:].reshape(bs, tail, cc)

    wq_a = jnp.concatenate([w_uq, jnp.zeros((ql, heads, HEAD_PAD - QK_DIM), F32)], axis=-1)
    wq_a = wq_a.reshape(ql, heads * HEAD_PAD).astype(BF16)
    wq_b = jnp.concatenate([_rot_half_cols(w_uq[..., NOPE_DIM:]),
                            jnp.zeros((ql, heads, LANES - ROPE_DIM), F32)], axis=-1)
    wq_b = wq_b.reshape(ql, heads * LANES).astype(BF16)
    gq = g_qn * g_kn * (ATTN_SCALE * math.log2(math.e))
    g_nope = gq[:NOPE_DIM].reshape(1, LANES)
    g_rope = jnp.concatenate([gq[NOPE_DIM:], jnp.zeros((LANES - ROPE_DIM,), F32)]).reshape(1, LANES)
    q_all = _qproj(cqn, wq_a, wq_b, cos128, sin128, g_nope, g_rope, heads)

    w_uk2 = w_uk.reshape(kvl, heads * NOPE_DIM)
    w_uv2 = w_uv.reshape(kvl, heads * V_DIM).astype(BF16)
    w_kv = jnp.concatenate([w_uk2.astype(BF16), w_uv2], axis=1)
    k_p, v_p = _kvproj(latk, w_kv, tp, heads, kvl)
    a_p = _attn_prompt(q_all, k_p, v_p, bp, sp, heads)
    a_s = _attn_sample(past_lat, past_kr, latk, q_all, w_uk2.T.astype(BF16), w_uv2, tp, heads)

    eh = jnp.repeat(jnp.eye(nh, dtype=BF16), M_HEADDIM, axis=1)
    dsk = jnp.repeat(d_skip, M_HEADDIM).reshape(1, ci)
    lcp = _pick(sp, (256, 128))
    assert sp % lcp == 0 and lcp % LANES == 0
    ncq = sp // lcp
    dt3_p = dt_all[:tp].reshape(bp * ncq, lcp, nh)
    dtt3_p = dtt_all[:, :tp].reshape(nh, bp * ncq, lcp).transpose(1, 0, 2)
    zpad = lambda a, axis: jnp.concatenate([a, jnp.zeros_like(a)], axis=axis)
    lcs = 2 * CHUNK
    dt3_s = zpad(dt_all[tp:].reshape(bs, CHUNK, nh), 1)
    dtt3_s = zpad(dtt_all[:, tp:].reshape(nh, bs, CHUNK).transpose(1, 0, 2), 2)
    pad_conv = lambda c0: jnp.concatenate(
        [jnp.zeros((c0.shape[0], SUBLANES - tail, cc), F32), c0], axis=1)
    m_p, ssm_p = _ssd(z_all, xbc_all, dt3_p, dtt3_p, jnp.zeros((bp, SUBLANES, cc), F32),
                      jnp.zeros((bp, ci, D_STATE), F32), conv_w, conv_b, a_log, dsk, g_ssm, eh,
                      0, bp, ncq, lcp, lcp, groups)
    m_s, ssm_s = _ssd(z_all, xbc_all, dt3_s, dtt3_s, pad_conv(conv_s0),
                      ssm_s0.reshape(bs, ci, D_STATE), conv_w, conv_b, a_log, dsk, g_ssm, eh,
                      tp, bs, 1, lcs, CHUNK, groups)

    mixed = _mix(a_p, a_s, m_p, m_s, gates, w_pa.astype(BF16), w_pb.astype(BF16))
    wr = jnp.concatenate([w_router, jnp.zeros((d, LANES - nexp), F32)], axis=1)
    wr_hi = wr.astype(BF16)
    wr_lo = (wr - wr_hi.astype(F32)).astype(BF16)
    br = jnp.concatenate([b_router, jnp.zeros((LANES - nexp,), F32)]).reshape(1, LANES)
    x1, h2, ti, tpr = _post(mixed, w_out.astype(BF16), x_all, gt1, sc2, sh2, g_norm2, wr_hi, wr_lo, br, nexp)

    assert nexp <= LANES
    bm = MOE_BLOCK
    n_blocks = -(-(t * TOP_K + nexp * (bm - 1)) // bm)
    tb = _pick(math.gcd(tp, ts), (128,))
    dest3, meta = _moe_slots(ti[:, :SUBLANES].T, bm, tb)
    block_e, next_e, n_used, zstart = _block_layout(meta, nexp, bm, n_blocks)
    xg = _moe_dispatch(zstart, dest3, h2, n_blocks * bm, bm)
    glu = _moe_gateup(block_e, next_e, n_used, xg, w_gate, w_up, b_gate, b_up)
    yb = _moe_down(block_e, next_e, n_used, glu, w_down, b_down)
    y_p = _moe_combine(dest3, yb, x1, gt2, tpr, 0, tp)
    y_s = _moe_combine(dest3, yb, x1, gt2, tpr, tp, ts)

    return (y_p.reshape(bp, sp, d), y_s.reshape(bs, ss, d),
            lat_all[:tp].reshape(bp, sp, kvl), kr_all[:tp].reshape(bp, sp, ROPE_DIM),
            ssm_p.reshape(bp, nh, M_HEADDIM, D_STATE), conv_p,
            lat_all[tp:].reshape(bs, ss, kvl), kr_all[tp:].reshape(bs, ss, ROPE_DIM),
            ssm_s.reshape(bs, nh, M_HEADDIM, D_STATE), conv_s)


def kernel(x_prompt, x_sample, cache_mla_latent, cache_mla_krope, state_ssm, state_conv, c_prompt, c_sample,
           w_ada, b_ada, g_norm1, w_in, g_cq, g_ckv, w_uq, w_uk, w_uv, g_qn, g_kn, conv_w, conv_b, dt_bias,
           a_log, d_skip, g_ssm, w_pa, w_pb, w_out, g_norm2, w_router, b_router, w_gate, b_gate, w_up, b_up,
           w_down, b_down):
    depth = w_ada.shape[0]
    assert depth == 1, "single-layer encoder"
    weights = (w_ada, b_ada, g_norm1, w_in, g_cq, g_ckv, w_uq, w_uk, w_uv, g_qn, g_kn, conv_w, conv_b, dt_bias,
               a_log, d_skip, g_ssm, w_pa, w_pb, w_out, g_norm2, w_router, b_router, w_gate, b_gate, w_up, b_up,
               w_down, b_down)
    outs = _layer(x_prompt, x_sample, cache_mla_latent[0], cache_mla_krope[0], state_ssm[0], state_conv[0],
                  c_prompt, c_sample, *[w[0] for w in weights])
    y_p, y_s = outs[0], outs[1]
    return (y_p, y_s) + tuple(o[None] for o in outs[2:6]) + tuple(o[None] for o in outs[6:])
```

```python
import functools
import math

import jax
import jax.numpy as jnp
from jax import lax
from jax.experimental import pallas as pl
from jax.experimental.pallas import tpu as pltpu

F32 = jnp.float32
BF16 = jnp.bfloat16
I32 = jnp.int32

CHUNK = 64
NOPE_DIM = 128
ROPE_DIM = 64
QK_DIM = NOPE_DIM + ROPE_DIM
V_DIM = 128
HEAD_PAD = 256
ROPE_THETA = 10000.0
ATTN_SCALE = QK_DIM ** -0.5
M_HEADDIM = 64
D_STATE = 128
CONV_W = 4
TOP_K = 4
SWIGLU_LIMIT = 7.0
SWIGLU_ALPHA = 1.702
EPS = 1e-6

LANES = 128
SUBLANES = 8
VMEM_LIMIT = 56 * 1024 * 1024

MOE_BLOCK = 256
WEIGHT_DMA_PRIORITY = 1
NEG_BIG = -1e30

_NT = (((1,), (1,)), ((), ()))
_TN = (((0,), (0,)), ((), ()))


def _cp(*sem):
    return pltpu.CompilerParams(dimension_semantics=sem, vmem_limit_bytes=VMEM_LIMIT)


def _pick(n, prefs):
    for p in prefs:
        if n % p == 0:
            return p
    return n


def _dot(a, b):
    return jnp.dot(a, b, preferred_element_type=F32)


def _split3(v):
    hi = v.astype(BF16)
    r1 = v - hi.astype(F32)
    mid = r1.astype(BF16)
    lo = (r1 - mid.astype(F32)).astype(BF16)
    return hi, mid, lo


def _silu(x):
    return x * jax.nn.sigmoid(x)


def _softplus(x):
    return jnp.maximum(x, 0.0) + jnp.log1p(jnp.exp(-jnp.abs(x)))


def _ada_kernel(c_ref, w_ref, b_ref, o_ref):
    s = _silu(c_ref[...]).astype(BF16)
    o_ref[...] = _dot(s, w_ref[...].astype(BF16)) + b_ref[...]


def _ada(c_all, w_ada, b_ada):
    r, d = c_all.shape
    n = w_ada.shape[1]
    tn = _pick(n, (1024, 512, 256, 128))
    return pl.pallas_call(
        _ada_kernel,
        out_shape=jax.ShapeDtypeStruct((r, n), F32),
        grid=(n // tn,),
        in_specs=[pl.BlockSpec((r, d), lambda j: (0, 0)),
                  pl.BlockSpec((d, tn), lambda j: (0, j)),
                  pl.BlockSpec((1, tn), lambda j: (0, j))],
        out_specs=pl.BlockSpec((r, tn), lambda j: (0, j)),
        compiler_params=_cp("arbitrary"),
        name="ada",
    )(c_all, w_ada, b_ada.reshape(1, n))


def _norm1_kernel(xp_ref, xs_ref, g_ref, sc_ref, sh_ref, h_ref, xall_ref, *, npb):
    i = pl.program_id(0)
    x = jnp.where(i < npb, xp_ref[...], xs_ref[...])
    xall_ref[...] = x
    xn = x * lax.rsqrt(jnp.mean(x * x, axis=-1, keepdims=True) + EPS)
    h_ref[...] = (xn * g_ref[...] * (1.0 + sc_ref[...]) + sh_ref[...]).astype(BF16)


def _norm1(xp3, xs3, g, sc, sh):
    ncp, _, d = xp3.shape
    ncs = xs3.shape[0]
    nch = ncp + ncs
    gc = _pick(math.gcd(ncp, ncs), (4, 2, 1))
    npb = ncp // gc
    blk = (gc, CHUNK, d)
    mod = pl.BlockSpec((gc, 1, d), lambda i: (i, 0, 0))
    return pl.pallas_call(
        functools.partial(_norm1_kernel, npb=npb),
        out_shape=(jax.ShapeDtypeStruct((nch, CHUNK, d), BF16),
                   jax.ShapeDtypeStruct((nch, CHUNK, d), F32)),
        grid=(nch // gc,),
        in_specs=[pl.BlockSpec(blk, lambda i: (jnp.minimum(i, npb - 1), 0, 0)),
                  pl.BlockSpec(blk, lambda i: (jnp.maximum(i - npb, 0), 0, 0)),
                  pl.BlockSpec((1, 1, d), lambda i: (0, 0, 0)),
                  mod, mod],
        out_specs=(pl.BlockSpec(blk, lambda i: (i, 0, 0)),
                   pl.BlockSpec(blk, lambda i: (i, 0, 0))),
        compiler_params=_cp("arbitrary"),
        name="norm1",
    )(xp3, xs3, g.reshape(1, 1, d), sc, sh)


def _mm_kernel(x_ref, w_ref, o_ref, *, act):
    acc = _dot(x_ref[...], w_ref[...])
    if act == "sigmoid":
        acc = jax.nn.sigmoid(acc)
    o_ref[...] = acc.astype(o_ref.dtype)


def _mm(x, w, out_dtype, name, act=None):
    m, k = x.shape
    n = w.shape[1]
    tm = _pick(m, (1024, 512, 256))
    tn = _pick(n, (1024, 512, 256, 128))
    return pl.pallas_call(
        functools.partial(_mm_kernel, act=act),
        out_shape=jax.ShapeDtypeStruct((m, n), out_dtype),
        grid=(m // tm, n // tn),
        in_specs=[pl.BlockSpec((tm, k), lambda i, j: (i, 0)),
                  pl.BlockSpec((k, tn), lambda i, j: (0, j))],
        out_specs=pl.BlockSpec((tm, tn), lambda i, j: (i, j)),
        compiler_params=_cp("arbitrary", "arbitrary"),
        name=name,
    )(x, w)


def _lat_kernel(h_ref, w_ref, wdt_ref, gcq_ref, gckv_ref, cos_ref, sin_ref, dtb_ref, dtbc_ref,
                cqn_ref, lat_ref, latk_ref, kr_ref, dt_ref, dtt_ref, *, ql, kvl):
    h = h_ref[...]
    acc = _dot(h, w_ref[...])
    cq = acc[:, :ql]
    cqn = cq * lax.rsqrt(jnp.mean(cq * cq, axis=-1, keepdims=True) + EPS) * gcq_ref[...]
    cqn_ref[...] = cqn.astype(BF16)
    ckv = acc[:, ql:ql + kvl]
    lat = ckv * lax.rsqrt(jnp.mean(ckv * ckv, axis=-1, keepdims=True) + EPS) * gckv_ref[...]
    lat_ref[...] = lat
    o = ql + kvl
    kr128 = acc[:, o:o + LANES] * cos_ref[...] + acc[:, o + LANES:o + 2 * LANES] * sin_ref[...]
    kr_ref[...] = kr128[:, :ROPE_DIM]
    latk_ref[:, :kvl] = lat.astype(BF16)
    latk_ref[:, kvl:] = kr128.astype(BF16)
    nh = dt_ref.shape[-1]
    dt_ref[...] = _softplus(acc[:, o + 2 * LANES:o + 2 * LANES + nh] + dtb_ref[...])
    dtt = lax.dot_general(wdt_ref[...], h, _NT, preferred_element_type=F32)
    dtt_ref[...] = _softplus(dtt + dtbc_ref[...])


def _lat(h_all, w_lat, w_dt_t, g_cq, g_ckv, cos128, sin128, dt_bias):
    t, d = h_all.shape
    ql, kvl = g_cq.shape[-1], g_ckv.shape[-1]
    nh = dt_bias.shape[-1]
    nl = w_lat.shape[1]
    tm = _pick(t, (512, 256, 128))
    row = lambda w: pl.BlockSpec((tm, w), lambda i: (i, 0))
    const = lambda a, b: pl.BlockSpec((a, b), lambda i: (0, 0))
    return pl.pallas_call(
        functools.partial(_lat_kernel, ql=ql, kvl=kvl),
        out_shape=(jax.ShapeDtypeStruct((t, ql), BF16),
                   jax.ShapeDtypeStruct((t, kvl), F32),
                   jax.ShapeDtypeStruct((t, kvl + LANES), BF16),
                   jax.ShapeDtypeStruct((t, ROPE_DIM), F32),
                   jax.ShapeDtypeStruct((t, nh), F32),
                   jax.ShapeDtypeStruct((nh, t), F32)),
        grid=(t // tm,),
        in_specs=[row(d), const(d, nl), const(nh, d), const(1, ql), const(1, kvl),
                  row(LANES), row(LANES), const(1, nh), const(nh, 1)],
        out_specs=(row(ql), row(kvl), row(kvl + LANES), row(ROPE_DIM), row(nh),
                   pl.BlockSpec((nh, tm), lambda i: (0, i))),
        compiler_params=_cp("arbitrary"),
        name="latent_proj",
    )(h_all, w_lat, w_dt_t, g_cq.reshape(1, ql), g_ckv.reshape(1, kvl), cos128, sin128,
      dt_bias.reshape(1, nh), dt_bias.reshape(nh, 1))


def _q_kernel(c_ref, wa_ref, wb_ref, cos_ref, sin_ref, gn_ref, gr_ref, o_ref, *, heads):
    c = c_ref[...]
    a = _dot(c, wa_ref[...])
    b = _dot(c, wb_ref[...])
    cos, sin = cos_ref[...], sin_ref[...]
    for h in range(heads):
        nope = a[:, h * HEAD_PAD:h * HEAD_PAD + LANES]
        rope = a[:, h * HEAD_PAD + LANES:(h + 1) * HEAD_PAD] * cos + b[:, h * LANES:(h + 1) * LANES] * sin
        ss = (jnp.sum(nope * nope, axis=-1, keepdims=True)
              + jnp.sum(rope * rope, axis=-1, keepdims=True)) * (1.0 / QK_DIM)
        r = lax.rsqrt(ss + EPS)
        o_ref[:, h * HEAD_PAD:h * HEAD_PAD + LANES] = (nope * r * gn_ref[...]).astype(BF16)
        o_ref[:, h * HEAD_PAD + LANES:(h + 1) * HEAD_PAD] = (rope * r * gr_ref[...]).astype(BF16)


def _qproj(cqn, wq_a, wq_b, cos128, sin128, g_nope, g_rope, heads):
    t, ql = cqn.shape
    tm = _pick(t, (256, 128))
    row = lambda w: pl.BlockSpec((tm, w), lambda i: (i, 0))
    const = lambda a, b: pl.BlockSpec((a, b), lambda i: (0, 0))
    return pl.pallas_call(
        functools.partial(_q_kernel, heads=heads),
        out_shape=jax.ShapeDtypeStruct((t, heads * HEAD_PAD), BF16),
        grid=(t // tm,),
        in_specs=[row(ql), const(ql, heads * HEAD_PAD), const(ql, heads * LANES),
                  row(LANES), row(LANES), const(1, LANES), const(1, LANES)],
        out_specs=row(heads * HEAD_PAD),
        compiler_params=_cp("arbitrary"),
        name="q_proj",
    )(cqn, wq_a, wq_b, cos128, sin128, g_nope, g_rope)


def _kv_kernel(lat_ref, kr_ref, w_ref, k_ref, v_ref, *, heads):
    acc = _dot(lat_ref[...], w_ref[...])
    kr = kr_ref[...].astype(F32)
    kr2 = jnp.sum(kr * kr, axis=-1, keepdims=True)
    for h in range(heads):
        kn = acc[:, h * LANES:(h + 1) * LANES]
        ss = (jnp.sum(kn * kn, axis=-1, keepdims=True) + kr2) * (1.0 / QK_DIM)
        r = lax.rsqrt(ss + EPS)
        k_ref[:, h * HEAD_PAD:h * HEAD_PAD + LANES] = (kn * r).astype(BF16)
        k_ref[:, h * HEAD_PAD + LANES:(h + 1) * HEAD_PAD] = (kr * r).astype(BF16)
    v_ref[...] = acc[:, heads * LANES:].astype(BF16)


def _kvproj(latk, w_kv, tp, heads, kvl):
    tm = _pick(tp, (256, 128))
    return pl.pallas_call(
        functools.partial(_kv_kernel, heads=heads),
        out_shape=(jax.ShapeDtypeStruct((tp, heads * HEAD_PAD), BF16),
                   jax.ShapeDtypeStruct((tp, heads * V_DIM), BF16)),
        grid=(tp // tm,),
        in_specs=[pl.BlockSpec((tm, kvl), lambda i: (i, 0)),
                  pl.BlockSpec((tm, LANES), lambda i: (i, kvl // LANES)),
                  pl.BlockSpec((kvl, 2 * heads * LANES), lambda i: (0, 0))],
        out_specs=(pl.BlockSpec((tm, heads * HEAD_PAD), lambda i: (i, 0)),
                   pl.BlockSpec((tm, heads * V_DIM), lambda i: (i, 0))),
        compiler_params=_cp("arbitrary"),
        name="kv_proj",
    )(latk, latk, w_kv)


def _attn_p_kernel(q_ref, k_ref, v_ref, o_ref, m_ref, l_ref, acc_ref, *, tq, hg):
    qi = pl.program_id(2)
    m_ref[...] = jnp.full(m_ref.shape, -jnp.inf, F32)
    l_ref[...] = jnp.zeros(l_ref.shape, F32)
    acc_ref[...] = jnp.zeros(acc_ref.shape, F32)

    def tile(j, masked):
        ks = pl.multiple_of(j * tq, tq)
        for g in range(hg):
            q = q_ref[:, g * HEAD_PAD:(g + 1) * HEAD_PAD]
            k = k_ref[pl.ds(ks, tq), g * HEAD_PAD:(g + 1) * HEAD_PAD]
            v = v_ref[pl.ds(ks, tq), g * V_DIM:(g + 1) * V_DIM]
            s = lax.dot_general(q, k, _NT, preferred_element_type=F32)
            if masked:
                rc = lax.broadcasted_iota(I32, (tq, tq), 0) // CHUNK
                cc = lax.broadcasted_iota(I32, (tq, tq), 1) // CHUNK
                s = jnp.where(cc <= rc, s, -jnp.inf)
            m_prev = m_ref[g]
            m_new = jnp.maximum(m_prev, jnp.max(s, axis=-1, keepdims=True))
            alpha = jnp.exp2(m_prev - m_new)
            p = jnp.exp2(s - jnp.tile(m_new, (1, tq // LANES)))
            l_ref[g] = alpha * l_ref[g] + jnp.sum(p, axis=-1, keepdims=True)
            acc_ref[g] = alpha * acc_ref[g] + _dot(p.astype(BF16), v)
            m_ref[g] = m_new

    def body(j, carry):
        tile(j, False)
        return carry

    lax.fori_loop(0, qi, body, 0)
    tile(qi, True)
    for g in range(hg):
        o_ref[:, g * V_DIM:(g + 1) * V_DIM] = (acc_ref[g] / l_ref[g]).astype(o_ref.dtype)


def _attn_prompt(q_all, k_p, v_p, batch, seq, heads):
    tq = _pick(seq, (512, 256, 128, 64))
    nq = seq // tq
    hg = _pick(heads, (4, 2, 1))
    return pl.pallas_call(
        functools.partial(_attn_p_kernel, tq=tq, hg=hg),
        out_shape=jax.ShapeDtypeStruct((batch * seq, heads * V_DIM), BF16),
        grid=(batch, heads // hg, nq),
        in_specs=[pl.BlockSpec((tq, hg * HEAD_PAD), lambda b, h, qi: (b * nq + qi, h)),
                  pl.BlockSpec((seq, hg * HEAD_PAD), lambda b, h, qi: (b, h)),
                  pl.BlockSpec((seq, hg * V_DIM), lambda b, h, qi: (b, h))],
        out_specs=pl.BlockSpec((tq, hg * V_DIM), lambda b, h, qi: (b * nq + qi, h)),
        scratch_shapes=[pltpu.VMEM((hg, tq, LANES), F32), pltpu.VMEM((hg, tq, LANES), F32),
                        pltpu.VMEM((hg, tq, V_DIM), F32)],
        compiler_params=_cp("arbitrary", "arbitrary", "arbitrary"),
        name="attn_prompt",
    )(q_all, k_p, v_p)


def _attn_s_kernel(pl_ref, pk_ref, nl_ref, q_ref, wukt_ref, wuv_ref, o_ref,
                   qabs_ref, qr_ref, m_ref, l_ref, acc_ref, *, heads, kvl, nkb):
    kb = pl.program_id(1)

    @pl.when(kb == 0)
    def _():
        for h in range(heads):
            qn = q_ref[:, h * HEAD_PAD:h * HEAD_PAD + LANES]
            qabs_ref[h * CHUNK:(h + 1) * CHUNK, :] = _dot(
                qn, wukt_ref[h * LANES:(h + 1) * LANES, :]).astype(BF16)
            qr_ref[h * CHUNK:(h + 1) * CHUNK, :] = q_ref[:, h * HEAD_PAD + LANES:(h + 1) * HEAD_PAD]
        m_ref[...] = jnp.full(m_ref.shape, -jnp.inf, F32)
        l_ref[...] = jnp.zeros(l_ref.shape, F32)
        acc_ref[...] = jnp.zeros(acc_ref.shape, F32)

    def block(xl, krf):
        rtop = lax.dot_general(wukt_ref[...], xl, _NT, preferred_element_type=F32)
        sq = krf * krf
        sq_hi = sq.astype(BF16)
        sq_lo = (sq - sq_hi.astype(F32)).astype(BF16)
        ones = jnp.ones((SUBLANES, ROPE_DIM), BF16)
        kr2 = (lax.dot_general(ones, sq_hi, _NT, preferred_element_type=F32)
               + lax.dot_general(ones, sq_lo, _NT, preferred_element_type=F32))[0:1, :]
        s = (lax.dot_general(qabs_ref[...], xl, _NT, preferred_element_type=F32)
             + lax.dot_general(qr_ref[:, :ROPE_DIM], krf.astype(BF16), _NT,
                               preferred_element_type=F32))
        parts = []
        for h in range(heads):
            rt = rtop[h * LANES:(h + 1) * LANES, :]
            kn2 = jnp.sum(rt * rt, axis=0, keepdims=True)
            r = lax.rsqrt((kn2 + kr2) * (1.0 / QK_DIM) + EPS)
            parts.append(s[h * CHUNK:(h + 1) * CHUNK, :] * r)
        s = jnp.concatenate(parts, axis=0)
        n = s.shape[1]
        m_prev = m_ref[...]
        m_new = jnp.maximum(m_prev, jnp.max(s, axis=-1, keepdims=True))
        alpha = jnp.exp2(m_prev - m_new)
        m_wide = jnp.tile(m_new, (1, n // LANES)) if n >= LANES else m_new[:, :n]
        p = jnp.exp2(s - m_wide)
        l_ref[...] = alpha * l_ref[...] + jnp.sum(p, axis=-1, keepdims=True)
        acc_ref[...] = jnp.tile(alpha, (1, kvl // LANES)) * acc_ref[...] + _dot(p.astype(BF16), xl)
        m_ref[...] = m_new

    @pl.when(kb < nkb)
    def _():
        block(pl_ref[0].astype(BF16), pk_ref[0])

    @pl.when(kb == nkb)
    def _():
        block(nl_ref[:, :kvl], nl_ref[:, kvl:kvl + ROPE_DIM].astype(F32))
        o = (acc_ref[...] / jnp.tile(l_ref[...], (1, kvl // LANES))).astype(BF16)
        for h in range(heads):
            o_ref[:, h * V_DIM:(h + 1) * V_DIM] = _dot(
                o[h * CHUNK:(h + 1) * CHUNK, :], wuv_ref[:, h * V_DIM:(h + 1) * V_DIM]).astype(o_ref.dtype)


def _attn_sample(past_lat, past_kr, latk, q_all, w_uk_t, w_uv2, tp, heads):
    bs, past, kvl = past_lat.shape
    tk = _pick(past, (512, 256, 128))
    nkb = past // tk
    c0 = tp // CHUNK
    pidx = lambda b, kb: (b, jnp.minimum(kb, nkb - 1), 0)
    hq = heads * CHUNK
    return pl.pallas_call(
        functools.partial(_attn_s_kernel, heads=heads, kvl=kvl, nkb=nkb),
        out_shape=jax.ShapeDtypeStruct((bs * CHUNK, heads * V_DIM), BF16),
        grid=(bs, nkb + 1),
        in_specs=[pl.BlockSpec((1, tk, kvl), pidx),
                  pl.BlockSpec((1, tk, ROPE_DIM), pidx),
                  pl.BlockSpec((CHUNK, kvl + LANES), lambda b, kb: (c0 + b, 0)),
                  pl.BlockSpec((CHUNK, heads * HEAD_PAD), lambda b, kb: (c0 + b, 0)),
                  pl.BlockSpec((heads * LANES, kvl), lambda b, kb: (0, 0)),
                  pl.BlockSpec((kvl, heads * V_DIM), lambda b, kb: (0, 0))],
        out_specs=pl.BlockSpec((CHUNK, heads * V_DIM), lambda b, kb: (b, 0)),
        scratch_shapes=[pltpu.VMEM((hq, kvl), BF16), pltpu.VMEM((hq, LANES), BF16),
                        pltpu.VMEM((hq, LANES), F32), pltpu.VMEM((hq, LANES), F32),
                        pltpu.VMEM((hq, kvl), F32)],
        compiler_params=_cp("arbitrary", "arbitrary"),
        name="attn_sample",
    )(past_lat, past_kr, latk, q_all, w_uk_t, w_uv2)


def _ssd_kernel(z_ref, x_ref, dt_ref, dtt_ref, conv0_ref, ssm0_ref, cw_ref, cb_ref, alr_ref, alc_ref,
                dsk_ref, gs_ref, eh_ref, o_ref, st_out_ref, xs_ref, st_ref,
                *, lc, lr, nh, groups):
    c = pl.program_id(1)
    p = M_HEADDIM
    n = D_STATE
    ci = nh * p
    k8 = nh // groups
    gw = k8 * p

    @pl.when(c == 0)
    def _():
        xs_ref[0:SUBLANES, :] = conv0_ref[0]
        st_ref[...] = ssm0_ref[0]

    xs_ref[SUBLANES:SUBLANES + lr, :] = x_ref[...].astype(F32)
    if lc > lr:
        xs_ref[SUBLANES + lr:SUBLANES + lc, :] = jnp.zeros((lc - lr, xs_ref.shape[1]), F32)

    def conv(lo, hi):
        u = xs_ref[SUBLANES - 3:SUBLANES - 3 + lc, lo:hi] * cw_ref[0:1, lo:hi]
        for tap in range(1, CONV_W):
            u = u + xs_ref[SUBLANES - 3 + tap:SUBLANES - 3 + tap + lc, lo:hi] * cw_ref[tap:tap + 1, lo:hi]
        return _silu(u + cb_ref[:, lo:hi])

    dt = dt_ref[0]
    dtt = dtt_ref[0]
    a_row = -jnp.exp(alr_ref[...])
    a_col = -jnp.exp(alc_ref[...])
    ri = lax.broadcasted_iota(I32, (lc, lc), 0)
    cidx = lax.broadcasted_iota(I32, (lc, lc), 1)
    tri = ri >= cidx
    tril = jnp.where(tri, 1.0, 0.0).astype(BF16)
    triu = jnp.where(ri <= cidx, 1.0, 0.0).astype(BF16)
    cs = sum(_dot(tril, piece) for piece in _split3(dt * a_row))
    cst = sum(_dot(piece, triu) for piece in _split3(dtt * a_col))
    exp_cs = jnp.exp(cs)
    w_end = jnp.exp(cs[lc - 1:lc, :] - cs)
    stacked = jnp.concatenate([dt, exp_cs, w_end], axis=0)
    eh = eh_ref[...]
    expanded = sum(_dot(piece, eh) for piece in _split3(stacked))
    dt_e, ecs_e, wend_e = expanded[:lc], expanded[lc:2 * lc], expanded[2 * lc:]
    cdec = jnp.exp(cst[:, lc - 1:lc])
    lane_lo = lax.broadcasted_iota(I32, (lc, LANES), 1) < p

    for g in range(groups):
        gs = slice(g * gw, (g + 1) * gw)
        xg = conv(g * gw, (g + 1) * gw)
        bg = conv(ci + g * n, ci + (g + 1) * n).astype(BF16)
        cg = conv(ci + groups * n + g * n, ci + groups * n + (g + 1) * n).astype(BF16)
        cbm = lax.dot_general(cg, bg, _NT, preferred_element_type=F32)
        xdt = xg * dt_e[:, gs]
        xdt_b = xdt.astype(BF16)
        pairs = []
        for q in range(k8 // 2):
            x2 = xdt_b[:, q * LANES:(q + 1) * LANES]
            ys = []
            for s in range(2):
                h = g * k8 + 2 * q + s
                seg = cs[:, h:h + 1] - cst[h:h + 1, :]
                dec = jnp.exp(jnp.where(tri, seg, -jnp.inf))
                ys.append(_dot((cbm * dec).astype(BF16), x2))
            pairs.append(jnp.where(lane_lo, ys[0], ys[1]))
        y_diag = jnp.concatenate(pairs, axis=1)
        sg = st_ref[g * gw:(g + 1) * gw, :]
        y_off = lax.dot_general(cg, sg.astype(BF16), _NT, preferred_element_type=F32) * ecs_e[:, gs]
        y = y_diag + y_off + xg * dsk_ref[:, gs]
        xw = (xdt * wend_e[:, gs]).astype(BF16)
        upd = lax.dot_general(xw, bg, _TN, preferred_element_type=F32)
        for k in range(k8):
            h = g * k8 + k
            rows = slice(g * gw + k * p, g * gw + (k + 1) * p)
            st_ref[rows, :] = st_ref[rows, :] * cdec[h:h + 1, :] + upd[k * p:(k + 1) * p, :]
        zg = z_ref[:, gs].astype(F32)
        u2 = y[:lr] * _silu(zg)
        ms = jnp.mean(u2 * u2, axis=-1, keepdims=True)
        o_ref[:, gs] = (u2 * lax.rsqrt(ms + EPS) * gs_ref[:, gs]).astype(o_ref.dtype)

    xs_ref[0:SUBLANES, :] = xs_ref[lr:lr + SUBLANES, :]

    @pl.when(c == pl.num_programs(1) - 1)
    def _():
        st_out_ref[0] = st_ref[...]


def _ssd(z_all, xbc_all, dt3, dtt3, conv0p, ssm0, conv_w, conv_b, a_log, dsk, g_ssm, eh,
         row0, nseq, nchunk, lc, lr, groups):
    ci = z_all.shape[1]
    cc = xbc_all.shape[1]
    nh = ci // M_HEADDIM
    rb0 = row0 // lr
    rowblk = lambda w: pl.BlockSpec((lr, w), lambda b, c: (rb0 + b * nchunk + c, 0))
    seq3 = lambda a, b_: pl.BlockSpec((1, a, b_), lambda b, c: (b * nchunk + c, 0, 0))
    perb = lambda a, b_: pl.BlockSpec((1, a, b_), lambda b, c: (b, 0, 0))
    const = lambda a, b_: pl.BlockSpec((a, b_), lambda b, c: (0, 0))
    return pl.pallas_call(
        functools.partial(_ssd_kernel, lc=lc, lr=lr, nh=nh, groups=groups),
        out_shape=(jax.ShapeDtypeStruct((nseq * nchunk * lr, ci), BF16),
                   jax.ShapeDtypeStruct((nseq, ci, D_STATE), F32)),
        grid=(nseq, nchunk),
        in_specs=[rowblk(ci), rowblk(cc), seq3(lc, nh), seq3(nh, lc),
                  perb(SUBLANES, cc), perb(ci, D_STATE),
                  const(CONV_W, cc), const(1, cc), const(1, nh), const(nh, 1),
                  const(1, ci), const(1, ci), const(nh, ci)],
        out_specs=(pl.BlockSpec((lr, ci), lambda b, c: (b * nchunk + c, 0)),
                   perb(ci, D_STATE)),
        scratch_shapes=[pltpu.VMEM((lc + SUBLANES, cc), F32), pltpu.VMEM((ci, D_STATE), F32)],
        compiler_params=_cp("arbitrary", "arbitrary"),
        name="ssd",
    )(z_all, xbc_all, dt3, dtt3, conv0p, ssm0, conv_w, conv_b.reshape(1, cc),
      a_log.reshape(1, nh), a_log.reshape(nh, 1), dsk, g_ssm.reshape(1, ci), eh)


def _mix_kernel(ap_ref, as_ref, mp_ref, ms_ref, ga_ref, gb_ref, wpa_ref, wpb_ref, o_ref, *, npb):
    i = pl.program_id(0)
    a = jnp.where(i < npb, ap_ref[...], as_ref[...])
    m = jnp.where(i < npb, mp_ref[...], ms_ref[...])
    pa = _dot(a, wpa_ref[...])
    pb = _dot(m, wpb_ref[...])
    o_ref[...] = (ga_ref[...].astype(F32) * pa + gb_ref[...].astype(F32) * pb).astype(o_ref.dtype)


def _mix(a_p, a_s, m_p, m_s, gates, w_pa, w_pb):
    tp, hv = a_p.shape
    ts = a_s.shape[0]
    ci = m_p.shape[1]
    d = w_pa.shape[1]
    tm = _pick(math.gcd(tp, ts), (512, 256, 128, 64))
    tn = _pick(d, (512, 256, 128))
    npb = tp // tm
    nj = d // tn
    pidx = lambda i, j: (jnp.minimum(i, npb - 1), 0)
    sidx = lambda i, j: (jnp.maximum(i - npb, 0), 0)
    return pl.pallas_call(
        functools.partial(_mix_kernel, npb=npb),
        out_shape=jax.ShapeDtypeStruct((tp + ts, d), BF16),
        grid=((tp + ts) // tm, nj),
        in_specs=[pl.BlockSpec((tm, hv), pidx), pl.BlockSpec((tm, hv), sidx),
                  pl.BlockSpec((tm, ci), pidx), pl.BlockSpec((tm, ci), sidx),
                  pl.BlockSpec((tm, tn), lambda i, j: (i, j)),
                  pl.BlockSpec((tm, tn), lambda i, j: (i, nj + j)),
                  pl.BlockSpec((hv, tn), lambda i, j: (0, j)),
                  pl.BlockSpec((ci, tn), lambda i, j: (0, j))],
        out_specs=pl.BlockSpec((tm, tn), lambda i, j: (i, j)),
        compiler_params=_cp("arbitrary", "arbitrary"),
        name="branch_mix",
    )(a_p, a_s, m_p, m_s, gates, gates, w_pa, w_pb)


def _pack_bf16_pairs(x):
    n = x.shape[1] // 2
    lo = pltpu.bitcast(x[:, :n].astype(BF16).astype(F32), jnp.uint32)
    hi = pltpu.bitcast(x[:, n:].astype(BF16).astype(F32), jnp.uint32)
    return hi | (lo >> 16)


def _unpack_bf16_pairs(u):
    lo = pltpu.bitcast(u << 16, F32)
    hi = pltpu.bitcast(u & jnp.uint32(0xFFFF0000), F32)
    return jnp.concatenate([lo, hi], axis=1)


def _post_kernel(mx_ref, wout_ref, x_ref, gt1_ref, sc2_ref, sh2_ref, g2_ref, wrh_ref, wrl_ref, br_ref,
                 x1_ref, h2p_ref, ti_ref, tp_ref, h2_ref, *, nexp):
    o = _dot(mx_ref[...], wout_ref[...])
    tm = o.shape[0]
    for ch in range(tm // CHUNK):
        rows = slice(ch * CHUNK, (ch + 1) * CHUNK)
        x1 = x_ref[rows, :] + gt1_ref[ch] * o[rows, :]
        x1_ref[rows, :] = x1
        xn = x1 * lax.rsqrt(jnp.mean(x1 * x1, axis=-1, keepdims=True) + EPS)
        h2_ref[rows, :] = xn * g2_ref[...] * (1.0 + sc2_ref[ch]) + sh2_ref[ch]
    h2 = h2_ref[...]
    h2p_ref[...] = _pack_bf16_pairs(h2)
    hh = h2.astype(BF16)
    hl = (h2 - hh.astype(F32)).astype(BF16)
    logits = _dot(hh, wrh_ref[...]) + _dot(hh, wrl_ref[...]) + _dot(hl, wrh_ref[...]) + br_ref[...]
    lane = lax.broadcasted_iota(I32, logits.shape, 1)
    logits = jnp.where(lane < nexp, logits, NEG_BIG)
    vals, idxs = [], []
    for _ in range(TOP_K):
        m = jnp.max(logits, axis=-1, keepdims=True)
        idx = jnp.min(jnp.where(logits == m, lane, LANES), axis=-1, keepdims=True)
        vals.append(m)
        idxs.append(idx)
        logits = jnp.where(lane == idx, 2.0 * NEG_BIG, logits)
    es = [jnp.exp(v - vals[0]) for v in vals]
    den = es[0]
    for e in es[1:]:
        den = den + e
    ti = jnp.zeros(lane.shape, I32)
    tpv = jnp.zeros(lane.shape, F32)
    for k in range(TOP_K):
        ti = jnp.where(lane == k, idxs[k], ti)
        tpv = jnp.where(lane == k, es[k] / den, tpv)
    ti_ref[...] = ti
    tp_ref[...] = tpv


def _post(mixed, w_out, x_all, gt1, sc2, sh2, g2, wr_hi, wr_lo, br, nexp):
    t, d = mixed.shape
    tm = _pick(t, (256, 128, 64))
    nc = tm // CHUNK
    row = lambda w: pl.BlockSpec((tm, w), lambda i: (i, 0))
    mod = pl.BlockSpec((nc, 1, d), lambda i: (i, 0, 0))
    const = lambda a, b: pl.BlockSpec((a, b), lambda i: (0, 0))
    return pl.pallas_call(
        functools.partial(_post_kernel, nexp=nexp),
        out_shape=(jax.ShapeDtypeStruct((t, d), F32), jax.ShapeDtypeStruct((t, d // 2), jnp.uint32),
                   jax.ShapeDtypeStruct((t, LANES), I32), jax.ShapeDtypeStruct((t, LANES), F32)),
        grid=(t // tm,),
        in_specs=[row(d), const(d, d), row(d), mod, mod, mod, const(1, d),
                  const(d, LANES), const(d, LANES), const(1, LANES)],
        out_specs=(row(d), row(d // 2), row(LANES), row(LANES)),
        scratch_shapes=[pltpu.VMEM((tm, d), F32)],
        compiler_params=_cp("arbitrary"),
        name="post_mix",
    )(mixed, w_out, x_all, gt1, sc2, sh2, g2.reshape(1, d), wr_hi, wr_lo, br)


def _slots_kernel(ti_ref, dest_ref, meta_ref, run_ref, *, bm, tb):
    ph = pl.program_id(0)
    i = pl.program_id(1)
    eid = lax.broadcasted_iota(I32, (LANES, tb), 0)
    onehots = [jnp.where(eid == ti_ref[k:k + 1, :], 1.0, 0.0) for k in range(TOP_K)]
    osum = onehots[0]
    for oh in onehots[1:]:
        osum = osum + oh
    blk_cnt = jnp.sum(osum, axis=1, keepdims=True)

    @pl.when(jnp.logical_and(ph == 0, i == 0))
    def _():
        run_ref[...] = jnp.zeros(run_ref.shape, F32)

    @pl.when(ph == 0)
    def _():
        run_ref[...] = run_ref[...] + blk_cnt

    @pl.when(jnp.logical_and(ph == 1, i == 0))
    def _():
        cnt = run_ref[...]
        nblk = jnp.ceil(cnt * (1.0 / bm))
        r = lax.broadcasted_iota(I32, (LANES, LANES), 0)
        c = lax.broadcasted_iota(I32, (LANES, LANES), 1)
        lstrict = jnp.where(c < r, 1.0, 0.0).astype(BF16)
        start_blk = _dot(lstrict, jnp.broadcast_to(nblk, (LANES, LANES)).astype(BF16))
        lane = lax.broadcasted_iota(I32, (LANES, LANES), 1)
        meta_ref[...] = jnp.where(lane == 0, cnt, jnp.where(lane == 1, start_blk, 0.0))
        run_ref[...] = start_blk[:, 0:1] * float(bm)

    @pl.when(ph == 1)
    def _():
        r = lax.broadcasted_iota(I32, (tb, tb), 0)
        c = lax.broadcasted_iota(I32, (tb, tb), 1)
        ustrict = jnp.where(r < c, 1.0, 0.0).astype(BF16)
        base = run_ref[...] + _dot(osum.astype(BF16), ustrict)
        for k in range(TOP_K):
            dest_ref[0, k:k + 1, :] = jnp.sum(onehots[k] * base, axis=0, keepdims=True).astype(I32)
        dest_ref[0, TOP_K:, :] = jnp.zeros((SUBLANES - TOP_K, tb), I32)
        run_ref[...] = run_ref[...] + blk_cnt


def _moe_slots(ti_t, bm, tb):
    t = ti_t.shape[1]
    nt = t // tb
    return pl.pallas_call(
        functools.partial(_slots_kernel, bm=bm, tb=tb),
        out_shape=(jax.ShapeDtypeStruct((nt, SUBLANES, tb), I32),
                   jax.ShapeDtypeStruct((LANES, LANES), F32)),
        grid=(2, nt),
        in_specs=[pl.BlockSpec((SUBLANES, tb), lambda ph, i: (0, i))],
        out_specs=(pl.BlockSpec((1, SUBLANES, tb), lambda ph, i: (i * ph, 0, 0)),
                   pl.BlockSpec((LANES, LANES), lambda ph, i: (0, 0))),
        scratch_shapes=[pltpu.VMEM((LANES, 1), F32)],
        compiler_params=_cp("arbitrary", "arbitrary"),
        name="moe_slots",
    )(ti_t)


def _dispatch_kernel(zs_ref, idx_ref, h_ref, xg_ref, zbuf_ref, idx_smem, isem, sem, *, nz, bm, tb):
    i = pl.program_id(0)

    def zero_fill(e):
        return pltpu.make_async_copy(zbuf_ref, xg_ref.at[pl.ds(pl.multiple_of(zs_ref[e], bm), bm)], sem)

    @pl.when(i == 0)
    def _():
        zbuf_ref[...] = jnp.zeros(zbuf_ref.shape, zbuf_ref.dtype)
        for e in range(nz):
            pl.when(zs_ref[e] >= 0)(lambda e=e: zero_fill(e).start())
        for e in range(nz):
            pl.when(zs_ref[e] >= 0)(lambda e=e: zero_fill(e).wait())

    icp = pltpu.make_async_copy(idx_ref.at[0], idx_smem, isem)
    icp.start()
    icp.wait()

    def start(r, carry):
        for k in range(TOP_K):
            pltpu.make_async_copy(h_ref.at[pl.ds(r, 1)], xg_ref.at[pl.ds(idx_smem[k, r], 1)],
                                  sem).start(priority=k % 2)
        return carry

    lax.fori_loop(0, tb, start, 0)
    for k in range(TOP_K):
        pltpu.make_async_copy(h_ref, xg_ref.at[pl.ds(0, tb)], sem).wait()


def _moe_dispatch(zstart, dest3, h2, n_slots, bm):
    nt, _, tb = dest3.shape
    d = h2.shape[1]
    return pl.pallas_call(
        functools.partial(_dispatch_kernel, nz=zstart.shape[0], bm=bm, tb=tb),
        out_shape=jax.ShapeDtypeStruct((n_slots, d), h2.dtype),
        grid_spec=pltpu.PrefetchScalarGridSpec(
            num_scalar_prefetch=1,
            grid=(nt,),
            in_specs=[pl.BlockSpec((1, SUBLANES, tb), lambda i, zs: (i, 0, 0)),
                      pl.BlockSpec((tb, d), lambda i, zs: (i, 0))],
            out_specs=pl.BlockSpec(memory_space=pl.ANY),
            scratch_shapes=[pltpu.VMEM((bm, d), h2.dtype), pltpu.SMEM((SUBLANES, tb), I32),
                            pltpu.SemaphoreType.DMA, pltpu.SemaphoreType.DMA]),
        compiler_params=_cp("arbitrary"),
        name="moe_dispatch",
    )(zstart, dest3, h2)


def _weight_stream(w_refs, wbuf_ref, wb_refs, sem, cnt_ref, be_ref, nx_ref, used, tw):
    c = pl.program_id(0)
    b = pl.program_id(1)
    nc = pl.num_programs(0)

    def fetch(e, cc, slot):
        col = pl.ds(pl.multiple_of(cc * tw, tw), tw)
        return [pltpu.make_async_copy(w.at[e, :, col], wbuf_ref.at[slot, m], sem.at[slot])
                for m, w in enumerate(w_refs)]

    @pl.when(jnp.logical_and(c == 0, b == 0))
    def _():
        cnt_ref[0] = 0
        for cp in fetch(be_ref[0], 0, 0):
            cp.start(priority=WEIGHT_DMA_PRIORITY)

    changed = jnp.logical_or(b == 0, be_ref[b] != be_ref[jnp.maximum(b - 1, 0)])

    @pl.when(jnp.logical_and(used, changed))
    def _():
        slot = cnt_ref[0] & 1
        for cp in fetch(be_ref[b], c, slot):
            cp.wait()
        for m, wb in enumerate(wb_refs):
            wb[...] = wbuf_ref[slot, m].astype(BF16)
        nxt = nx_ref[b]
        same_chunk = nxt >= 0
        nxt_e = jnp.where(same_chunk, nxt, be_ref[0])
        nxt_c = jnp.where(same_chunk, c, c + 1)

        @pl.when(jnp.logical_or(same_chunk, c + 1 < nc))
        def _():
            for cp in fetch(nxt_e, nxt_c, 1 - slot):
                cp.start(priority=WEIGHT_DMA_PRIORITY)

        cnt_ref[0] = cnt_ref[0] + 1


def _gateup_kernel(be_ref, nx_ref, nu_ref, x_ref, bg_ref, bu_ref, wg_ref, wu_ref, o_ref,
                   wbuf_ref, wgb_ref, wub_ref, sem, cnt_ref, *, tf):
    used = pl.program_id(1) < nu_ref[0]
    _weight_stream([wg_ref, wu_ref], wbuf_ref, [wgb_ref, wub_ref], sem, cnt_ref, be_ref, nx_ref, used, tf)

    @pl.when(used)
    def _():
        x = _unpack_bf16_pairs(x_ref[...]).astype(BF16)
        gate = jnp.minimum(_dot(x, wgb_ref[...]) + bg_ref[0], SWIGLU_LIMIT)
        up = jnp.clip(_dot(x, wub_ref[...]) + bu_ref[0], -SWIGLU_LIMIT, SWIGLU_LIMIT)
        glu = gate * jax.nn.sigmoid(SWIGLU_ALPHA * gate)
        o_ref[...] = ((up + 1.0) * glu).astype(o_ref.dtype)

    @pl.when(jnp.logical_not(used))
    def _():
        o_ref[...] = jnp.zeros(o_ref.shape, o_ref.dtype)


def _moe_gateup(block_e, next_e, n_used, xg, w_gate, w_up, b_gate, b_up):
    ns, dh = xg.shape
    nexp, d, ff = w_gate.shape
    assert d == 2 * dh
    bm = MOE_BLOCK
    nb = ns // bm
    tf = _pick(ff, (1024, 512, 256, 128))
    bspec = pl.BlockSpec((1, 1, tf), lambda c, b, be, nx, nu: (be[b], 0, c))
    hbm = pl.BlockSpec(memory_space=pl.ANY)
    return pl.pallas_call(
        functools.partial(_gateup_kernel, tf=tf),
        out_shape=jax.ShapeDtypeStruct((ns, ff), BF16),
        grid_spec=pltpu.PrefetchScalarGridSpec(
            num_scalar_prefetch=3,
            grid=(ff // tf, nb),
            in_specs=[pl.BlockSpec((bm, dh), lambda c, b, be, nx, nu: (jnp.minimum(b, nu[0] - 1), 0)),
                      bspec, bspec, hbm, hbm],
            out_specs=pl.BlockSpec((bm, tf), lambda c, b, be, nx, nu: (b, c)),
            scratch_shapes=[pltpu.VMEM((2, 2, d, tf), F32), pltpu.VMEM((d, tf), BF16),
                            pltpu.VMEM((d, tf), BF16), pltpu.SemaphoreType.DMA((2,)),
                            pltpu.SMEM((1,), I32)]),
        compiler_params=_cp("arbitrary", "arbitrary"),
        name="moe_gateup",
    )(block_e, next_e, n_used, xg, b_gate.reshape(nexp, 1, ff), b_up.reshape(nexp, 1, ff), w_gate, w_up)


def _down_kernel(be_ref, nx_ref, nu_ref, g_ref, bd_ref, wd_ref, o_ref, wbuf_ref, wdb_ref, sem, cnt_ref, *, tn):
    used = pl.program_id(1) < nu_ref[0]
    _weight_stream([wd_ref], wbuf_ref, [wdb_ref], sem, cnt_ref, be_ref, nx_ref, used, tn)

    @pl.when(used)
    def _():
        o_ref[...] = _pack_bf16_pairs(_dot(g_ref[...], wdb_ref[...]) + bd_ref[0])

    @pl.when(jnp.logical_not(used))
    def _():
        o_ref[...] = jnp.zeros(o_ref.shape, o_ref.dtype)


def _moe_down(block_e, next_e, n_used, glu, w_down, b_down):
    ns, ff = glu.shape
    nexp, _, d = w_down.shape
    bm = MOE_BLOCK
    nb = ns // bm
    tn = d
    return pl.pallas_call(
        functools.partial(_down_kernel, tn=tn),
        out_shape=jax.ShapeDtypeStruct((ns, d // 2), jnp.uint32),
        grid_spec=pltpu.PrefetchScalarGridSpec(
            num_scalar_prefetch=3,
            grid=(d // tn, nb),
            in_specs=[pl.BlockSpec((bm, ff), lambda c, b, be, nx, nu: (jnp.minimum(b, nu[0] - 1), 0)),
                      pl.BlockSpec((1, 1, tn), lambda c, b, be, nx, nu: (be[b], 0, c)),
                      pl.BlockSpec(memory_space=pl.ANY)],
            out_specs=pl.BlockSpec((bm, tn // 2), lambda c, b, be, nx, nu: (b, c)),
            scratch_shapes=[pltpu.VMEM((2, 1, ff, tn), F32), pltpu.VMEM((ff, tn), BF16),
                            pltpu.SemaphoreType.DMA((2,)), pltpu.SMEM((1,), I32)]),
        compiler_params=_cp("arbitrary", "arbitrary"),
        name="moe_down",
    )(block_e, next_e, n_used, glu, b_down.reshape(nexp, 1, d), w_down)


def _combine_kernel(idx_ref, yb_ref, x1_ref, gt2_ref, p_ref, o_ref, buf_ref, idx_smem, isem, sem, *, tc):
    icp = pltpu.make_async_copy(idx_ref.at[0], idx_smem, isem)
    icp.start()
    icp.wait()

    def start(r, carry):
        for k in range(TOP_K):
            pltpu.make_async_copy(yb_ref.at[pl.ds(idx_smem[k, r], 1)],
                                  buf_ref.at[pl.ds(k * tc + r, 1)], sem).start(priority=k % 2)
        return carry

    lax.fori_loop(0, tc, start, 0)
    pltpu.make_async_copy(yb_ref.at[pl.ds(0, TOP_K * tc)], buf_ref, sem).wait()
    ff = p_ref[:, 0:1] * _unpack_bf16_pairs(buf_ref[0:tc, :])
    for k in range(1, TOP_K):
        ff = ff + p_ref[:, k:k + 1] * _unpack_bf16_pairs(buf_ref[k * tc:(k + 1) * tc, :])
    for ch in range(tc // CHUNK):
        rows = slice(ch * CHUNK, (ch + 1) * CHUNK)
        o_ref[rows, :] = x1_ref[rows, :] + gt2_ref[ch] * ff[rows, :]


def _moe_combine(dest3, yb, x1, gt2, top_p, row0, nrows):
    d = 2 * yb.shape[1]
    tc = dest3.shape[2]
    nc = tc // CHUNK
    rb0 = row0 // tc
    return pl.pallas_call(
        functools.partial(_combine_kernel, tc=tc),
        out_shape=jax.ShapeDtypeStruct((nrows, d), F32),
        grid=(nrows // tc,),
        in_specs=[pl.BlockSpec((1, SUBLANES, tc), lambda i: (rb0 + i, 0, 0)),
                  pl.BlockSpec(memory_space=pl.ANY),
                  pl.BlockSpec((tc, d), lambda i: (rb0 + i, 0)),
                  pl.BlockSpec((nc, 1, d), lambda i: (rb0 + i, 0, 0)),
                  pl.BlockSpec((tc, LANES), lambda i: (rb0 + i, 0))],
        out_specs=pl.BlockSpec((tc, d), lambda i: (i, 0)),
        scratch_shapes=[pltpu.VMEM((TOP_K * tc, d // 2), jnp.uint32), pltpu.SMEM((SUBLANES, tc), I32),
                        pltpu.SemaphoreType.DMA, pltpu.SemaphoreType.DMA],
        compiler_params=_cp("arbitrary"),
        name="moe_combine",
    )(dest3, yb, x1, gt2, top_p)


def _block_layout(meta, nexp, bm, n_blocks):
    counts = meta[:nexp, 0]
    start_blk = meta[:nexp, 1].astype(I32)
    nblk = jnp.ceil(counts * (1.0 / bm)).astype(I32)
    end_blk = start_blk + nblk
    n_used = jnp.sum(nblk)
    blk = jnp.minimum(jnp.arange(n_blocks), n_used - 1)
    block_e = jnp.minimum(jnp.sum(end_blk[None, :] <= blk[:, None], axis=1), nexp - 1).astype(I32)
    onehot_e = block_e[:, None] == jnp.arange(nexp)[None, :]
    run_end = jnp.sum(jnp.where(onehot_e, end_blk[None, :], 0), axis=1)
    at_end = run_end[:, None] == jnp.arange(n_blocks)[None, :]
    next_e = jnp.where(run_end < n_used, jnp.sum(jnp.where(at_end, block_e[None, :], 0), axis=1), -1).astype(I32)
    last = jnp.where(nblk > 0, (end_blk - 1) * bm, -1)
    spare = n_used + jnp.arange(nexp)
    spare = jnp.where(spare < n_blocks, spare * bm, -1)
    zstart = jnp.concatenate([last, spare]).astype(I32)
    return block_e, next_e, n_used.reshape(1).astype(I32), zstart


def _rope_tables(pos):
    half = ROPE_DIM // 2
    inv = ROPE_THETA ** (-jnp.arange(half, dtype=F32) / half)
    ang = pos.astype(F32)[:, None] * inv[None, :]
    z = jnp.zeros((pos.shape[0], LANES - ROPE_DIM), F32)
    cos, sin = jnp.cos(ang), jnp.sin(ang)
    return jnp.concatenate([cos, cos, z], axis=1), jnp.concatenate([sin, sin, z], axis=1)


def _rot_half_cols(w):
    half = ROPE_DIM // 2
    return jnp.concatenate([-w[..., half:], w[..., :half]], axis=-1)


def _layer(x_prompt, x_sample, past_lat, past_kr, ssm_s0, conv_s0, c_prompt, c_sample,
           w_ada, b_ada, g_norm1, w_in, g_cq, g_ckv, w_uq, w_uk, w_uv, g_qn, g_kn, conv_w, conv_b, dt_bias,
           a_log, d_skip, g_ssm, w_pa, w_pb, w_out, g_norm2, w_router, b_router, w_gate, b_gate, w_up, b_up,
           w_down, b_down):
    bp, sp, d = x_prompt.shape
    bs, ss, _ = x_sample.shape
    assert ss == CHUNK and sp % CHUNK == 0
    past = past_lat.shape[1]
    ql, kvl = g_cq.shape[-1], g_ckv.shape[-1]
    heads = w_uq.shape[1]
    ci = g_ssm.shape[-1]
    cc = conv_w.shape[-1]
    nh = ci // M_HEADDIM
    groups = (cc - ci) // (2 * D_STATE)
    nexp = w_router.shape[-1]
    tp, ts = bp * sp, bs * ss
    t = tp + ts
    ncp, ncs = tp // CHUNK, ts // CHUNK

    ada = _ada(jnp.concatenate([c_prompt, c_sample], axis=0), w_ada, b_ada)
    per_chunk = jnp.concatenate([jnp.repeat(ada[:bp], sp // CHUNK, axis=0), ada[bp:]], axis=0)
    sh1, sc1, gt1, sh2, sc2, gt2 = [m[:, None, :] for m in jnp.split(per_chunk, 6, axis=-1)]

    h3, x3 = _norm1(x_prompt.reshape(ncp, CHUNK, d), x_sample.reshape(ncs, CHUNK, d), g_norm1, sc1, sh1)
    h_all = h3.reshape(t, d)
    x_all = x3.reshape(t, d)

    o = 0
    w_cq = w_in[:, o:o + ql]; o += ql
    w_ckv = w_in[:, o:o + kvl]; o += kvl
    w_kr = w_in[:, o:o + ROPE_DIM]; o += ROPE_DIM
    w_z = w_in[:, o:o + ci]; o += ci
    w_xbc = w_in[:, o:o + cc]; o += cc
    w_dt = w_in[:, o:o + nh]; o += nh
    w_gab = w_in[:, o:o + 2 * d]
    zc = lambda n_: jnp.zeros((d, n_), F32)
    w_lat = jnp.concatenate([w_cq, w_ckv, w_kr, zc(LANES - ROPE_DIM), _rot_half_cols(w_kr),
                             zc(LANES - ROPE_DIM), w_dt, zc(LANES - nh)], axis=1).astype(BF16)

    pos = jnp.concatenate([jnp.tile(jnp.arange(sp), bp), jnp.tile(past + jnp.arange(ss), bs)])
    cos128, sin128 = _rope_tables(pos)

    cqn, lat_all, latk, kr_all, dt_all, dtt_all = _lat(
        h_all, w_lat, w_dt.T.astype(BF16), g_cq, g_ckv, cos128, sin128, dt_bias)
    z_all = _mm(h_all, w_z.astype(BF16), BF16, "proj_z")
    xbc_all = _mm(h_all, w_xbc.astype(BF16), BF16, "proj_xbc")
    gates = _mm(h_all, w_gab.astype(BF16), BF16, "proj_gates", act="sigmoid")

    tail = CONV_W - 1
    h_tail = jnp.concatenate([h3[:ncp].reshape(bp, sp, d)[:, sp - tail:, :].reshape(bp * tail, d),
                              h3[ncp:][:, CHUNK - tail:, :].reshape(bs * tail, d)], axis=0)
    conv_tail = _mm(h_tail, w_xbc.astype(BF16), F32, "proj_conv_tail")
    conv_p = conv_tail[:bp * tail].reshape(bp, tail, cc)
    conv_s = conv_tail[bp * tail:].reshape(bs, tail, cc)

    wq_a = jnp.concatenate([w_uq, jnp.zeros((ql, heads, HEAD_PAD - QK_DIM), F32)], axis=-1)
    wq_a = wq_a.reshape(ql, heads * HEAD_PAD).astype(BF16)
    wq_b = jnp.concatenate([_rot_half_cols(w_uq[..., NOPE_DIM:]),
                            jnp.zeros((ql, heads, LANES - ROPE_DIM), F32)], axis=-1)
    wq_b = wq_b.reshape(ql, heads * LANES).astype(BF16)
    gq = g_qn * g_kn * (ATTN_SCALE * math.log2(math.e))
    g_nope = gq[:NOPE_DIM].reshape(1, LANES)
    g_rope = jnp.concatenate([gq[NOPE_DIM:], jnp.zeros((LANES - ROPE_DIM,), F32)]).reshape(1, LANES)
    q_all = _qproj(cqn, wq_a, wq_b, cos128, sin128, g_nope, g_rope, heads)

    w_uk2 = w_uk.reshape(kvl, heads * NOPE_DIM)
    w_uv2 = w_uv.reshape(kvl, heads * V_DIM).astype(BF16)
    w_kv = jnp.concatenate([w_uk2.astype(BF16), w_uv2], axis=1)
    k_p, v_p = _kvproj(latk, w_kv, tp, heads, kvl)
    a_p = _attn_prompt(q_all, k_p, v_p, bp, sp, heads)
    a_s = _attn_sample(past_lat, past_kr, latk, q_all, w_uk2.T.astype(BF16), w_uv2, tp, heads)

    eh = jnp.repeat(jnp.eye(nh, dtype=BF16), M_HEADDIM, axis=1)
    dsk = jnp.repeat(d_skip, M_HEADDIM).reshape(1, ci)
    lcp = _pick(sp, (256, 128))
    assert sp % lcp == 0 and lcp % LANES == 0
    ncq = sp // lcp
    dt3_p = dt_all[:tp].reshape(bp * ncq, lcp, nh)
    dtt3_p = dtt_all[:, :tp].reshape(nh, bp * ncq, lcp).transpose(1, 0, 2)
    zpad = lambda a, axis: jnp.concatenate([a, jnp.zeros_like(a)], axis=axis)
    lcs = 2 * CHUNK
    dt3_s = zpad(dt_all[tp:].reshape(bs, CHUNK, nh), 1)
    dtt3_s = zpad(dtt_all[:, tp:].reshape(nh, bs, CHUNK).transpose(1, 0, 2), 2)
    pad_conv = lambda c0: jnp.concatenate(
        [jnp.zeros((c0.shape[0], SUBLANES - tail, cc), F32), c0], axis=1)
    m_p, ssm_p = _ssd(z_all, xbc_all, dt3_p, dtt3_p, jnp.zeros((bp, SUBLANES, cc), F32),
                      jnp.zeros((bp, ci, D_STATE), F32), conv_w, conv_b, a_log, dsk, g_ssm, eh,
                      0, bp, ncq, lcp, lcp, groups)
    m_s, ssm_s = _ssd(z_all, xbc_all, dt3_s, dtt3_s, pad_conv(conv_s0),
                      ssm_s0.reshape(bs, ci, D_STATE), conv_w, conv_b, a_log, dsk, g_ssm, eh,
                      tp, bs, 1, lcs, CHUNK, groups)

    mixed = _mix(a_p, a_s, m_p, m_s, gates, w_pa.astype(BF16), w_pb.astype(BF16))
    wr = jnp.concatenate([w_router, jnp.zeros((d, LANES - nexp), F32)], axis=1)
    wr_hi = wr.astype(BF16)
    wr_lo = (wr - wr_hi.astype(F32)).astype(BF16)
    br = jnp.concatenate([b_router, jnp.zeros((LANES - nexp,), F32)]).reshape(1, LANES)
    x1, h2, ti, tpr = _post(mixed, w_out.astype(BF16), x_all, gt1, sc2, sh2, g_norm2, wr_hi, wr_lo, br, nexp)

    assert nexp <= LANES
    bm = MOE_BLOCK
    n_blocks = -(-(t * TOP_K + nexp * (bm - 1)) // bm)
    tb = _pick(math.gcd(tp, ts), (128,))
    dest3, meta = _moe_slots(ti[:, :SUBLANES].T, bm, tb)
    block_e, next_e, n_used, zstart = _block_layout(meta, nexp, bm, n_blocks)
    xg = _moe_dispatch(zstart, dest3, h2, n_blocks * bm, bm)
    glu = _moe_gateup(block_e, next_e, n_used, xg, w_gate, w_up, b_gate, b_up)
    yb = _moe_down(block_e, next_e, n_used, glu, w_down, b_down)
    y_p = _moe_combine(dest3, yb, x1, gt2, tpr, 0, tp)
    y_s = _moe_combine(dest3, yb, x1, gt2, tpr, tp, ts)

    return (y_p.reshape(bp, sp, d), y_s.reshape(bs, ss, d),
            lat_all[:tp].reshape(bp, sp, kvl), kr_all[:tp].reshape(bp, sp, ROPE_DIM),
            ssm_p.reshape(bp, nh, M_HEADDIM, D_STATE), conv_p,
            lat_all[tp:].reshape(bs, ss, kvl), kr_all[tp:].reshape(bs, ss, ROPE_DIM),
            ssm_s.reshape(bs, nh, M_HEADDIM, D_STATE), conv_s)


def kernel(x_prompt, x_sample, cache_mla_latent, cache_mla_krope, state_ssm, state_conv, c_prompt, c_sample,
           w_ada, b_ada, g_norm1, w_in, g_cq, g_ckv, w_uq, w_uk, w_uv, g_qn, g_kn, conv_w, conv_b, dt_bias,
           a_log, d_skip, g_ssm, w_pa, w_pb, w_out, g_norm2, w_router, b_router, w_gate, b_gate, w_up, b_up,
           w_down, b_down):
    depth = w_ada.shape[0]
    assert depth == 1, "single-layer encoder"
    weights = (w_ada, b_ada, g_norm1, w_in, g_cq, g_ckv, w_uq, w_uk, w_uv, g_qn, g_kn, conv_w, conv_b, dt_bias,
               a_log, d_skip, g_ssm, w_pa, w_pb, w_out, g_norm2, w_router, b_router, w_gate, b_gate, w_up, b_up,
               w_down, b_down)
    outs = _layer(x_prompt, x_sample, cache_mla_latent[0], cache_mla_krope[0], state_ssm[0], state_conv[0],
                  c_prompt, c_sample, *[w[0] for w in weights])
    y_p, y_s = outs[0], outs[1]
    return (y_p, y_s) + tuple(o[None] for o in outs[2:6]) + tuple(o[None] for o in outs[6:])
```

```python
import functools
import math

import jax
import jax.numpy as jnp
from jax import lax
from jax.experimental import pallas as pl
from jax.experimental.pallas import tpu as pltpu

F32 = jnp.float32
BF16 = jnp.bfloat16
I32 = jnp.int32

CHUNK = 64
NOPE_DIM = 128
ROPE_DIM = 64
QK_DIM = NOPE_DIM + ROPE_DIM
V_DIM = 128
HEAD_PAD = 256
ROPE_THETA = 10000.0
ATTN_SCALE = QK_DIM ** -0.5
M_HEADDIM = 64
D_STATE = 128
CONV_W = 4
TOP_K = 4
SWIGLU_LIMIT = 7.0
SWIGLU_ALPHA = 1.702
EPS = 1e-6

LANES = 128
SUBLANES = 8
VMEM_LIMIT = 56 * 1024 * 1024

MOE_BLOCK = 256
WEIGHT_DMA_PRIORITY = 1
NEG_BIG = -1e30

_NT = (((1,), (1,)), ((), ()))
_TN = (((0,), (0,)), ((), ()))


def _cp(*sem):
    return pltpu.CompilerParams(dimension_semantics=sem, vmem_limit_bytes=VMEM_LIMIT)


def _pick(n, prefs):
    for p in prefs:
        if n % p == 0:
            return p
    return n


def _dot(a, b):
    return jnp.dot(a, b, preferred_element_type=F32)


def _split3(v):
    hi = v.astype(BF16)
    r1 = v - hi.astype(F32)
    mid = r1.astype(BF16)
    lo = (r1 - mid.astype(F32)).astype(BF16)
    return hi, mid, lo


def _silu(x):
    return x * jax.nn.sigmoid(x)


def _softplus(x):
    return jnp.maximum(x, 0.0) + jnp.log1p(jnp.exp(-jnp.abs(x)))


def _ada_kernel(c_ref, w_ref, b_ref, o_ref):
    s = _silu(c_ref[...]).astype(BF16)
    o_ref[...] = _dot(s, w_ref[...].astype(BF16)) + b_ref[...]


def _ada(c_all, w_ada, b_ada):
    r, d = c_all.shape
    n = w_ada.shape[1]
    tn = _pick(n, (1024, 512, 256, 128))
    return pl.pallas_call(
        _ada_kernel,
        out_shape=jax.ShapeDtypeStruct((r, n), F32),
        grid=(n // tn,),
        in_specs=[pl.BlockSpec((r, d), lambda j: (0, 0)),
                  pl.BlockSpec((d, tn), lambda j: (0, j)),
                  pl.BlockSpec((1, tn), lambda j: (0, j))],
        out_specs=pl.BlockSpec((r, tn), lambda j: (0, j)),
        compiler_params=_cp("arbitrary"),
        name="ada",
    )(c_all, w_ada, b_ada.reshape(1, n))


def _norm1_kernel(xp_ref, xs_ref, g_ref, sc_ref, sh_ref, h_ref, *, npb):
    i = pl.program_id(0)
    x = jnp.where(i < npb, xp_ref[...], xs_ref[...])
    xn = x * lax.rsqrt(jnp.mean(x * x, axis=-1, keepdims=True) + EPS)
    h_ref[...] = (xn * g_ref[...] * (1.0 + sc_ref[...]) + sh_ref[...]).astype(BF16)


def _norm1(xp3, xs3, g, sc, sh):
    ncp, _, d = xp3.shape
    ncs = xs3.shape[0]
    nch = ncp + ncs
    gc = _pick(math.gcd(ncp, ncs), (4, 2, 1))
    npb = ncp // gc
    blk = (gc, CHUNK, d)
    mod = pl.BlockSpec((gc, 1, d), lambda i: (i, 0, 0))
    return pl.pallas_call(
        functools.partial(_norm1_kernel, npb=npb),
        out_shape=jax.ShapeDtypeStruct((nch, CHUNK, d), BF16),
        grid=(nch // gc,),
        in_specs=[pl.BlockSpec(blk, lambda i: (jnp.minimum(i, npb - 1), 0, 0)),
                  pl.BlockSpec(blk, lambda i: (jnp.maximum(i - npb, 0), 0, 0)),
                  pl.BlockSpec((1, 1, d), lambda i: (0, 0, 0)),
                  mod, mod],
        out_specs=pl.BlockSpec(blk, lambda i: (i, 0, 0)),
        compiler_params=_cp("arbitrary"),
        name="norm1",
    )(xp3, xs3, g.reshape(1, 1, d), sc, sh)


def _mm_kernel(x_ref, w_ref, o_ref, *, act):
    acc = _dot(x_ref[...], w_ref[...])
    if act == "sigmoid":
        acc = jax.nn.sigmoid(acc)
    o_ref[...] = acc.astype(o_ref.dtype)


def _mm(x, w, out_dtype, name, act=None):
    m, k = x.shape
    n = w.shape[1]
    tm = _pick(m, (1024, 512, 256))
    tn = _pick(n, (1024, 512, 256, 128))
    return pl.pallas_call(
        functools.partial(_mm_kernel, act=act),
        out_shape=jax.ShapeDtypeStruct((m, n), out_dtype),
        grid=(m // tm, n // tn),
        in_specs=[pl.BlockSpec((tm, k), lambda i, j: (i, 0)),
                  pl.BlockSpec((k, tn), lambda i, j: (0, j))],
        out_specs=pl.BlockSpec((tm, tn), lambda i, j: (i, j)),
        compiler_params=_cp("arbitrary", "arbitrary"),
        name=name,
    )(x, w)


def _lat_kernel(h_ref, w_ref, wdt_ref, gcq_ref, gckv_ref, cos_ref, sin_ref, dtb_ref, dtbc_ref,
                cqn_ref, lat_ref, latk_ref, kr_ref, dt_ref, dtt_ref, *, ql, kvl):
    h = h_ref[...]
    acc = _dot(h, w_ref[...])
    cq = acc[:, :ql]
    cqn = cq * lax.rsqrt(jnp.mean(cq * cq, axis=-1, keepdims=True) + EPS) * gcq_ref[...]
    cqn_ref[...] = cqn.astype(BF16)
    ckv = acc[:, ql:ql + kvl]
    lat = ckv * lax.rsqrt(jnp.mean(ckv * ckv, axis=-1, keepdims=True) + EPS) * gckv_ref[...]
    lat_ref[...] = lat
    o = ql + kvl
    kr128 = acc[:, o:o + LANES] * cos_ref[...] + acc[:, o + LANES:o + 2 * LANES] * sin_ref[...]
    kr_ref[...] = kr128[:, :ROPE_DIM]
    latk_ref[:, :kvl] = lat.astype(BF16)
    latk_ref[:, kvl:] = kr128.astype(BF16)
    nh = dt_ref.shape[-1]
    dt_ref[...] = _softplus(acc[:, o + 2 * LANES:o + 2 * LANES + nh] + dtb_ref[...])
    dtt = lax.dot_general(wdt_ref[...], h, _NT, preferred_element_type=F32)
    dtt_ref[...] = _softplus(dtt + dtbc_ref[...])


def _lat(h_all, w_lat, w_dt_t, g_cq, g_ckv, cos128, sin128, dt_bias):
    t, d = h_all.shape
    ql, kvl = g_cq.shape[-1], g_ckv.shape[-1]
    nh = dt_bias.shape[-1]
    nl = w_lat.shape[1]
    tm = _pick(t, (512, 256, 128))
    row = lambda w: pl.BlockSpec((tm, w), lambda i: (i, 0))
    const = lambda a, b: pl.BlockSpec((a, b), lambda i: (0, 0))
    return pl.pallas_call(
        functools.partial(_lat_kernel, ql=ql, kvl=kvl),
        out_shape=(jax.ShapeDtypeStruct((t, ql), BF16),
                   jax.ShapeDtypeStruct((t, kvl), F32),
                   jax.ShapeDtypeStruct((t, kvl + LANES), BF16),
                   jax.ShapeDtypeStruct((t, ROPE_DIM), F32),
                   jax.ShapeDtypeStruct((t, nh), F32),
                   jax.ShapeDtypeStruct((nh, t), F32)),
        grid=(t // tm,),
        in_specs=[row(d), const(d, nl), const(nh, d), const(1, ql), const(1, kvl),
                  row(LANES), row(LANES), const(1, nh), const(nh, 1)],
        out_specs=(row(ql), row(kvl), row(kvl + LANES), row(ROPE_DIM), row(nh),
                   pl.BlockSpec((nh, tm), lambda i: (0, i))),
        compiler_params=_cp("arbitrary"),
        name="latent_proj",
    )(h_all, w_lat, w_dt_t, g_cq.reshape(1, ql), g_ckv.reshape(1, kvl), cos128, sin128,
      dt_bias.reshape(1, nh), dt_bias.reshape(nh, 1))


def _q_kernel(c_ref, wa_ref, wb_ref, cos_ref, sin_ref, gn_ref, gr_ref, o_ref, *, heads):
    c = c_ref[...]
    a = _dot(c, wa_ref[...])
    b = _dot(c, wb_ref[...])
    cos, sin = cos_ref[...], sin_ref[...]
    for h in range(heads):
        nope = a[:, h * HEAD_PAD:h * HEAD_PAD + LANES]
        rope = a[:, h * HEAD_PAD + LANES:(h + 1) * HEAD_PAD] * cos + b[:, h * LANES:(h + 1) * LANES] * sin
        ss = (jnp.sum(nope * nope, axis=-1, keepdims=True)
              + jnp.sum(rope * rope, axis=-1, keepdims=True)) * (1.0 / QK_DIM)
        r = lax.rsqrt(ss + EPS)
        o_ref[:, h * HEAD_PAD:h * HEAD_PAD + LANES] = (nope * r * gn_ref[...]).astype(BF16)
        o_ref[:, h * HEAD_PAD + LANES:(h + 1) * HEAD_PAD] = (rope * r * gr_ref[...]).astype(BF16)


def _qproj(cqn, wq_a, wq_b, cos128, sin128, g_nope, g_rope, heads):
    t, ql = cqn.shape
    tm = _pick(t, (256, 128))
    row = lambda w: pl.BlockSpec((tm, w), lambda i: (i, 0))
    const = lambda a, b: pl.BlockSpec((a, b), lambda i: (0, 0))
    return pl.pallas_call(
        functools.partial(_q_kernel, heads=heads),
        out_shape=jax.ShapeDtypeStruct((t, heads * HEAD_PAD), BF16),
        grid=(t // tm,),
        in_specs=[row(ql), const(ql, heads * HEAD_PAD), const(ql, heads * LANES),
                  row(LANES), row(LANES), const(1, LANES), const(1, LANES)],
        out_specs=row(heads * HEAD_PAD),
        compiler_params=_cp("arbitrary"),
        name="q_proj",
    )(cqn, wq_a, wq_b, cos128, sin128, g_nope, g_rope)


def _kv_kernel(lat_ref, kr_ref, w_ref, k_ref, v_ref, *, heads):
    acc = _dot(lat_ref[...], w_ref[...])
    kr = kr_ref[...].astype(F32)
    kr2 = jnp.sum(kr * kr, axis=-1, keepdims=True)
    for h in range(heads):
        kn = acc[:, h * LANES:(h + 1) * LANES]
        ss = (jnp.sum(kn * kn, axis=-1, keepdims=True) + kr2) * (1.0 / QK_DIM)
        r = lax.rsqrt(ss + EPS)
        k_ref[:, h * HEAD_PAD:h * HEAD_PAD + LANES] = (kn * r).astype(BF16)
        k_ref[:, h * HEAD_PAD + LANES:(h + 1) * HEAD_PAD] = (kr * r).astype(BF16)
    v_ref[...] = acc[:, heads * LANES:].astype(BF16)


def _kvproj(latk, w_kv, tp, heads, kvl):
    tm = _pick(tp, (256, 128))
    return pl.pallas_call(
        functools.partial(_kv_kernel, heads=heads),
        out_shape=(jax.ShapeDtypeStruct((tp, heads * HEAD_PAD), BF16),
                   jax.ShapeDtypeStruct((tp, heads * V_DIM), BF16)),
        grid=(tp // tm,),
        in_specs=[pl.BlockSpec((tm, kvl), lambda i: (i, 0)),
                  pl.BlockSpec((tm, LANES), lambda i: (i, kvl // LANES)),
                  pl.BlockSpec((kvl, 2 * heads * LANES), lambda i: (0, 0))],
        out_specs=(pl.BlockSpec((tm, heads * HEAD_PAD), lambda i: (i, 0)),
                   pl.BlockSpec((tm, heads * V_DIM), lambda i: (i, 0))),
        compiler_params=_cp("arbitrary"),
        name="kv_proj",
    )(latk, latk, w_kv)


def _attn_p_kernel(q_ref, k_ref, v_ref, o_ref, m_ref, l_ref, acc_ref, *, tq, hg):
    qi = pl.program_id(2)
    m_ref[...] = jnp.full(m_ref.shape, -jnp.inf, F32)
    l_ref[...] = jnp.zeros(l_ref.shape, F32)
    acc_ref[...] = jnp.zeros(acc_ref.shape, F32)

    def tile(j, masked):
        ks = pl.multiple_of(j * tq, tq)
        for g in range(hg):
            q = q_ref[:, g * HEAD_PAD:(g + 1) * HEAD_PAD]
            k = k_ref[pl.ds(ks, tq), g * HEAD_PAD:(g + 1) * HEAD_PAD]
            v = v_ref[pl.ds(ks, tq), g * V_DIM:(g + 1) * V_DIM]
            s = lax.dot_general(q, k, _NT, preferred_element_type=F32)
            if masked:
                rc = lax.broadcasted_iota(I32, (tq, tq), 0) // CHUNK
                cc = lax.broadcasted_iota(I32, (tq, tq), 1) // CHUNK
                s = jnp.where(cc <= rc, s, -jnp.inf)
            m_prev = m_ref[g]
            m_new = jnp.maximum(m_prev, jnp.max(s, axis=-1, keepdims=True))
            alpha = jnp.exp2(m_prev - m_new)
            p = jnp.exp2(s - jnp.tile(m_new, (1, tq // LANES)))
            l_ref[g] = alpha * l_ref[g] + jnp.sum(p, axis=-1, keepdims=True)
            acc_ref[g] = alpha * acc_ref[g] + _dot(p.astype(BF16), v)
            m_ref[g] = m_new

    def body(j, carry):
        tile(j, False)
        return carry

    lax.fori_loop(0, qi, body, 0)
    tile(qi, True)
    for g in range(hg):
        o_ref[:, g * V_DIM:(g + 1) * V_DIM] = (acc_ref[g] / l_ref[g]).astype(o_ref.dtype)


def _attn_prompt(q_all, k_p, v_p, batch, seq, heads):
    tq = _pick(seq, (512, 256, 128, 64))
    nq = seq // tq
    hg = _pick(heads, (4, 2, 1))
    return pl.pallas_call(
        functools.partial(_attn_p_kernel, tq=tq, hg=hg),
        out_shape=jax.ShapeDtypeStruct((batch * seq, heads * V_DIM), BF16),
        grid=(batch, heads // hg, nq),
        in_specs=[pl.BlockSpec((tq, hg * HEAD_PAD), lambda b, h, qi: (b * nq + qi, h)),
                  pl.BlockSpec((seq, hg * HEAD_PAD), lambda b, h, qi: (b, h)),
                  pl.BlockSpec((seq, hg * V_DIM), lambda b, h, qi: (b, h))],
        out_specs=pl.BlockSpec((tq, hg * V_DIM), lambda b, h, qi: (b * nq + qi, h)),
        scratch_shapes=[pltpu.VMEM((hg, tq, LANES), F32), pltpu.VMEM((hg, tq, LANES), F32),
                        pltpu.VMEM((hg, tq, V_DIM), F32)],
        compiler_params=_cp("arbitrary", "arbitrary", "arbitrary"),
        name="attn_prompt",
    )(q_all, k_p, v_p)


def _attn_s_kernel(pl_ref, pk_ref, nl_ref, q_ref, wukt_ref, wuv_ref, o_ref,
                   qabs_ref, qr_ref, m_ref, l_ref, acc_ref, *, heads, kvl, nkb):
    kb = pl.program_id(1)

    @pl.when(kb == 0)
    def _():
        for h in range(heads):
            qn = q_ref[:, h * HEAD_PAD:h * HEAD_PAD + LANES]
            qabs_ref[h * CHUNK:(h + 1) * CHUNK, :] = _dot(
                qn, wukt_ref[h * LANES:(h + 1) * LANES, :]).astype(BF16)
            qr_ref[h * CHUNK:(h + 1) * CHUNK, :] = q_ref[:, h * HEAD_PAD + LANES:(h + 1) * HEAD_PAD]
        m_ref[...] = jnp.full(m_ref.shape, -jnp.inf, F32)
        l_ref[...] = jnp.zeros(l_ref.shape, F32)
        acc_ref[...] = jnp.zeros(acc_ref.shape, F32)

    def block(xl, krf):
        rtop = lax.dot_general(wukt_ref[...], xl, _NT, preferred_element_type=F32)
        sq = krf * krf
        sq_hi = sq.astype(BF16)
        sq_lo = (sq - sq_hi.astype(F32)).astype(BF16)
        ones = jnp.ones((SUBLANES, ROPE_DIM), BF16)
        kr2 = (lax.dot_general(ones, sq_hi, _NT, preferred_element_type=F32)
               + lax.dot_general(ones, sq_lo, _NT, preferred_element_type=F32))[0:1, :]
        s = (lax.dot_general(qabs_ref[...], xl, _NT, preferred_element_type=F32)
             + lax.dot_general(qr_ref[:, :ROPE_DIM], krf.astype(BF16), _NT,
                               preferred_element_type=F32))
        parts = []
        for h in range(heads):
            rt = rtop[h * LANES:(h + 1) * LANES, :]
            kn2 = jnp.sum(rt * rt, axis=0, keepdims=True)
            r = lax.rsqrt((kn2 + kr2) * (1.0 / QK_DIM) + EPS)
            parts.append(s[h * CHUNK:(h + 1) * CHUNK, :] * r)
        s = jnp.concatenate(parts, axis=0)
        n = s.shape[1]
        m_prev = m_ref[...]
        m_new = jnp.maximum(m_prev, jnp.max(s, axis=-1, keepdims=True))
        alpha = jnp.exp2(m_prev - m_new)
        m_wide = jnp.tile(m_new, (1, n // LANES)) if n >= LANES else m_new[:, :n]
        p = jnp.exp2(s - m_wide)
        l_ref[...] = alpha * l_ref[...] + jnp.sum(p, axis=-1, keepdims=True)
        acc_ref[...] = jnp.tile(alpha, (1, kvl // LANES)) * acc_ref[...] + _dot(p.astype(BF16), xl)
        m_ref[...] = m_new

    @pl.when(kb < nkb)
    def _():
        block(pl_ref[0].astype(BF16), pk_ref[0])

    @pl.when(kb == nkb)
    def _():
        block(nl_ref[:, :kvl], nl_ref[:, kvl:kvl + ROPE_DIM].astype(F32))
        o = (acc_ref[...] / jnp.tile(l_ref[...], (1, kvl // LANES))).astype(BF16)
        for h in range(heads):
            o_ref[:, h * V_DIM:(h + 1) * V_DIM] = _dot(
                o[h * CHUNK:(h + 1) * CHUNK, :], wuv_ref[:, h * V_DIM:(h + 1) * V_DIM]).astype(o_ref.dtype)


def _attn_sample(past_lat, past_kr, latk, q_all, w_uk_t, w_uv2, tp, heads):
    bs, past, kvl = past_lat.shape
    tk = _pick(past, (512, 256, 128))
    nkb = past // tk
    c0 = tp // CHUNK
    pidx = lambda b, kb: (b, jnp.minimum(kb, nkb - 1), 0)
    hq = heads * CHUNK
    return pl.pallas_call(
        functools.partial(_attn_s_kernel, heads=heads, kvl=kvl, nkb=nkb),
        out_shape=jax.ShapeDtypeStruct((bs * CHUNK, heads * V_DIM), BF16),
        grid=(bs, nkb + 1),
        in_specs=[pl.BlockSpec((1, tk, kvl), pidx),
                  pl.BlockSpec((1, tk, ROPE_DIM), pidx),
                  pl.BlockSpec((CHUNK, kvl + LANES), lambda b, kb: (c0 + b, 0)),
                  pl.BlockSpec((CHUNK, heads * HEAD_PAD), lambda b, kb: (c0 + b, 0)),
                  pl.BlockSpec((heads * LANES, kvl), lambda b, kb: (0, 0)),
                  pl.BlockSpec((kvl, heads * V_DIM), lambda b, kb: (0, 0))],
        out_specs=pl.BlockSpec((CHUNK, heads * V_DIM), lambda b, kb: (b, 0)),
        scratch_shapes=[pltpu.VMEM((hq, kvl), BF16), pltpu.VMEM((hq, LANES), BF16),
                        pltpu.VMEM((hq, LANES), F32), pltpu.VMEM((hq, LANES), F32),
                        pltpu.VMEM((hq, kvl), F32)],
        compiler_params=_cp("arbitrary", "arbitrary"),
        name="attn_sample",
    )(past_lat, past_kr, latk, q_all, w_uk_t, w_uv2)


def _ssd_kernel(z_ref, x_ref, dt_ref, dtt_ref, conv0_ref, ssm0_ref, cw_ref, cb_ref, alr_ref, alc_ref,
                dsk_ref, gs_ref, eh_ref, o_ref, st_out_ref, xs_ref, st_ref,
                *, lc, lr, nh, groups):
    c = pl.program_id(1)
    p = M_HEADDIM
    n = D_STATE
    ci = nh * p
    k8 = nh // groups
    gw = k8 * p

    @pl.when(c == 0)
    def _():
        xs_ref[0:SUBLANES, :] = conv0_ref[0]
        st_ref[...] = ssm0_ref[0]

    xs_ref[SUBLANES:SUBLANES + lr, :] = x_ref[...].astype(F32)
    if lc > lr:
        xs_ref[SUBLANES + lr:SUBLANES + lc, :] = jnp.zeros((lc - lr, xs_ref.shape[1]), F32)

    def conv(lo, hi):
        u = xs_ref[SUBLANES - 3:SUBLANES - 3 + lc, lo:hi] * cw_ref[0:1, lo:hi]
        for tap in range(1, CONV_W):
            u = u + xs_ref[SUBLANES - 3 + tap:SUBLANES - 3 + tap + lc, lo:hi] * cw_ref[tap:tap + 1, lo:hi]
        return _silu(u + cb_ref[:, lo:hi])

    dt = dt_ref[0]
    dtt = dtt_ref[0]
    a_row = -jnp.exp(alr_ref[...])
    a_col = -jnp.exp(alc_ref[...])
    ri = lax.broadcasted_iota(I32, (lc, lc), 0)
    cidx = lax.broadcasted_iota(I32, (lc, lc), 1)
    tri = ri >= cidx
    tril = jnp.where(tri, 1.0, 0.0).astype(BF16)
    triu = jnp.where(ri <= cidx, 1.0, 0.0).astype(BF16)
    cs = sum(_dot(tril, piece) for piece in _split3(dt * a_row))
    cst = sum(_dot(piece, triu) for piece in _split3(dtt * a_col))
    exp_cs = jnp.exp(cs)
    w_end = jnp.exp(cs[lc - 1:lc, :] - cs)
    stacked = jnp.concatenate([dt, exp_cs, w_end], axis=0)
    eh = eh_ref[...]
    expanded = sum(_dot(piece, eh) for piece in _split3(stacked))
    dt_e, ecs_e, wend_e = expanded[:lc], expanded[lc:2 * lc], expanded[2 * lc:]
    cdec = jnp.exp(cst[:, lc - 1:lc])
    lane_lo = lax.broadcasted_iota(I32, (lc, LANES), 1) < p

    for g in range(groups):
        gs = slice(g * gw, (g + 1) * gw)
        xg = conv(g * gw, (g + 1) * gw)
        bg = conv(ci + g * n, ci + (g + 1) * n).astype(BF16)
        cg = conv(ci + groups * n + g * n, ci + groups * n + (g + 1) * n).astype(BF16)
        cbm = lax.dot_general(cg, bg, _NT, preferred_element_type=F32)
        xdt = xg * dt_e[:, gs]
        xdt_b = xdt.astype(BF16)
        pairs = []
        for q in range(k8 // 2):
            x2 = xdt_b[:, q * LANES:(q + 1) * LANES]
            ys = []
            for s in range(2):
                h = g * k8 + 2 * q + s
                seg = cs[:, h:h + 1] - cst[h:h + 1, :]
                dec = jnp.exp(jnp.where(tri, seg, -jnp.inf))
                ys.append(_dot((cbm * dec).astype(BF16), x2))
            pairs.append(jnp.where(lane_lo, ys[0], ys[1]))
        y_diag = jnp.concatenate(pairs, axis=1)
        sg = st_ref[g * gw:(g + 1) * gw, :]
        y_off = lax.dot_general(cg, sg.astype(BF16), _NT, preferred_element_type=F32) * ecs_e[:, gs]
        y = y_diag + y_off + xg * dsk_ref[:, gs]
        xw = (xdt * wend_e[:, gs]).astype(BF16)
        upd = lax.dot_general(xw, bg, _TN, preferred_element_type=F32)
        for k in range(k8):
            h = g * k8 + k
            rows = slice(g * gw + k * p, g * gw + (k + 1) * p)
            st_ref[rows, :] = st_ref[rows, :] * cdec[h:h + 1, :] + upd[k * p:(k + 1) * p, :]
        zg = z_ref[:, gs].astype(F32)
        u2 = y[:lr] * _silu(zg)
        ms = jnp.mean(u2 * u2, axis=-1, keepdims=True)
        o_ref[:, gs] = (u2 * lax.rsqrt(ms + EPS) * gs_ref[:, gs]).astype(o_ref.dtype)

    xs_ref[0:SUBLANES, :] = xs_ref[lr:lr + SUBLANES, :]

    @pl.when(c == pl.num_programs(1) - 1)
    def _():
        st_out_ref[0] = st_ref[...]


def _ssd(z_all, xbc_all, dt3, dtt3, conv0p, ssm0, conv_w, conv_b, a_log, dsk, g_ssm, eh,
         row0, nseq, nchunk, lc, lr, groups):
    ci = z_all.shape[1]
    cc = xbc_all.shape[1]
    nh = ci // M_HEADDIM
    rb0 = row0 // lr
    rowblk = lambda w: pl.BlockSpec((lr, w), lambda b, c: (rb0 + b * nchunk + c, 0))
    seq3 = lambda a, b_: pl.BlockSpec((1, a, b_), lambda b, c: (b * nchunk + c, 0, 0))
    perb = lambda a, b_: pl.BlockSpec((1, a, b_), lambda b, c: (b, 0, 0))
    const = lambda a, b_: pl.BlockSpec((a, b_), lambda b, c: (0, 0))
    return pl.pallas_call(
        functools.partial(_ssd_kernel, lc=lc, lr=lr, nh=nh, groups=groups),
        out_shape=(jax.ShapeDtypeStruct((nseq * nchunk * lr, ci), BF16),
                   jax.ShapeDtypeStruct((nseq, ci, D_STATE), F32)),
        grid=(nseq, nchunk),
        in_specs=[rowblk(ci), rowblk(cc), seq3(lc, nh), seq3(nh, lc),
                  perb(SUBLANES, cc), perb(ci, D_STATE),
                  const(CONV_W, cc), const(1, cc), const(1, nh), const(nh, 1),
                  const(1, ci), const(1, ci), const(nh, ci)],
        out_specs=(pl.BlockSpec((lr, ci), lambda b, c: (b * nchunk + c, 0)),
                   perb(ci, D_STATE)),
        scratch_shapes=[pltpu.VMEM((lc + SUBLANES, cc), F32), pltpu.VMEM((ci, D_STATE), F32)],
        compiler_params=_cp("arbitrary", "arbitrary"),
        name="ssd",
    )(z_all, xbc_all, dt3, dtt3, conv0p, ssm0, conv_w, conv_b.reshape(1, cc),
      a_log.reshape(1, nh), a_log.reshape(nh, 1), dsk, g_ssm.reshape(1, ci), eh)


def _mix_kernel(ap_ref, as_ref, mp_ref, ms_ref, ga_ref, gb_ref, wpa_ref, wpb_ref, o_ref, *, npb):
    i = pl.program_id(0)
    a = jnp.where(i < npb, ap_ref[...], as_ref[...])
    m = jnp.where(i < npb, mp_ref[...], ms_ref[...])
    pa = _dot(a, wpa_ref[...])
    pb = _dot(m, wpb_ref[...])
    o_ref[...] = (ga_ref[...].astype(F32) * pa + gb_ref[...].astype(F32) * pb).astype(o_ref.dtype)


def _mix(a_p, a_s, m_p, m_s, gates, w_pa, w_pb):
    tp, hv = a_p.shape
    ts = a_s.shape[0]
    ci = m_p.shape[1]
    d = w_pa.shape[1]
    tm = _pick(math.gcd(tp, ts), (512, 256, 128, 64))
    tn = _pick(d, (512, 256, 128))
    npb = tp // tm
    nj = d // tn
    pidx = lambda i, j: (jnp.minimum(i, npb - 1), 0)
    sidx = lambda i, j: (jnp.maximum(i - npb, 0), 0)
    return pl.pallas_call(
        functools.partial(_mix_kernel, npb=npb),
        out_shape=jax.ShapeDtypeStruct((tp + ts, d), BF16),
        grid=((tp + ts) // tm, nj),
        in_specs=[pl.BlockSpec((tm, hv), pidx), pl.BlockSpec((tm, hv), sidx),
                  pl.BlockSpec((tm, ci), pidx), pl.BlockSpec((tm, ci), sidx),
                  pl.BlockSpec((tm, tn), lambda i, j: (i, j)),
                  pl.BlockSpec((tm, tn), lambda i, j: (i, nj + j)),
                  pl.BlockSpec((hv, tn), lambda i, j: (0, j)),
                  pl.BlockSpec((ci, tn), lambda i, j: (0, j))],
        out_specs=pl.BlockSpec((tm, tn), lambda i, j: (i, j)),
        compiler_params=_cp("arbitrary", "arbitrary"),
        name="branch_mix",
    )(a_p, a_s, m_p, m_s, gates, gates, w_pa, w_pb)


def _pack_bf16_pairs(x):
    n = x.shape[1] // 2
    lo = pltpu.bitcast(x[:, :n].astype(BF16).astype(F32), jnp.uint32)
    hi = pltpu.bitcast(x[:, n:].astype(BF16).astype(F32), jnp.uint32)
    return hi | (lo >> 16)


def _unpack_bf16_pairs(u):
    lo = pltpu.bitcast(u << 16, F32)
    hi = pltpu.bitcast(u & jnp.uint32(0xFFFF0000), F32)
    return jnp.concatenate([lo, hi], axis=1)


def _post_kernel(mx_ref, wout_ref, xp_ref, xs_ref, gt1_ref, sc2_ref, sh2_ref, g2_ref, wrh_ref, wrl_ref, br_ref,
                 x1_ref, h2p_ref, ti_ref, tp_ref, h2_ref, *, nexp, npb):
    is_prompt = pl.program_id(0) < npb
    o = _dot(mx_ref[...], wout_ref[...])
    tm = o.shape[0]
    for ch in range(tm // CHUNK):
        rows = slice(ch * CHUNK, (ch + 1) * CHUNK)
        x1 = jnp.where(is_prompt, xp_ref[ch], xs_ref[ch]) + gt1_ref[ch] * o[rows, :]
        x1_ref[rows, :] = x1
        xn = x1 * lax.rsqrt(jnp.mean(x1 * x1, axis=-1, keepdims=True) + EPS)
        h2_ref[rows, :] = xn * g2_ref[...] * (1.0 + sc2_ref[ch]) + sh2_ref[ch]
    h2 = h2_ref[...]
    h2p_ref[...] = _pack_bf16_pairs(h2)
    hh = h2.astype(BF16)
    hl = (h2 - hh.astype(F32)).astype(BF16)
    logits = _dot(hh, wrh_ref[...]) + _dot(hh, wrl_ref[...]) + _dot(hl, wrh_ref[...]) + br_ref[...]
    lane = lax.broadcasted_iota(I32, logits.shape, 1)
    logits = jnp.where(lane < nexp, logits, NEG_BIG)
    vals, idxs = [], []
    for _ in range(TOP_K):
        m = jnp.max(logits, axis=-1, keepdims=True)
        idx = jnp.min(jnp.where(logits == m, lane, LANES), axis=-1, keepdims=True)
        vals.append(m)
        idxs.append(idx)
        logits = jnp.where(lane == idx, 2.0 * NEG_BIG, logits)
    es = [jnp.exp(v - vals[0]) for v in vals]
    den = es[0]
    for e in es[1:]:
        den = den + e
    ti = jnp.zeros(lane.shape, I32)
    tpv = jnp.zeros(lane.shape, F32)
    for k in range(TOP_K):
        ti = jnp.where(lane == k, idxs[k], ti)
        tpv = jnp.where(lane == k, es[k] / den, tpv)
    ti_ref[...] = ti
    tp_ref[...] = tpv


def _post(mixed, w_out, xp3, xs3, gt1, sc2, sh2, g2, wr_hi, wr_lo, br, nexp):
    t, d = mixed.shape
    ncp, ncs = xp3.shape[0], xs3.shape[0]
    nc = _pick(math.gcd(ncp, ncs), (4, 2, 1))
    tm = nc * CHUNK
    npb = ncp // nc
    row = lambda w: pl.BlockSpec((tm, w), lambda i: (i, 0))
    mod = pl.BlockSpec((nc, 1, d), lambda i: (i, 0, 0))
    const = lambda a, b: pl.BlockSpec((a, b), lambda i: (0, 0))
    return pl.pallas_call(
        functools.partial(_post_kernel, nexp=nexp, npb=npb),
        out_shape=(jax.ShapeDtypeStruct((t, d), F32), jax.ShapeDtypeStruct((t, d // 2), jnp.uint32),
                   jax.ShapeDtypeStruct((t, LANES), I32), jax.ShapeDtypeStruct((t, LANES), F32)),
        grid=(t // tm,),
        in_specs=[row(d), const(d, d),
                  pl.BlockSpec((nc, CHUNK, d), lambda i: (jnp.minimum(i, npb - 1), 0, 0)),
                  pl.BlockSpec((nc, CHUNK, d), lambda i: (jnp.maximum(i - npb, 0), 0, 0)),
                  mod, mod, mod, const(1, d),
                  const(d, LANES), const(d, LANES), const(1, LANES)],
        out_specs=(row(d), row(d // 2), row(LANES), row(LANES)),
        scratch_shapes=[pltpu.VMEM((tm, d), F32)],
        compiler_params=_cp("arbitrary"),
        name="post_mix",
    )(mixed, w_out, xp3, xs3, gt1, sc2, sh2, g2.reshape(1, d), wr_hi, wr_lo, br)


def _slots_kernel(ti_ref, dest_ref, meta_ref, run_ref, *, bm, tb):
    ph = pl.program_id(0)
    i = pl.program_id(1)
    eid = lax.broadcasted_iota(I32, (LANES, tb), 0)
    onehots = [jnp.where(eid == ti_ref[k:k + 1, :], 1.0, 0.0) for k in range(TOP_K)]
    osum = onehots[0]
    for oh in onehots[1:]:
        osum = osum + oh
    blk_cnt = jnp.sum(osum, axis=1, keepdims=True)

    @pl.when(jnp.logical_and(ph == 0, i == 0))
    def _():
        run_ref[...] = jnp.zeros(run_ref.shape, F32)

    @pl.when(ph == 0)
    def _():
        run_ref[...] = run_ref[...] + blk_cnt

    @pl.when(jnp.logical_and(ph == 1, i == 0))
    def _():
        cnt = run_ref[...]
        nblk = jnp.ceil(cnt * (1.0 / bm))
        r = lax.broadcasted_iota(I32, (LANES, LANES), 0)
        c = lax.broadcasted_iota(I32, (LANES, LANES), 1)
        lstrict = jnp.where(c < r, 1.0, 0.0).astype(BF16)
        start_blk = _dot(lstrict, jnp.broadcast_to(nblk, (LANES, LANES)).astype(BF16))
        lane = lax.broadcasted_iota(I32, (LANES, LANES), 1)
        meta_ref[...] = jnp.where(lane == 0, cnt, jnp.where(lane == 1, start_blk, 0.0))
        run_ref[...] = start_blk[:, 0:1] * float(bm)

    @pl.when(ph == 1)
    def _():
        r = lax.broadcasted_iota(I32, (tb, tb), 0)
        c = lax.broadcasted_iota(I32, (tb, tb), 1)
        ustrict = jnp.where(r < c, 1.0, 0.0).astype(BF16)
        base = run_ref[...] + _dot(osum.astype(BF16), ustrict)
        for k in range(TOP_K):
            dest_ref[0, k:k + 1, :] = jnp.sum(onehots[k] * base, axis=0, keepdims=True).astype(I32)
        dest_ref[0, TOP_K:, :] = jnp.zeros((SUBLANES - TOP_K, tb), I32)
        run_ref[...] = run_ref[...] + blk_cnt


def _moe_slots(ti_t, bm, tb):
    t = ti_t.shape[1]
    nt = t // tb
    return pl.pallas_call(
        functools.partial(_slots_kernel, bm=bm, tb=tb),
        out_shape=(jax.ShapeDtypeStruct((nt, SUBLANES, tb), I32),
                   jax.ShapeDtypeStruct((LANES, LANES), F32)),
        grid=(2, nt),
        in_specs=[pl.BlockSpec((SUBLANES, tb), lambda ph, i: (0, i))],
        out_specs=(pl.BlockSpec((1, SUBLANES, tb), lambda ph, i: (i * ph, 0, 0)),
                   pl.BlockSpec((LANES, LANES), lambda ph, i: (0, 0))),
        scratch_shapes=[pltpu.VMEM((LANES, 1), F32)],
        compiler_params=_cp("arbitrary", "arbitrary"),
        name="moe_slots",
    )(ti_t)


def _dispatch_kernel(zs_ref, idx_ref, h_ref, xg_ref, zbuf_ref, idx_smem, isem, sem, *, nz, bm, tb):
    i = pl.program_id(0)

    def zero_fill(e):
        return pltpu.make_async_copy(zbuf_ref, xg_ref.at[pl.ds(pl.multiple_of(zs_ref[e], bm), bm)], sem)

    @pl.when(i == 0)
    def _():
        zbuf_ref[...] = jnp.zeros(zbuf_ref.shape, zbuf_ref.dtype)
        for e in range(nz):
            pl.when(zs_ref[e] >= 0)(lambda e=e: zero_fill(e).start())
        for e in range(nz):
            pl.when(zs_ref[e] >= 0)(lambda e=e: zero_fill(e).wait())

    icp = pltpu.make_async_copy(idx_ref.at[0], idx_smem, isem)
    icp.start()
    icp.wait()

    def start(r, carry):
        for k in range(TOP_K):
            pltpu.make_async_copy(h_ref.at[pl.ds(r, 1)], xg_ref.at[pl.ds(idx_smem[k, r], 1)],
                                  sem).start(priority=k % 2)
        return carry

    lax.fori_loop(0, tb, start, 0, unroll=True)
    for k in range(TOP_K):
        pltpu.make_async_copy(h_ref, xg_ref.at[pl.ds(0, tb)], sem).wait()


def _moe_dispatch(zstart, dest3, h2, n_slots, bm):
    nt, _, tb = dest3.shape
    d = h2.shape[1]
    return pl.pallas_call(
        functools.partial(_dispatch_kernel, nz=zstart.shape[0], bm=bm, tb=tb),
        out_shape=jax.ShapeDtypeStruct((n_slots, d), h2.dtype),
        grid_spec=pltpu.PrefetchScalarGridSpec(
            num_scalar_prefetch=1,
            grid=(nt,),
            in_specs=[pl.BlockSpec((1, SUBLANES, tb), lambda i, zs: (i, 0, 0)),
                      pl.BlockSpec((tb, d), lambda i, zs: (i, 0))],
            out_specs=pl.BlockSpec(memory_space=pl.ANY),
            scratch_shapes=[pltpu.VMEM((bm, d), h2.dtype), pltpu.SMEM((SUBLANES, tb), I32),
                            pltpu.SemaphoreType.DMA, pltpu.SemaphoreType.DMA]),
        compiler_params=_cp("arbitrary"),
        name="moe_dispatch",
    )(zstart, dest3, h2)


def _weight_stream(w_refs, wbuf_ref, wb_refs, sem, cnt_ref, be_ref, nx_ref, used, tw):
    c = pl.program_id(0)
    b = pl.program_id(1)
    nc = pl.num_programs(0)

    def fetch(e, cc, slot):
        col = pl.ds(pl.multiple_of(cc * tw, tw), tw)
        return [pltpu.make_async_copy(w.at[e, :, col], wbuf_ref.at[slot, m], sem.at[slot])
                for m, w in enumerate(w_refs)]

    @pl.when(jnp.logical_and(c == 0, b == 0))
    def _():
        cnt_ref[0] = 0
        for cp in fetch(be_ref[0], 0, 0):
            cp.start(priority=WEIGHT_DMA_PRIORITY)

    changed = jnp.logical_or(b == 0, be_ref[b] != be_ref[jnp.maximum(b - 1, 0)])

    @pl.when(jnp.logical_and(used, changed))
    def _():
        slot = cnt_ref[0] & 1
        for cp in fetch(be_ref[b], c, slot):
            cp.wait()
        for m, wb in enumerate(wb_refs):
            wb[...] = wbuf_ref[slot, m].astype(BF16)
        nxt = nx_ref[b]
        same_chunk = nxt >= 0
        nxt_e = jnp.where(same_chunk, nxt, be_ref[0])
        nxt_c = jnp.where(same_chunk, c, c + 1)

        @pl.when(jnp.logical_or(same_chunk, c + 1 < nc))
        def _():
            for cp in fetch(nxt_e, nxt_c, 1 - slot):
                cp.start(priority=WEIGHT_DMA_PRIORITY)

        cnt_ref[0] = cnt_ref[0] + 1


def _gateup_kernel(be_ref, nx_ref, nu_ref, x_ref, bg_ref, bu_ref, wg_ref, wu_ref, o_ref,
                   wbuf_ref, wgb_ref, wub_ref, sem, cnt_ref, *, tf):
    used = pl.program_id(1) < nu_ref[0]
    _weight_stream([wg_ref, wu_ref], wbuf_ref, [wgb_ref, wub_ref], sem, cnt_ref, be_ref, nx_ref, used, tf)

    @pl.when(used)
    def _():
        x = _unpack_bf16_pairs(x_ref[...]).astype(BF16)
        gate = jnp.minimum(_dot(x, wgb_ref[...]) + bg_ref[0], SWIGLU_LIMIT)
        up = jnp.clip(_dot(x, wub_ref[...]) + bu_ref[0], -SWIGLU_LIMIT, SWIGLU_LIMIT)
        glu = gate * jax.nn.sigmoid(SWIGLU_ALPHA * gate)
        o_ref[...] = ((up + 1.0) * glu).astype(o_ref.dtype)

    @pl.when(jnp.logical_not(used))
    def _():
        o_ref[...] = jnp.zeros(o_ref.shape, o_ref.dtype)


def _moe_gateup(block_e, next_e, n_used, xg, w_gate, w_up, b_gate, b_up):
    ns, dh = xg.shape
    nexp, d, ff = w_gate.shape
    assert d == 2 * dh
    bm = MOE_BLOCK
    nb = ns // bm
    tf = _pick(ff, (1024, 512, 256, 128))
    bspec = pl.BlockSpec((1, 1, tf), lambda c, b, be, nx, nu: (be[b], 0, c))
    hbm = pl.BlockSpec(memory_space=pl.ANY)
    return pl.pallas_call(
        functools.partial(_gateup_kernel, tf=tf),
        out_shape=jax.ShapeDtypeStruct((ns, ff), BF16),
        grid_spec=pltpu.PrefetchScalarGridSpec(
            num_scalar_prefetch=3,
            grid=(ff // tf, nb),
            in_specs=[pl.BlockSpec((bm, dh), lambda c, b, be, nx, nu: (jnp.minimum(b, nu[0] - 1), 0)),
                      bspec, bspec, hbm, hbm],
            out_specs=pl.BlockSpec((bm, tf), lambda c, b, be, nx, nu: (b, c)),
            scratch_shapes=[pltpu.VMEM((2, 2, d, tf), F32), pltpu.VMEM((d, tf), BF16),
                            pltpu.VMEM((d, tf), BF16), pltpu.SemaphoreType.DMA((2,)),
                            pltpu.SMEM((1,), I32)]),
        compiler_params=_cp("arbitrary", "arbitrary"),
        name="moe_gateup",
    )(block_e, next_e, n_used, xg, b_gate.reshape(nexp, 1, ff), b_up.reshape(nexp, 1, ff), w_gate, w_up)


def _down_kernel(be_ref, nx_ref, nu_ref, g_ref, bd_ref, wd_ref, o_ref, wbuf_ref, wdb_ref, sem, cnt_ref, *, tn):
    used = pl.program_id(1) < nu_ref[0]
    _weight_stream([wd_ref], wbuf_ref, [wdb_ref], sem, cnt_ref, be_ref, nx_ref, used, tn)

    @pl.when(used)
    def _():
        o_ref[...] = _pack_bf16_pairs(_dot(g_ref[...], wdb_ref[...]) + bd_ref[0])

    @pl.when(jnp.logical_not(used))
    def _():
        o_ref[...] = jnp.zeros(o_ref.shape, o_ref.dtype)


def _moe_down(block_e, next_e, n_used, glu, w_down, b_down):
    ns, ff = glu.shape
    nexp, _, d = w_down.shape
    bm = MOE_BLOCK
    nb = ns // bm
    tn = d
    return pl.pallas_call(
        functools.partial(_down_kernel, tn=tn),
        out_shape=jax.ShapeDtypeStruct((ns, d // 2), jnp.uint32),
        grid_spec=pltpu.PrefetchScalarGridSpec(
            num_scalar_prefetch=3,
            grid=(d // tn, nb),
            in_specs=[pl.BlockSpec((bm, ff), lambda c, b, be, nx, nu: (jnp.minimum(b, nu[0] - 1), 0)),
                      pl.BlockSpec((1, 1, tn), lambda c, b, be, nx, nu: (be[b], 0, c)),
                      pl.BlockSpec(memory_space=pl.ANY)],
            out_specs=pl.BlockSpec((bm, tn // 2), lambda c, b, be, nx, nu: (b, c)),
            scratch_shapes=[pltpu.VMEM((2, 1, ff, tn), F32), pltpu.VMEM((ff, tn), BF16),
                            pltpu.SemaphoreType.DMA((2,)), pltpu.SMEM((1,), I32)]),
        compiler_params=_cp("arbitrary", "arbitrary"),
        name="moe_down",
    )(block_e, next_e, n_used, glu, b_down.reshape(nexp, 1, d), w_down)


def _combine_kernel(idx_ref, yb_ref, x1_ref, gt2_ref, p_ref, o_ref, buf_ref, idx_smem, isem, sem, *, tc):
    icp = pltpu.make_async_copy(idx_ref.at[0], idx_smem, isem)
    icp.start()
    icp.wait()

    def start(r, carry):
        for k in range(TOP_K):
            pltpu.make_async_copy(yb_ref.at[pl.ds(idx_smem[k, r], 1)],
                                  buf_ref.at[pl.ds(k * tc + r, 1)], sem).start(priority=k % 2)
        return carry

    lax.fori_loop(0, tc, start, 0, unroll=True)
    pltpu.make_async_copy(yb_ref.at[pl.ds(0, TOP_K * tc)], buf_ref, sem).wait()
    ff = p_ref[:, 0:1] * _unpack_bf16_pairs(buf_ref[0:tc, :])
    for k in range(1, TOP_K):
        ff = ff + p_ref[:, k:k + 1] * _unpack_bf16_pairs(buf_ref[k * tc:(k + 1) * tc, :])
    for ch in range(tc // CHUNK):
        rows = slice(ch * CHUNK, (ch + 1) * CHUNK)
        o_ref[rows, :] = x1_ref[rows, :] + gt2_ref[ch] * ff[rows, :]


def _moe_combine(dest3, yb, x1, gt2, top_p, row0, nrows):
    d = 2 * yb.shape[1]
    tc = dest3.shape[2]
    nc = tc // CHUNK
    rb0 = row0 // tc
    return pl.pallas_call(
        functools.partial(_combine_kernel, tc=tc),
        out_shape=jax.ShapeDtypeStruct((nrows, d), F32),
        grid=(nrows // tc,),
        in_specs=[pl.BlockSpec((1, SUBLANES, tc), lambda i: (rb0 + i, 0, 0)),
                  pl.BlockSpec(memory_space=pl.ANY),
                  pl.BlockSpec((tc, d), lambda i: (rb0 + i, 0)),
                  pl.BlockSpec((nc, 1, d), lambda i: (rb0 + i, 0, 0)),
                  pl.BlockSpec((tc, LANES), lambda i: (rb0 + i, 0))],
        out_specs=pl.BlockSpec((tc, d), lambda i: (i, 0)),
        scratch_shapes=[pltpu.VMEM((TOP_K * tc, d // 2), jnp.uint32), pltpu.SMEM((SUBLANES, tc), I32),
                        pltpu.SemaphoreType.DMA, pltpu.SemaphoreType.DMA],
        compiler_params=_cp("arbitrary"),
        name="moe_combine",
    )(dest3, yb, x1, gt2, top_p)


def _block_layout(meta, nexp, bm, n_blocks):
    counts = meta[:nexp, 0]
    start_blk = meta[:nexp, 1].astype(I32)
    nblk = jnp.ceil(counts * (1.0 / bm)).astype(I32)
    end_blk = start_blk + nblk
    n_used = jnp.sum(nblk)
    blk = jnp.minimum(jnp.arange(n_blocks), n_used - 1)
    block_e = jnp.minimum(jnp.sum(end_blk[None, :] <= blk[:, None], axis=1), nexp - 1).astype(I32)
    onehot_e = block_e[:, None] == jnp.arange(nexp)[None, :]
    run_end = jnp.sum(jnp.where(onehot_e, end_blk[None, :], 0), axis=1)
    at_end = run_end[:, None] == jnp.arange(n_blocks)[None, :]
    next_e = jnp.where(run_end < n_used, jnp.sum(jnp.where(at_end, block_e[None, :], 0), axis=1), -1).astype(I32)
    last = jnp.where(nblk > 0, (end_blk - 1) * bm, -1)
    spare = n_used + jnp.arange(nexp)
    spare = jnp.where(spare < n_blocks, spare * bm, -1)
    zstart = jnp.concatenate([last, spare]).astype(I32)
    return block_e, next_e, n_used.reshape(1).astype(I32), zstart


def _rope_tables(pos):
    half = ROPE_DIM // 2
    inv = ROPE_THETA ** (-jnp.arange(half, dtype=F32) / half)
    ang = pos.astype(F32)[:, None] * inv[None, :]
    z = jnp.zeros((pos.shape[0], LANES - ROPE_DIM), F32)
    cos, sin = jnp.cos(ang), jnp.sin(ang)
    return jnp.concatenate([cos, cos, z], axis=1), jnp.concatenate([sin, sin, z], axis=1)


def _rot_half_cols(w):
    half = ROPE_DIM // 2
    return jnp.concatenate([-w[..., half:], w[..., :half]], axis=-1)


def _layer(x_prompt, x_sample, past_lat, past_kr, ssm_s0, conv_s0, c_prompt, c_sample,
           w_ada, b_ada, g_norm1, w_in, g_cq, g_ckv, w_uq, w_uk, w_uv, g_qn, g_kn, conv_w, conv_b, dt_bias,
           a_log, d_skip, g_ssm, w_pa, w_pb, w_out, g_norm2, w_router, b_router, w_gate, b_gate, w_up, b_up,
           w_down, b_down):
    bp, sp, d = x_prompt.shape
    bs, ss, _ = x_sample.shape
    assert ss == CHUNK and sp % CHUNK == 0
    past = past_lat.shape[1]
    ql, kvl = g_cq.shape[-1], g_ckv.shape[-1]
    heads = w_uq.shape[1]
    ci = g_ssm.shape[-1]
    cc = conv_w.shape[-1]
    nh = ci // M_HEADDIM
    groups = (cc - ci) // (2 * D_STATE)
    nexp = w_router.shape[-1]
    tp, ts = bp * sp, bs * ss
    t = tp + ts
    ncp, ncs = tp // CHUNK, ts // CHUNK

    ada = _ada(jnp.concatenate([c_prompt, c_sample], axis=0), w_ada, b_ada)
    per_chunk = jnp.concatenate([jnp.repeat(ada[:bp], sp // CHUNK, axis=0), ada[bp:]], axis=0)
    sh1, sc1, gt1, sh2, sc2, gt2 = [m[:, None, :] for m in jnp.split(per_chunk, 6, axis=-1)]

    xp3 = x_prompt.reshape(ncp, CHUNK, d)
    xs3 = x_sample.reshape(ncs, CHUNK, d)
    h3 = _norm1(xp3, xs3, g_norm1, sc1, sh1)
    h_all = h3.reshape(t, d)

    o = 0
    w_cq = w_in[:, o:o + ql]; o += ql
    w_ckv = w_in[:, o:o + kvl]; o += kvl
    w_kr = w_in[:, o:o + ROPE_DIM]; o += ROPE_DIM
    w_z = w_in[:, o:o + ci]; o += ci
    w_xbc = w_in[:, o:o + cc]; o += cc
    w_dt = w_in[:, o:o + nh]; o += nh
    w_gab = w_in[:, o:o + 2 * d]
    zc = lambda n_: jnp.zeros((d, n_), F32)
    w_lat = jnp.concatenate([w_cq, w_ckv, w_kr, zc(LANES - ROPE_DIM), _rot_half_cols(w_kr),
                             zc(LANES - ROPE_DIM), w_dt, zc(LANES - nh)], axis=1).astype(BF16)

    pos = jnp.concatenate([jnp.tile(jnp.arange(sp), bp), jnp.tile(past + jnp.arange(ss), bs)])
    cos128, sin128 = _rope_tables(pos)

    cqn, lat_all, latk, kr_all, dt_all, dtt_all = _lat(
        h_all, w_lat, w_dt.T.astype(BF16), g_cq, g_ckv, cos128, sin128, dt_bias)
    z_all = _mm(h_all, w_z.astype(BF16), BF16, "proj_z")
    xbc_all = _mm(h_all, w_xbc.astype(BF16), BF16, "proj_xbc")
    gates = _mm(h_all, w_gab.astype(BF16), BF16, "proj_gates", act="sigmoid")

    tail = CONV_W - 1
    h_tail = jnp.concatenate([h3[:ncp].reshape(bp, sp, d)[:, sp - tail:, :].reshape(bp * tail, d),
                              h3[ncp:][:, CHUNK - tail:, :].reshape(bs * tail, d)], axis=0)
    conv_tail = _mm(h_tail, w_xbc.astype(BF16), F32, "proj_conv_tail")
    conv_p = conv_tail[:bp * tail].reshape(bp, tail, cc)
    conv_s = conv_tail[bp * tail:].reshape(bs, tail, cc)

    wq_a = jnp.concatenate([w_uq, jnp.zeros((ql, heads, HEAD_PAD - QK_DIM), F32)], axis=-1)
    wq_a = wq_a.reshape(ql, heads * HEAD_PAD).astype(BF16)
    wq_b = jnp.concatenate([_rot_half_cols(w_uq[..., NOPE_DIM:]),
                            jnp.zeros((ql, heads, LANES - ROPE_DIM), F32)], axis=-1)
    wq_b = wq_b.reshape(ql, heads * LANES).astype(BF16)
    gq = g_qn * g_kn * (ATTN_SCALE * math.log2(math.e))
    g_nope = gq[:NOPE_DIM].reshape(1, LANES)
    g_rope = jnp.concatenate([gq[NOPE_DIM:], jnp.zeros((LANES - ROPE_DIM,), F32)]).reshape(1, LANES)
    q_all = _qproj(cqn, wq_a, wq_b, cos128, sin128, g_nope, g_rope, heads)

    w_uk2 = w_uk.reshape(kvl, heads * NOPE_DIM)
    w_uv2 = w_uv.reshape(kvl, heads * V_DIM).astype(BF16)
    w_kv = jnp.concatenate([w_uk2.astype(BF16), w_uv2], axis=1)
    k_p, v_p = _kvproj(latk, w_kv, tp, heads, kvl)
    a_p = _attn_prompt(q_all, k_p, v_p, bp, sp, heads)
    a_s = _attn_sample(past_lat, past_kr, latk, q_all, w_uk2.T.astype(BF16), w_uv2, tp, heads)

    eh = jnp.repeat(jnp.eye(nh, dtype=BF16), M_HEADDIM, axis=1)
    dsk = jnp.repeat(d_skip, M_HEADDIM).reshape(1, ci)
    lcp = _pick(sp, (256, 128))
    assert sp % lcp == 0 and lcp % LANES == 0
    ncq = sp // lcp
    dt3_p = dt_all[:tp].reshape(bp * ncq, lcp, nh)
    dtt3_p = dtt_all[:, :tp].reshape(nh, bp * ncq, lcp).transpose(1, 0, 2)
    zpad = lambda a, axis: jnp.concatenate([a, jnp.zeros_like(a)], axis=axis)
    lcs = 2 * CHUNK
    dt3_s = zpad(dt_all[tp:].reshape(bs, CHUNK, nh), 1)
    dtt3_s = zpad(dtt_all[:, tp:].reshape(nh, bs, CHUNK).transpose(1, 0, 2), 2)
    pad_conv = lambda c0: jnp.concatenate(
        [jnp.zeros((c0.shape[0], SUBLANES - tail, cc), F32), c0], axis=1)
    m_p, ssm_p = _ssd(z_all, xbc_all, dt3_p, dtt3_p, jnp.zeros((bp, SUBLANES, cc), F32),
                      jnp.zeros((bp, ci, D_STATE), F32), conv_w, conv_b, a_log, dsk, g_ssm, eh,
                      0, bp, ncq, lcp, lcp, groups)
    m_s, ssm_s = _ssd(z_all, xbc_all, dt3_s, dtt3_s, pad_conv(conv_s0),
                      ssm_s0.reshape(bs, ci, D_STATE), conv_w, conv_b, a_log, dsk, g_ssm, eh,
                      tp, bs, 1, lcs, CHUNK, groups)

    mixed = _mix(a_p, a_s, m_p, m_s, gates, w_pa.astype(BF16), w_pb.astype(BF16))
    wr = jnp.concatenate([w_router, jnp.zeros((d, LANES - nexp), F32)], axis=1)
    wr_hi = wr.astype(BF16)
    wr_lo = (wr - wr_hi.astype(F32)).astype(BF16)
    br = jnp.concatenate([b_router, jnp.zeros((LANES - nexp,), F32)]).reshape(1, LANES)
    x1, h2, ti, tpr = _post(mixed, w_out.astype(BF16), xp3, xs3, gt1, sc2, sh2, g_norm2, wr_hi, wr_lo, br, nexp)

    assert nexp <= LANES
    bm = MOE_BLOCK
    n_blocks = -(-(t * TOP_K + nexp * (bm - 1)) // bm)
    tb = _pick(math.gcd(tp, ts), (256, 128))
    dest3, meta = _moe_slots(ti[:, :SUBLANES].T, bm, tb)
    block_e, next_e, n_used, zstart = _block_layout(meta, nexp, bm, n_blocks)
    xg = _moe_dispatch(zstart, dest3, h2, n_blocks * bm, bm)
    glu = _moe_gateup(block_e, next_e, n_used, xg, w_gate, w_up, b_gate, b_up)
    yb = _moe_down(block_e, next_e, n_used, glu, w_down, b_down)
    y_p = _moe_combine(dest3, yb, x1, gt2, tpr, 0, tp)
    y_s = _moe_combine(dest3, yb, x1, gt2, tpr, tp, ts)

    return (y_p.reshape(bp, sp, d), y_s.reshape(bs, ss, d),
            lat_all[:tp].reshape(bp, sp, kvl), kr_all[:tp].reshape(bp, sp, ROPE_DIM),
            ssm_p.reshape(bp, nh, M_HEADDIM, D_STATE), conv_p,
            lat_all[tp:].reshape(bs, ss, kvl), kr_all[tp:].reshape(bs, ss, ROPE_DIM),
            ssm_s.reshape(bs, nh, M_HEADDIM, D_STATE), conv_s)


def kernel(x_prompt, x_sample, cache_mla_latent, cache_mla_krope, state_ssm, state_conv, c_prompt, c_sample,
           w_ada, b_ada, g_norm1, w_in, g_cq, g_ckv, w_uq, w_uk, w_uv, g_qn, g_kn, conv_w, conv_b, dt_bias,
           a_log, d_skip, g_ssm, w_pa, w_pb, w_out, g_norm2, w_router, b_router, w_gate, b_gate, w_up, b_up,
           w_down, b_down):
    depth = w_ada.shape[0]
    assert depth == 1, "single-layer encoder"
    weights = (w_ada, b_ada, g_norm1, w_in, g_cq, g_ckv, w_uq, w_uk, w_uv, g_qn, g_kn, conv_w, conv_b, dt_bias,
               a_log, d_skip, g_ssm, w_pa, w_pb, w_out, g_norm2, w_router, b_router, w_gate, b_gate, w_up, b_up,
               w_down, b_down)
    outs = _layer(x_prompt, x_sample, cache_mla_latent[0], cache_mla_krope[0], state_ssm[0], state_conv[0],
                  c_prompt, c_sample, *[w[0] for w in weights])
    y_p, y_s = outs[0], outs[1]
    return (y_p, y_s) + tuple(o[None] for o in outs[2:6]) + tuple(o[None] for o in outs[6:])
```

```python
import functools
import math

import jax
import jax.numpy as jnp
from jax import lax
from jax.experimental import pallas as pl
from jax.experimental.pallas import tpu as pltpu

F32 = jnp.float32
BF16 = jnp.bfloat16
I32 = jnp.int32

CHUNK = 64
NOPE_DIM = 128
ROPE_DIM = 64
QK_DIM = NOPE_DIM + ROPE_DIM
V_DIM = 128
HEAD_PAD = 256
ROPE_THETA = 10000.0
ATTN_SCALE = QK_DIM ** -0.5
M_HEADDIM = 64
D_STATE = 128
CONV_W = 4
TOP_K = 4
SWIGLU_LIMIT = 7.0
SWIGLU_ALPHA = 1.702
EPS = 1e-6

LANES = 128
SUBLANES = 8
VMEM_LIMIT = 56 * 1024 * 1024

MOE_BLOCK = 256
WEIGHT_DMA_PRIORITY = 1
NEG_BIG = -1e30

_NT = (((1,), (1,)), ((), ()))
_TN = (((0,), (0,)), ((), ()))


def _cp(*sem):
    return pltpu.CompilerParams(dimension_semantics=sem, vmem_limit_bytes=VMEM_LIMIT)


def _pick(n, prefs):
    for p in prefs:
        if n % p == 0:
            return p
    return n


def _dot(a, b):
    return jnp.dot(a, b, preferred_element_type=F32)


def _split3(v):
    hi = v.astype(BF16)
    r1 = v - hi.astype(F32)
    mid = r1.astype(BF16)
    lo = (r1 - mid.astype(F32)).astype(BF16)
    return hi, mid, lo


def _silu(x):
    return x * jax.nn.sigmoid(x)


def _softplus(x):
    return jnp.maximum(x, 0.0) + jnp.log1p(jnp.exp(-jnp.abs(x)))


def _ada_kernel(c_ref, w_ref, b_ref, o_ref):
    s = _silu(c_ref[...]).astype(BF16)
    o_ref[...] = _dot(s, w_ref[...].astype(BF16)) + b_ref[...]


def _ada(c_all, w_ada, b_ada):
    r, d = c_all.shape
    n = w_ada.shape[1]
    tn = _pick(n, (1024, 512, 256, 128))
    return pl.pallas_call(
        _ada_kernel,
        out_shape=jax.ShapeDtypeStruct((r, n), F32),
        grid=(n // tn,),
        in_specs=[pl.BlockSpec((r, d), lambda j: (0, 0)),
                  pl.BlockSpec((d, tn), lambda j: (0, j)),
                  pl.BlockSpec((1, tn), lambda j: (0, j))],
        out_specs=pl.BlockSpec((r, tn), lambda j: (0, j)),
        compiler_params=_cp("arbitrary"),
        name="ada",
    )(c_all, w_ada, b_ada.reshape(1, n))


def _norm1_kernel(xp_ref, xs_ref, g_ref, sc_ref, sh_ref, h_ref, *, npb):
    i = pl.program_id(0)
    x = jnp.where(i < npb, xp_ref[...], xs_ref[...])
    xn = x * lax.rsqrt(jnp.mean(x * x, axis=-1, keepdims=True) + EPS)
    h_ref[...] = (xn * g_ref[...] * (1.0 + sc_ref[...]) + sh_ref[...]).astype(BF16)


def _norm1(xp3, xs3, g, sc, sh):
    ncp, _, d = xp3.shape
    ncs = xs3.shape[0]
    nch = ncp + ncs
    gc = _pick(math.gcd(ncp, ncs), (4, 2, 1))
    npb = ncp // gc
    blk = (gc, CHUNK, d)
    mod = pl.BlockSpec((gc, 1, d), lambda i: (i, 0, 0))
    return pl.pallas_call(
        functools.partial(_norm1_kernel, npb=npb),
        out_shape=jax.ShapeDtypeStruct((nch, CHUNK, d), BF16),
        grid=(nch // gc,),
        in_specs=[pl.BlockSpec(blk, lambda i: (jnp.minimum(i, npb - 1), 0, 0)),
                  pl.BlockSpec(blk, lambda i: (jnp.maximum(i - npb, 0), 0, 0)),
                  pl.BlockSpec((1, 1, d), lambda i: (0, 0, 0)),
                  mod, mod],
        out_specs=pl.BlockSpec(blk, lambda i: (i, 0, 0)),
        compiler_params=_cp("arbitrary"),
        name="norm1",
    )(xp3, xs3, g.reshape(1, 1, d), sc, sh)


def _mm_kernel(x_ref, w_ref, o_ref, *, act):
    acc = _dot(x_ref[...], w_ref[...])
    if act == "sigmoid":
        acc = jax.nn.sigmoid(acc)
    o_ref[...] = acc.astype(o_ref.dtype)


def _mm(x, w, out_dtype, name, act=None):
    m, k = x.shape
    n = w.shape[1]
    tm = _pick(m, (1024, 512, 256))
    tn = _pick(n, (1024, 512, 256, 128))
    return pl.pallas_call(
        functools.partial(_mm_kernel, act=act),
        out_shape=jax.ShapeDtypeStruct((m, n), out_dtype),
        grid=(m // tm, n // tn),
        in_specs=[pl.BlockSpec((tm, k), lambda i, j: (i, 0)),
                  pl.BlockSpec((k, tn), lambda i, j: (0, j))],
        out_specs=pl.BlockSpec((tm, tn), lambda i, j: (i, j)),
        compiler_params=_cp("arbitrary", "arbitrary"),
        name=name,
    )(x, w)


def _lat_kernel(h_ref, w_ref, wdt_ref, gcq_ref, gckv_ref, cos_ref, sin_ref, dtb_ref, dtbc_ref,
                cqn_ref, lat_ref, latk_ref, kr_ref, dt_ref, dtt_ref, *, ql, kvl):
    h = h_ref[...]
    acc = _dot(h, w_ref[...])
    cq = acc[:, :ql]
    cqn = cq * lax.rsqrt(jnp.mean(cq * cq, axis=-1, keepdims=True) + EPS) * gcq_ref[...]
    cqn_ref[...] = cqn.astype(BF16)
    ckv = acc[:, ql:ql + kvl]
    lat = ckv * lax.rsqrt(jnp.mean(ckv * ckv, axis=-1, keepdims=True) + EPS) * gckv_ref[...]
    lat_ref[...] = lat
    o = ql + kvl
    kr128 = acc[:, o:o + LANES] * cos_ref[...] + acc[:, o + LANES:o + 2 * LANES] * sin_ref[...]
    kr_ref[...] = kr128[:, :ROPE_DIM]
    latk_ref[:, :kvl] = lat.astype(BF16)
    latk_ref[:, kvl:] = kr128.astype(BF16)
    nh = dt_ref.shape[-1]
    dt_ref[...] = _softplus(acc[:, o + 2 * LANES:o + 2 * LANES + nh] + dtb_ref[...])
    dtt = lax.dot_general(wdt_ref[...], h, _NT, preferred_element_type=F32)
    dtt_ref[...] = _softplus(dtt + dtbc_ref[...])


def _lat(h_all, w_lat, w_dt_t, g_cq, g_ckv, cos128, sin128, dt_bias):
    t, d = h_all.shape
    ql, kvl = g_cq.shape[-1], g_ckv.shape[-1]
    nh = dt_bias.shape[-1]
    nl = w_lat.shape[1]
    tm = _pick(t, (512, 256, 128))
    row = lambda w: pl.BlockSpec((tm, w), lambda i: (i, 0))
    const = lambda a, b: pl.BlockSpec((a, b), lambda i: (0, 0))
    return pl.pallas_call(
        functools.partial(_lat_kernel, ql=ql, kvl=kvl),
        out_shape=(jax.ShapeDtypeStruct((t, ql), BF16),
                   jax.ShapeDtypeStruct((t, kvl), F32),
                   jax.ShapeDtypeStruct((t, kvl + LANES), BF16),
                   jax.ShapeDtypeStruct((t, ROPE_DIM), F32),
                   jax.ShapeDtypeStruct((t, nh), F32),
                   jax.ShapeDtypeStruct((nh, t), F32)),
        grid=(t // tm,),
        in_specs=[row(d), const(d, nl), const(nh, d), const(1, ql), const(1, kvl),
                  row(LANES), row(LANES), const(1, nh), const(nh, 1)],
        out_specs=(row(ql), row(kvl), row(kvl + LANES), row(ROPE_DIM), row(nh),
                   pl.BlockSpec((nh, tm), lambda i: (0, i))),
        compiler_params=_cp("arbitrary"),
        name="latent_proj",
    )(h_all, w_lat, w_dt_t, g_cq.reshape(1, ql), g_ckv.reshape(1, kvl), cos128, sin128,
      dt_bias.reshape(1, nh), dt_bias.reshape(nh, 1))


def _q_kernel(c_ref, wa_ref, wb_ref, cos_ref, sin_ref, gn_ref, gr_ref, o_ref, *, heads):
    c = c_ref[...]
    a = _dot(c, wa_ref[...])
    b = _dot(c, wb_ref[...])
    cos, sin = cos_ref[...], sin_ref[...]
    for h in range(heads):
        nope = a[:, h * HEAD_PAD:h * HEAD_PAD + LANES]
        rope = a[:, h * HEAD_PAD + LANES:(h + 1) * HEAD_PAD] * cos + b[:, h * LANES:(h + 1) * LANES] * sin
        ss = (jnp.sum(nope * nope, axis=-1, keepdims=True)
              + jnp.sum(rope * rope, axis=-1, keepdims=True)) * (1.0 / QK_DIM)
        r = lax.rsqrt(ss + EPS)
        o_ref[:, h * HEAD_PAD:h * HEAD_PAD + LANES] = (nope * r * gn_ref[...]).astype(BF16)
        o_ref[:, h * HEAD_PAD + LANES:(h + 1) * HEAD_PAD] = (rope * r * gr_ref[...]).astype(BF16)


def _qproj(cqn, wq_a, wq_b, cos128, sin128, g_nope, g_rope, heads):
    t, ql = cqn.shape
    tm = _pick(t, (256, 128))
    row = lambda w: pl.BlockSpec((tm, w), lambda i: (i, 0))
    const = lambda a, b: pl.BlockSpec((a, b), lambda i: (0, 0))
    return pl.pallas_call(
        functools.partial(_q_kernel, heads=heads),
        out_shape=jax.ShapeDtypeStruct((t, heads * HEAD_PAD), BF16),
        grid=(t // tm,),
        in_specs=[row(ql), const(ql, heads * HEAD_PAD), const(ql, heads * LANES),
                  row(LANES), row(LANES), const(1, LANES), const(1, LANES)],
        out_specs=row(heads * HEAD_PAD),
        compiler_params=_cp("arbitrary"),
        name="q_proj",
    )(cqn, wq_a, wq_b, cos128, sin128, g_nope, g_rope)


def _kv_kernel(lat_ref, kr_ref, w_ref, k_ref, v_ref, *, heads):
    acc = _dot(lat_ref[...], w_ref[...])
    kr = kr_ref[...].astype(F32)
    kr2 = jnp.sum(kr * kr, axis=-1, keepdims=True)
    for h in range(heads):
        kn = acc[:, h * LANES:(h + 1) * LANES]
        ss = (jnp.sum(kn * kn, axis=-1, keepdims=True) + kr2) * (1.0 / QK_DIM)
        r = lax.rsqrt(ss + EPS)
        k_ref[:, h * HEAD_PAD:h * HEAD_PAD + LANES] = (kn * r).astype(BF16)
        k_ref[:, h * HEAD_PAD + LANES:(h + 1) * HEAD_PAD] = (kr * r).astype(BF16)
    v_ref[...] = acc[:, heads * LANES:].astype(BF16)


def _kvproj(latk, w_kv, tp, heads, kvl):
    tm = _pick(tp, (256, 128))
    return pl.pallas_call(
        functools.partial(_kv_kernel, heads=heads),
        out_shape=(jax.ShapeDtypeStruct((tp, heads * HEAD_PAD), BF16),
                   jax.ShapeDtypeStruct((tp, heads * V_DIM), BF16)),
        grid=(tp // tm,),
        in_specs=[pl.BlockSpec((tm, kvl), lambda i: (i, 0)),
                  pl.BlockSpec((tm, LANES), lambda i: (i, kvl // LANES)),
                  pl.BlockSpec((kvl, 2 * heads * LANES), lambda i: (0, 0))],
        out_specs=(pl.BlockSpec((tm, heads * HEAD_PAD), lambda i: (i, 0)),
                   pl.BlockSpec((tm, heads * V_DIM), lambda i: (i, 0))),
        compiler_params=_cp("arbitrary"),
        name="kv_proj",
    )(latk, latk, w_kv)


def _attn_p_kernel(q_ref, k_ref, v_ref, o_ref, m_ref, l_ref, acc_ref, *, tq, hg):
    qi = pl.program_id(2)
    m_ref[...] = jnp.full(m_ref.shape, -jnp.inf, F32)
    l_ref[...] = jnp.zeros(l_ref.shape, F32)
    acc_ref[...] = jnp.zeros(acc_ref.shape, F32)

    def tile(j, masked):
        ks = pl.multiple_of(j * tq, tq)
        for g in range(hg):
            q = q_ref[:, g * HEAD_PAD:(g + 1) * HEAD_PAD]
            k = k_ref[pl.ds(ks, tq), g * HEAD_PAD:(g + 1) * HEAD_PAD]
            v = v_ref[pl.ds(ks, tq), g * V_DIM:(g + 1) * V_DIM]
            s = lax.dot_general(q, k, _NT, preferred_element_type=F32)
            if masked:
                rc = lax.broadcasted_iota(I32, (tq, tq), 0) // CHUNK
                cc = lax.broadcasted_iota(I32, (tq, tq), 1) // CHUNK
                s = jnp.where(cc <= rc, s, -jnp.inf)
            m_prev = m_ref[g]
            m_new = jnp.maximum(m_prev, jnp.max(s, axis=-1, keepdims=True))
            alpha = jnp.exp2(m_prev - m_new)
            p = jnp.exp2(s - jnp.tile(m_new, (1, tq // LANES)))
            l_ref[g] = alpha * l_ref[g] + jnp.sum(p, axis=-1, keepdims=True)
            acc_ref[g] = alpha * acc_ref[g] + _dot(p.astype(BF16), v)
            m_ref[g] = m_new

    def body(j, carry):
        tile(j, False)
        return carry

    lax.fori_loop(0, qi, body, 0)
    tile(qi, True)
    for g in range(hg):
        o_ref[:, g * V_DIM:(g + 1) * V_DIM] = (acc_ref[g] / l_ref[g]).astype(o_ref.dtype)


def _attn_prompt(q_all, k_p, v_p, batch, seq, heads):
    tq = _pick(seq, (512, 256, 128, 64))
    nq = seq // tq
    hg = _pick(heads, (4, 2, 1))
    return pl.pallas_call(
        functools.partial(_attn_p_kernel, tq=tq, hg=hg),
        out_shape=jax.ShapeDtypeStruct((batch * seq, heads * V_DIM), BF16),
        grid=(batch, heads // hg, nq),
        in_specs=[pl.BlockSpec((tq, hg * HEAD_PAD), lambda b, h, qi: (b * nq + qi, h)),
                  pl.BlockSpec((seq, hg * HEAD_PAD), lambda b, h, qi: (b, h)),
                  pl.BlockSpec((seq, hg * V_DIM), lambda b, h, qi: (b, h))],
        out_specs=pl.BlockSpec((tq, hg * V_DIM), lambda b, h, qi: (b * nq + qi, h)),
        scratch_shapes=[pltpu.VMEM((hg, tq, LANES), F32), pltpu.VMEM((hg, tq, LANES), F32),
                        pltpu.VMEM((hg, tq, V_DIM), F32)],
        compiler_params=_cp("arbitrary", "arbitrary", "arbitrary"),
        name="attn_prompt",
    )(q_all, k_p, v_p)


def _attn_s_kernel(pl_ref, pk_ref, nl_ref, q_ref, wukt_ref, wuv_ref, o_ref,
                   qabs_ref, qr_ref, m_ref, l_ref, acc_ref, *, heads, kvl, nkb):
    kb = pl.program_id(1)

    @pl.when(kb == 0)
    def _():
        for h in range(heads):
            qn = q_ref[:, h * HEAD_PAD:h * HEAD_PAD + LANES]
            qabs_ref[h * CHUNK:(h + 1) * CHUNK, :] = _dot(
                qn, wukt_ref[h * LANES:(h + 1) * LANES, :]).astype(BF16)
            qr_ref[h * CHUNK:(h + 1) * CHUNK, :] = q_ref[:, h * HEAD_PAD + LANES:(h + 1) * HEAD_PAD]
        m_ref[...] = jnp.full(m_ref.shape, -jnp.inf, F32)
        l_ref[...] = jnp.zeros(l_ref.shape, F32)
        acc_ref[...] = jnp.zeros(acc_ref.shape, F32)

    def block(xl, krf):
        rtop = lax.dot_general(wukt_ref[...], xl, _NT, preferred_element_type=F32)
        sq = krf * krf
        sq_hi = sq.astype(BF16)
        sq_lo = (sq - sq_hi.astype(F32)).astype(BF16)
        ones = jnp.ones((SUBLANES, ROPE_DIM), BF16)
        kr2 = (lax.dot_general(ones, sq_hi, _NT, preferred_element_type=F32)
               + lax.dot_general(ones, sq_lo, _NT, preferred_element_type=F32))[0:1, :]
        s = (lax.dot_general(qabs_ref[...], xl, _NT, preferred_element_type=F32)
             + lax.dot_general(qr_ref[:, :ROPE_DIM], krf.astype(BF16), _NT,
                               preferred_element_type=F32))
        parts = []
        for h in range(heads):
            rt = rtop[h * LANES:(h + 1) * LANES, :]
            kn2 = jnp.sum(rt * rt, axis=0, keepdims=True)
            r = lax.rsqrt((kn2 + kr2) * (1.0 / QK_DIM) + EPS)
            parts.append(s[h * CHUNK:(h + 1) * CHUNK, :] * r)
        s = jnp.concatenate(parts, axis=0)
        n = s.shape[1]
        m_prev = m_ref[...]
        m_new = jnp.maximum(m_prev, jnp.max(s, axis=-1, keepdims=True))
        alpha = jnp.exp2(m_prev - m_new)
        m_wide = jnp.tile(m_new, (1, n // LANES)) if n >= LANES else m_new[:, :n]
        p = jnp.exp2(s - m_wide)
        l_ref[...] = alpha * l_ref[...] + jnp.sum(p, axis=-1, keepdims=True)
        acc_ref[...] = jnp.tile(alpha, (1, kvl // LANES)) * acc_ref[...] + _dot(p.astype(BF16), xl)
        m_ref[...] = m_new

    @pl.when(kb < nkb)
    def _():
        block(pl_ref[0].astype(BF16), pk_ref[0])

    @pl.when(kb == nkb)
    def _():
        block(nl_ref[:, :kvl], nl_ref[:, kvl:kvl + ROPE_DIM].astype(F32))
        o = (acc_ref[...] / jnp.tile(l_ref[...], (1, kvl // LANES))).astype(BF16)
        for h in range(heads):
            o_ref[:, h * V_DIM:(h + 1) * V_DIM] = _dot(
                o[h * CHUNK:(h + 1) * CHUNK, :], wuv_ref[:, h * V_DIM:(h + 1) * V_DIM]).astype(o_ref.dtype)


def _attn_sample(past_lat, past_kr, latk, q_all, w_uk_t, w_uv2, tp, heads):
    bs, past, kvl = past_lat.shape
    tk = _pick(past, (512, 256, 128))
    nkb = past // tk
    c0 = tp // CHUNK
    pidx = lambda b, kb: (b, jnp.minimum(kb, nkb - 1), 0)
    hq = heads * CHUNK
    return pl.pallas_call(
        functools.partial(_attn_s_kernel, heads=heads, kvl=kvl, nkb=nkb),
        out_shape=jax.ShapeDtypeStruct((bs * CHUNK, heads * V_DIM), BF16),
        grid=(bs, nkb + 1),
        in_specs=[pl.BlockSpec((1, tk, kvl), pidx),
                  pl.BlockSpec((1, tk, ROPE_DIM), pidx),
                  pl.BlockSpec((CHUNK, kvl + LANES), lambda b, kb: (c0 + b, 0)),
                  pl.BlockSpec((CHUNK, heads * HEAD_PAD), lambda b, kb: (c0 + b, 0)),
                  pl.BlockSpec((heads * LANES, kvl), lambda b, kb: (0, 0)),
                  pl.BlockSpec((kvl, heads * V_DIM), lambda b, kb: (0, 0))],
        out_specs=pl.BlockSpec((CHUNK, heads * V_DIM), lambda b, kb: (b, 0)),
        scratch_shapes=[pltpu.VMEM((hq, kvl), BF16), pltpu.VMEM((hq, LANES), BF16),
                        pltpu.VMEM((hq, LANES), F32), pltpu.VMEM((hq, LANES), F32),
                        pltpu.VMEM((hq, kvl), F32)],
        compiler_params=_cp("arbitrary", "arbitrary"),
        name="attn_sample",
    )(past_lat, past_kr, latk, q_all, w_uk_t, w_uv2)


def _ssd_kernel(z_ref, x_ref, dt_ref, dtt_ref, conv0_ref, ssm0_ref, cw_ref, cb_ref, alr_ref, alc_ref,
                dsk_ref, gs_ref, eh_ref, o_ref, st_out_ref, xs_ref, st_ref,
                *, lc, lr, nh, groups):
    c = pl.program_id(1)
    p = M_HEADDIM
    n = D_STATE
    ci = nh * p
    k8 = nh // groups
    gw = k8 * p

    @pl.when(c == 0)
    def _():
        xs_ref[0:SUBLANES, :] = conv0_ref[0]
        st_ref[...] = ssm0_ref[0]

    xs_ref[SUBLANES:SUBLANES + lr, :] = x_ref[...].astype(F32)
    if lc > lr:
        xs_ref[SUBLANES + lr:SUBLANES + lc, :] = jnp.zeros((lc - lr, xs_ref.shape[1]), F32)

    def conv(lo, hi):
        u = xs_ref[SUBLANES - 3:SUBLANES - 3 + lc, lo:hi] * cw_ref[0:1, lo:hi]
        for tap in range(1, CONV_W):
            u = u + xs_ref[SUBLANES - 3 + tap:SUBLANES - 3 + tap + lc, lo:hi] * cw_ref[tap:tap + 1, lo:hi]
        return _silu(u + cb_ref[:, lo:hi])

    dt = dt_ref[0]
    dtt = dtt_ref[0]
    a_row = -jnp.exp(alr_ref[...])
    a_col = -jnp.exp(alc_ref[...])
    ri = lax.broadcasted_iota(I32, (lc, lc), 0)
    cidx = lax.broadcasted_iota(I32, (lc, lc), 1)
    tri = ri >= cidx
    tril = jnp.where(tri, 1.0, 0.0).astype(BF16)
    triu = jnp.where(ri <= cidx, 1.0, 0.0).astype(BF16)
    cs = sum(_dot(tril, piece) for piece in _split3(dt * a_row))
    cst = sum(_dot(piece, triu) for piece in _split3(dtt * a_col))
    exp_cs = jnp.exp(cs)
    w_end = jnp.exp(cs[lc - 1:lc, :] - cs)
    stacked = jnp.concatenate([dt, exp_cs, w_end], axis=0)
    eh = eh_ref[...]
    expanded = sum(_dot(piece, eh) for piece in _split3(stacked))
    dt_e, ecs_e, wend_e = expanded[:lc], expanded[lc:2 * lc], expanded[2 * lc:]
    cdec = jnp.exp(cst[:, lc - 1:lc])
    lane_lo = lax.broadcasted_iota(I32, (lc, LANES), 1) < p

    for g in range(groups):
        gs = slice(g * gw, (g + 1) * gw)
        xg = conv(g * gw, (g + 1) * gw)
        bg = conv(ci + g * n, ci + (g + 1) * n).astype(BF16)
        cg = conv(ci + groups * n + g * n, ci + groups * n + (g + 1) * n).astype(BF16)
        cbm = lax.dot_general(cg, bg, _NT, preferred_element_type=F32)
        xdt = xg * dt_e[:, gs]
        xdt_b = xdt.astype(BF16)
        pairs = []
        for q in range(k8 // 2):
            x2 = xdt_b[:, q * LANES:(q + 1) * LANES]
            ys = []
            for s in range(2):
                h = g * k8 + 2 * q + s
                seg = cs[:, h:h + 1] - cst[h:h + 1, :]
                dec = jnp.exp(jnp.where(tri, seg, -jnp.inf))
                ys.append(_dot((cbm * dec).astype(BF16), x2))
            pairs.append(jnp.where(lane_lo, ys[0], ys[1]))
        y_diag = jnp.concatenate(pairs, axis=1)
        sg = st_ref[g * gw:(g + 1) * gw, :]
        y_off = lax.dot_general(cg, sg.astype(BF16), _NT, preferred_element_type=F32) * ecs_e[:, gs]
        y = y_diag + y_off + xg * dsk_ref[:, gs]
        xw = (xdt * wend_e[:, gs]).astype(BF16)
        upd = lax.dot_general(xw, bg, _TN, preferred_element_type=F32)
        for k in range(k8):
            h = g * k8 + k
            rows = slice(g * gw + k * p, g * gw + (k + 1) * p)
            st_ref[rows, :] = st_ref[rows, :] * cdec[h:h + 1, :] + upd[k * p:(k + 1) * p, :]
        zg = z_ref[:, gs].astype(F32)
        u2 = y[:lr] * _silu(zg)
        ms = jnp.mean(u2 * u2, axis=-1, keepdims=True)
        o_ref[:, gs] = (u2 * lax.rsqrt(ms + EPS) * gs_ref[:, gs]).astype(o_ref.dtype)

    xs_ref[0:SUBLANES, :] = xs_ref[lr:lr + SUBLANES, :]

    @pl.when(c == pl.num_programs(1) - 1)
    def _():
        st_out_ref[0] = st_ref[...]


def _ssd(z_all, xbc_all, dt3, dtt3, conv0p, ssm0, conv_w, conv_b, a_log, dsk, g_ssm, eh,
         row0, nseq, nchunk, lc, lr, groups):
    ci = z_all.shape[1]
    cc = xbc_all.shape[1]
    nh = ci // M_HEADDIM
    rb0 = row0 // lr
    rowblk = lambda w: pl.BlockSpec((lr, w), lambda b, c: (rb0 + b * nchunk + c, 0))
    seq3 = lambda a, b_: pl.BlockSpec((1, a, b_), lambda b, c: (b * nchunk + c, 0, 0))
    perb = lambda a, b_: pl.BlockSpec((1, a, b_), lambda b, c: (b, 0, 0))
    const = lambda a, b_: pl.BlockSpec((a, b_), lambda b, c: (0, 0))
    return pl.pallas_call(
        functools.partial(_ssd_kernel, lc=lc, lr=lr, nh=nh, groups=groups),
        out_shape=(jax.ShapeDtypeStruct((nseq * nchunk * lr, ci), BF16),
                   jax.ShapeDtypeStruct((nseq, ci, D_STATE), F32)),
        grid=(nseq, nchunk),
        in_specs=[rowblk(ci), rowblk(cc), seq3(lc, nh), seq3(nh, lc),
                  perb(SUBLANES, cc), perb(ci, D_STATE),
                  const(CONV_W, cc), const(1, cc), const(1, nh), const(nh, 1),
                  const(1, ci), const(1, ci), const(nh, ci)],
        out_specs=(pl.BlockSpec((lr, ci), lambda b, c: (b * nchunk + c, 0)),
                   perb(ci, D_STATE)),
        scratch_shapes=[pltpu.VMEM((lc + SUBLANES, cc), F32), pltpu.VMEM((ci, D_STATE), F32)],
        compiler_params=_cp("arbitrary", "arbitrary"),
        name="ssd",
    )(z_all, xbc_all, dt3, dtt3, conv0p, ssm0, conv_w, conv_b.reshape(1, cc),
      a_log.reshape(1, nh), a_log.reshape(nh, 1), dsk, g_ssm.reshape(1, ci), eh)


def _mix_kernel(ap_ref, as_ref, mp_ref, ms_ref, ga_ref, gb_ref, wpa_ref, wpb_ref, o_ref, *, npb):
    i = pl.program_id(1)
    a = jnp.where(i < npb, ap_ref[...], as_ref[...])
    m = jnp.where(i < npb, mp_ref[...], ms_ref[...])
    pa = _dot(a, wpa_ref[...])
    pb = _dot(m, wpb_ref[...])
    o_ref[...] = (ga_ref[...].astype(F32) * pa + gb_ref[...].astype(F32) * pb).astype(o_ref.dtype)


def _mix(a_p, a_s, m_p, m_s, gates, w_pa, w_pb):
    tp, hv = a_p.shape
    ts = a_s.shape[0]
    ci = m_p.shape[1]
    d = w_pa.shape[1]
    tm = _pick(math.gcd(tp, ts), (256, 128, 64))
    tn = _pick(d, (1024, 512, 256, 128))
    npb = tp // tm
    nj = d // tn
    pidx = lambda j, i: (jnp.minimum(i, npb - 1), 0)
    sidx = lambda j, i: (jnp.maximum(i - npb, 0), 0)
    return pl.pallas_call(
        functools.partial(_mix_kernel, npb=npb),
        out_shape=jax.ShapeDtypeStruct((tp + ts, d), BF16),
        grid=(nj, (tp + ts) // tm),
        in_specs=[pl.BlockSpec((tm, hv), pidx), pl.BlockSpec((tm, hv), sidx),
                  pl.BlockSpec((tm, ci), pidx), pl.BlockSpec((tm, ci), sidx),
                  pl.BlockSpec((tm, tn), lambda j, i: (i, j)),
                  pl.BlockSpec((tm, tn), lambda j, i: (i, nj + j)),
                  pl.BlockSpec((hv, tn), lambda j, i: (0, j)),
                  pl.BlockSpec((ci, tn), lambda j, i: (0, j))],
        out_specs=pl.BlockSpec((tm, tn), lambda j, i: (i, j)),
        compiler_params=_cp("arbitrary", "arbitrary"),
        name="branch_mix",
    )(a_p, a_s, m_p, m_s, gates, gates, w_pa, w_pb)


def _pack_bf16_pairs(x):
    n = x.shape[1] // 2
    lo = pltpu.bitcast(x[:, :n].astype(BF16).astype(F32), jnp.uint32)
    hi = pltpu.bitcast(x[:, n:].astype(BF16).astype(F32), jnp.uint32)
    return hi | (lo >> 16)


def _unpack_bf16_pairs(u):
    lo = pltpu.bitcast(u << 16, F32)
    hi = pltpu.bitcast(u & jnp.uint32(0xFFFF0000), F32)
    return jnp.concatenate([lo, hi], axis=1)


def _post_kernel(mx_ref, wout_ref, xp_ref, xs_ref, gt1_ref, sc2_ref, sh2_ref, g2_ref, wrh_ref, wrl_ref, br_ref,
                 x1_ref, h2p_ref, ti_ref, tp_ref, h2_ref, *, nexp, npb):
    is_prompt = pl.program_id(0) < npb
    o = _dot(mx_ref[...], wout_ref[...])
    tm = o.shape[0]
    for ch in range(tm // CHUNK):
        rows = slice(ch * CHUNK, (ch + 1) * CHUNK)
        x1 = jnp.where(is_prompt, xp_ref[ch], xs_ref[ch]) + gt1_ref[ch] * o[rows, :]
        x1_ref[rows, :] = x1
        xn = x1 * lax.rsqrt(jnp.mean(x1 * x1, axis=-1, keepdims=True) + EPS)
        h2_ref[rows, :] = xn * g2_ref[...] * (1.0 + sc2_ref[ch]) + sh2_ref[ch]
    h2 = h2_ref[...]
    h2p_ref[...] = _pack_bf16_pairs(h2)
    hh = h2.astype(BF16)
    hl = (h2 - hh.astype(F32)).astype(BF16)
    logits = _dot(hh, wrh_ref[...]) + _dot(hh, wrl_ref[...]) + _dot(hl, wrh_ref[...]) + br_ref[...]
    lane = lax.broadcasted_iota(I32, logits.shape, 1)
    logits = jnp.where(lane < nexp, logits, NEG_BIG)
    vals, idxs = [], []
    for _ in range(TOP_K):
        m = jnp.max(logits, axis=-1, keepdims=True)
        idx = jnp.min(jnp.where(logits == m, lane, LANES), axis=-1, keepdims=True)
        vals.append(m)
        idxs.append(idx)
        logits = jnp.where(lane == idx, 2.0 * NEG_BIG, logits)
    es = [jnp.exp(v - vals[0]) for v in vals]
    den = es[0]
    for e in es[1:]:
        den = den + e
    ti = jnp.zeros(lane.shape, I32)
    tpv = jnp.zeros(lane.shape, F32)
    for k in range(TOP_K):
        ti = jnp.where(lane == k, idxs[k], ti)
        tpv = jnp.where(lane == k, es[k] / den, tpv)
    ti_ref[...] = ti
    tp_ref[...] = tpv


def _post(mixed, w_out, xp3, xs3, gt1, sc2, sh2, g2, wr_hi, wr_lo, br, nexp):
    t, d = mixed.shape
    ncp, ncs = xp3.shape[0], xs3.shape[0]
    nc = _pick(math.gcd(ncp, ncs), (4, 2, 1))
    tm = nc * CHUNK
    npb = ncp // nc
    row = lambda w: pl.BlockSpec((tm, w), lambda i: (i, 0))
    mod = pl.BlockSpec((nc, 1, d), lambda i: (i, 0, 0))
    const = lambda a, b: pl.BlockSpec((a, b), lambda i: (0, 0))
    return pl.pallas_call(
        functools.partial(_post_kernel, nexp=nexp, npb=npb),
        out_shape=(jax.ShapeDtypeStruct((t, d), F32), jax.ShapeDtypeStruct((t, d // 2), jnp.uint32),
                   jax.ShapeDtypeStruct((t, LANES), I32), jax.ShapeDtypeStruct((t, LANES), F32)),
        grid=(t // tm,),
        in_specs=[row(d), const(d, d),
                  pl.BlockSpec((nc, CHUNK, d), lambda i: (jnp.minimum(i, npb - 1), 0, 0)),
                  pl.BlockSpec((nc, CHUNK, d), lambda i: (jnp.maximum(i - npb, 0), 0, 0)),
                  mod, mod, mod, const(1, d),
                  const(d, LANES), const(d, LANES), const(1, LANES)],
        out_specs=(row(d), row(d // 2), row(LANES), row(LANES)),
        scratch_shapes=[pltpu.VMEM((tm, d), F32)],
        compiler_params=_cp("arbitrary"),
        name="post_mix",
    )(mixed, w_out, xp3, xs3, gt1, sc2, sh2, g2.reshape(1, d), wr_hi, wr_lo, br)


def _slots_kernel(ti_ref, dest_ref, meta_ref, run_ref, *, bm, tb):
    ph = pl.program_id(0)
    i = pl.program_id(1)
    eid = lax.broadcasted_iota(I32, (LANES, tb), 0)
    onehots = [jnp.where(eid == ti_ref[k:k + 1, :], 1.0, 0.0) for k in range(TOP_K)]
    osum = onehots[0]
    for oh in onehots[1:]:
        osum = osum + oh
    blk_cnt = jnp.sum(osum, axis=1, keepdims=True)

    @pl.when(jnp.logical_and(ph == 0, i == 0))
    def _():
        run_ref[...] = jnp.zeros(run_ref.shape, F32)

    @pl.when(ph == 0)
    def _():
        run_ref[...] = run_ref[...] + blk_cnt

    @pl.when(jnp.logical_and(ph == 1, i == 0))
    def _():
        cnt = run_ref[...]
        nblk = jnp.ceil(cnt * (1.0 / bm))
        r = lax.broadcasted_iota(I32, (LANES, LANES), 0)
        c = lax.broadcasted_iota(I32, (LANES, LANES), 1)
        lstrict = jnp.where(c < r, 1.0, 0.0).astype(BF16)
        start_blk = _dot(lstrict, jnp.broadcast_to(nblk, (LANES, LANES)).astype(BF16))
        lane = lax.broadcasted_iota(I32, (LANES, LANES), 1)
        meta_ref[...] = jnp.where(lane == 0, cnt, jnp.where(lane == 1, start_blk, 0.0))
        run_ref[...] = start_blk[:, 0:1] * float(bm)

    @pl.when(ph == 1)
    def _():
        r = lax.broadcasted_iota(I32, (tb, tb), 0)
        c = lax.broadcasted_iota(I32, (tb, tb), 1)
        ustrict = jnp.where(r < c, 1.0, 0.0).astype(BF16)
        base = run_ref[...] + _dot(osum.astype(BF16), ustrict)
        for k in range(TOP_K):
            dest_ref[0, k:k + 1, :] = jnp.sum(onehots[k] * base, axis=0, keepdims=True).astype(I32)
        dest_ref[0, TOP_K:, :] = jnp.zeros((SUBLANES - TOP_K, tb), I32)
        run_ref[...] = run_ref[...] + blk_cnt


def _moe_slots(ti_t, bm, tb):
    t = ti_t.shape[1]
    nt = t // tb
    return pl.pallas_call(
        functools.partial(_slots_kernel, bm=bm, tb=tb),
        out_shape=(jax.ShapeDtypeStruct((nt, SUBLANES, tb), I32),
                   jax.ShapeDtypeStruct((LANES, LANES), F32)),
        grid=(2, nt),
        in_specs=[pl.BlockSpec((SUBLANES, tb), lambda ph, i: (0, i))],
        out_specs=(pl.BlockSpec((1, SUBLANES, tb), lambda ph, i: (i * ph, 0, 0)),
                   pl.BlockSpec((LANES, LANES), lambda ph, i: (0, 0))),
        scratch_shapes=[pltpu.VMEM((LANES, 1), F32)],
        compiler_params=_cp("arbitrary", "arbitrary"),
        name="moe_slots",
    )(ti_t)


def _dispatch_kernel(zs_ref, idx_ref, h_ref, xg_ref, zbuf_ref, idx_smem, isem, sem, *, nz, bm, tb):
    i = pl.program_id(0)

    def zero_fill(e):
        return pltpu.make_async_copy(zbuf_ref, xg_ref.at[pl.ds(pl.multiple_of(zs_ref[e], bm), bm)], sem)

    @pl.when(i == 0)
    def _():
        zbuf_ref[...] = jnp.zeros(zbuf_ref.shape, zbuf_ref.dtype)
        for e in range(nz):
            pl.when(zs_ref[e] >= 0)(lambda e=e: zero_fill(e).start())
        for e in range(nz):
            pl.when(zs_ref[e] >= 0)(lambda e=e: zero_fill(e).wait())

    icp = pltpu.make_async_copy(idx_ref.at[0], idx_smem, isem)
    icp.start()
    icp.wait()

    def start(r, carry):
        for k in range(TOP_K):
            pltpu.make_async_copy(h_ref.at[pl.ds(r, 1)], xg_ref.at[pl.ds(idx_smem[k, r], 1)],
                                  sem).start(priority=k % 2)
        return carry

    lax.fori_loop(0, tb, start, 0, unroll=True)
    for k in range(TOP_K):
        pltpu.make_async_copy(h_ref, xg_ref.at[pl.ds(0, tb)], sem).wait()


def _moe_dispatch(zstart, dest3, h2, n_slots, bm):
    nt, _, tb = dest3.shape
    d = h2.shape[1]
    return pl.pallas_call(
        functools.partial(_dispatch_kernel, nz=zstart.shape[0], bm=bm, tb=tb),
        out_shape=jax.ShapeDtypeStruct((n_slots, d), h2.dtype),
        grid_spec=pltpu.PrefetchScalarGridSpec(
            num_scalar_prefetch=1,
            grid=(nt,),
            in_specs=[pl.BlockSpec((1, SUBLANES, tb), lambda i, zs: (i, 0, 0)),
                      pl.BlockSpec((tb, d), lambda i, zs: (i, 0))],
            out_specs=pl.BlockSpec(memory_space=pl.ANY),
            scratch_shapes=[pltpu.VMEM((bm, d), h2.dtype), pltpu.SMEM((SUBLANES, tb), I32),
                            pltpu.SemaphoreType.DMA, pltpu.SemaphoreType.DMA]),
        compiler_params=_cp("arbitrary"),
        name="moe_dispatch",
    )(zstart, dest3, h2)


def _weight_stream(w_refs, wbuf_ref, wb_refs, sem, cnt_ref, be_ref, nx_ref, used, tw):
    c = pl.program_id(0)
    b = pl.program_id(1)
    nc = pl.num_programs(0)

    def fetch(e, cc, slot):
        col = pl.ds(pl.multiple_of(cc * tw, tw), tw)
        return [pltpu.make_async_copy(w.at[e, :, col], wbuf_ref.at[slot, m], sem.at[slot])
                for m, w in enumerate(w_refs)]

    @pl.when(jnp.logical_and(c == 0, b == 0))
    def _():
        cnt_ref[0] = 0
        for cp in fetch(be_ref[0], 0, 0):
            cp.start(priority=WEIGHT_DMA_PRIORITY)

    changed = jnp.logical_or(b == 0, be_ref[b] != be_ref[jnp.maximum(b - 1, 0)])

    @pl.when(jnp.logical_and(used, changed))
    def _():
        slot = cnt_ref[0] & 1
        for cp in fetch(be_ref[b], c, slot):
            cp.wait()
        for m, wb in enumerate(wb_refs):
            wb[...] = wbuf_ref[slot, m].astype(BF16)
        nxt = nx_ref[b]
        same_chunk = nxt >= 0
        nxt_e = jnp.where(same_chunk, nxt, be_ref[0])
        nxt_c = jnp.where(same_chunk, c, c + 1)

        @pl.when(jnp.logical_or(same_chunk, c + 1 < nc))
        def _():
            for cp in fetch(nxt_e, nxt_c, 1 - slot):
                cp.start(priority=WEIGHT_DMA_PRIORITY)

        cnt_ref[0] = cnt_ref[0] + 1


def _gateup_kernel(be_ref, nx_ref, nu_ref, x_ref, bg_ref, bu_ref, wg_ref, wu_ref, o_ref,
                   wbuf_ref, wgb_ref, wub_ref, sem, cnt_ref, *, tf):
    used = pl.program_id(1) < nu_ref[0]
    _weight_stream([wg_ref, wu_ref], wbuf_ref, [wgb_ref, wub_ref], sem, cnt_ref, be_ref, nx_ref, used, tf)

    @pl.when(used)
    def _():
        x = _unpack_bf16_pairs(x_ref[...]).astype(BF16)
        gate = jnp.minimum(_dot(x, wgb_ref[...]) + bg_ref[0], SWIGLU_LIMIT)
        up = jnp.clip(_dot(x, wub_ref[...]) + bu_ref[0], -SWIGLU_LIMIT, SWIGLU_LIMIT)
        glu = gate * jax.nn.sigmoid(SWIGLU_ALPHA * gate)
        o_ref[...] = ((up + 1.0) * glu).astype(o_ref.dtype)

    @pl.when(jnp.logical_not(used))
    def _():
        o_ref[...] = jnp.zeros(o_ref.shape, o_ref.dtype)


def _moe_gateup(block_e, next_e, n_used, xg, w_gate, w_up, b_gate, b_up):
    ns, dh = xg.shape
    nexp, d, ff = w_gate.shape
    assert d == 2 * dh
    bm = MOE_BLOCK
    nb = ns // bm
    tf = _pick(ff, (1024, 512, 256, 128))
    bspec = pl.BlockSpec((1, 1, tf), lambda c, b, be, nx, nu: (be[b], 0, c))
    hbm = pl.BlockSpec(memory_space=pl.ANY)
    return pl.pallas_call(
        functools.partial(_gateup_kernel, tf=tf),
        out_shape=jax.ShapeDtypeStruct((ns, ff), BF16),
        grid_spec=pltpu.PrefetchScalarGridSpec(
            num_scalar_prefetch=3,
            grid=(ff // tf, nb),
            in_specs=[pl.BlockSpec((bm, dh), lambda c, b, be, nx, nu: (jnp.minimum(b, nu[0] - 1), 0)),
                      bspec, bspec, hbm, hbm],
            out_specs=pl.BlockSpec((bm, tf), lambda c, b, be, nx, nu: (b, c)),
            scratch_shapes=[pltpu.VMEM((2, 2, d, tf), F32), pltpu.VMEM((d, tf), BF16),
                            pltpu.VMEM((d, tf), BF16), pltpu.SemaphoreType.DMA((2,)),
                            pltpu.SMEM((1,), I32)]),
        compiler_params=_cp("arbitrary", "arbitrary"),
        name="moe_gateup",
    )(block_e, next_e, n_used, xg, b_gate.reshape(nexp, 1, ff), b_up.reshape(nexp, 1, ff), w_gate, w_up)


def _down_kernel(be_ref, nx_ref, nu_ref, g_ref, bd_ref, wd_ref, o_ref, wbuf_ref, wdb_ref, sem, cnt_ref, *, tn):
    used = pl.program_id(1) < nu_ref[0]
    _weight_stream([wd_ref], wbuf_ref, [wdb_ref], sem, cnt_ref, be_ref, nx_ref, used, tn)

    @pl.when(used)
    def _():
        o_ref[...] = _pack_bf16_pairs(_dot(g_ref[...], wdb_ref[...]) + bd_ref[0])

    @pl.when(jnp.logical_not(used))
    def _():
        o_ref[...] = jnp.zeros(o_ref.shape, o_ref.dtype)


def _moe_down(block_e, next_e, n_used, glu, w_down, b_down):
    ns, ff = glu.shape
    nexp, _, d = w_down.shape
    bm = MOE_BLOCK
    nb = ns // bm
    tn = d
    return pl.pallas_call(
        functools.partial(_down_kernel, tn=tn),
        out_shape=jax.ShapeDtypeStruct((ns, d // 2), jnp.uint32),
        grid_spec=pltpu.PrefetchScalarGridSpec(
            num_scalar_prefetch=3,
            grid=(d // tn, nb),
            in_specs=[pl.BlockSpec((bm, ff), lambda c, b, be, nx, nu: (jnp.minimum(b, nu[0] - 1), 0)),
                      pl.BlockSpec((1, 1, tn), lambda c, b, be, nx, nu: (be[b], 0, c)),
                      pl.BlockSpec(memory_space=pl.ANY)],
            out_specs=pl.BlockSpec((bm, tn // 2), lambda c, b, be, nx, nu: (b, c)),
            scratch_shapes=[pltpu.VMEM((2, 1, ff, tn), F32), pltpu.VMEM((ff, tn), BF16),
                            pltpu.SemaphoreType.DMA((2,)), pltpu.SMEM((1,), I32)]),
        compiler_params=_cp("arbitrary", "arbitrary"),
        name="moe_down",
    )(block_e, next_e, n_used, glu, b_down.reshape(nexp, 1, d), w_down)


def _combine_kernel(idx_ref, yb_ref, x1_ref, gt2_ref, p_ref, o_ref, buf_ref, idx_smem, isem, sem, *, tc):
    icp = pltpu.make_async_copy(idx_ref.at[0], idx_smem, isem)
    icp.start()
    icp.wait()

    def start(r, carry):
        for k in range(TOP_K):
            pltpu.make_async_copy(yb_ref.at[pl.ds(idx_smem[k, r], 1)],
                                  buf_ref.at[pl.ds(k * tc + r, 1)], sem).start(priority=k % 2)
        return carry

    lax.fori_loop(0, tc, start, 0, unroll=True)
    pltpu.make_async_copy(yb_ref.at[pl.ds(0, TOP_K * tc)], buf_ref, sem).wait()
    ff = p_ref[:, 0:1] * _unpack_bf16_pairs(buf_ref[0:tc, :])
    for k in range(1, TOP_K):
        ff = ff + p_ref[:, k:k + 1] * _unpack_bf16_pairs(buf_ref[k * tc:(k + 1) * tc, :])
    for ch in range(tc // CHUNK):
        rows = slice(ch * CHUNK, (ch + 1) * CHUNK)
        o_ref[rows, :] = x1_ref[rows, :] + gt2_ref[ch] * ff[rows, :]


def _moe_combine(dest3, yb, x1, gt2, top_p, row0, nrows):
    d = 2 * yb.shape[1]
    tc = dest3.shape[2]
    nc = tc // CHUNK
    rb0 = row0 // tc
    return pl.pallas_call(
        functools.partial(_combine_kernel, tc=tc),
        out_shape=jax.ShapeDtypeStruct((nrows, d), F32),
        grid=(nrows // tc,),
        in_specs=[pl.BlockSpec((1, SUBLANES, tc), lambda i: (rb0 + i, 0, 0)),
                  pl.BlockSpec(memory_space=pl.ANY),
                  pl.BlockSpec((tc, d), lambda i: (rb0 + i, 0)),
                  pl.BlockSpec((nc, 1, d), lambda i: (rb0 + i, 0, 0)),
                  pl.BlockSpec((tc, LANES), lambda i: (rb0 + i, 0))],
        out_specs=pl.BlockSpec((tc, d), lambda i: (i, 0)),
        scratch_shapes=[pltpu.VMEM((TOP_K * tc, d // 2), jnp.uint32), pltpu.SMEM((SUBLANES, tc), I32),
                        pltpu.SemaphoreType.DMA, pltpu.SemaphoreType.DMA],
        compiler_params=_cp("arbitrary"),
        name="moe_combine",
    )(dest3, yb, x1, gt2, top_p)


def _block_layout(meta, nexp, bm, n_blocks):
    counts = meta[:nexp, 0]
    start_blk = meta[:nexp, 1].astype(I32)
    nblk = jnp.ceil(counts * (1.0 / bm)).astype(I32)
    end_blk = start_blk + nblk
    n_used = jnp.sum(nblk)
    blk = jnp.minimum(jnp.arange(n_blocks), n_used - 1)
    block_e = jnp.minimum(jnp.sum(end_blk[None, :] <= blk[:, None], axis=1), nexp - 1).astype(I32)
    onehot_e = block_e[:, None] == jnp.arange(nexp)[None, :]
    run_end = jnp.sum(jnp.where(onehot_e, end_blk[None, :], 0), axis=1)
    at_end = run_end[:, None] == jnp.arange(n_blocks)[None, :]
    next_e = jnp.where(run_end < n_used, jnp.sum(jnp.where(at_end, block_e[None, :], 0), axis=1), -1).astype(I32)
    last = jnp.where(nblk > 0, (end_blk - 1) * bm, -1)
    spare = n_used + jnp.arange(nexp)
    spare = jnp.where(spare < n_blocks, spare * bm, -1)
    zstart = jnp.concatenate([last, spare]).astype(I32)
    return block_e, next_e, n_used.reshape(1).astype(I32), zstart


def _rope_tables(pos):
    half = ROPE_DIM // 2
    inv = ROPE_THETA ** (-jnp.arange(half, dtype=F32) / half)
    ang = pos.astype(F32)[:, None] * inv[None, :]
    z = jnp.zeros((pos.shape[0], LANES - ROPE_DIM), F32)
    cos, sin = jnp.cos(ang), jnp.sin(ang)
    return jnp.concatenate([cos, cos, z], axis=1), jnp.concatenate([sin, sin, z], axis=1)


def _rot_half_cols(w):
    half = ROPE_DIM // 2
    return jnp.concatenate([-w[..., half:], w[..., :half]], axis=-1)


def _layer(x_prompt, x_sample, past_lat, past_kr, ssm_s0, conv_s0, c_prompt, c_sample,
           w_ada, b_ada, g_norm1, w_in, g_cq, g_ckv, w_uq, w_uk, w_uv, g_qn, g_kn, conv_w, conv_b, dt_bias,
           a_log, d_skip, g_ssm, w_pa, w_pb, w_out, g_norm2, w_router, b_router, w_gate, b_gate, w_up, b_up,
           w_down, b_down):
    bp, sp, d = x_prompt.shape
    bs, ss, _ = x_sample.shape
    assert ss == CHUNK and sp % CHUNK == 0
    past = past_lat.shape[1]
    ql, kvl = g_cq.shape[-1], g_ckv.shape[-1]
    heads = w_uq.shape[1]
    ci = g_ssm.shape[-1]
    cc = conv_w.shape[-1]
    nh = ci // M_HEADDIM
    groups = (cc - ci) // (2 * D_STATE)
    nexp = w_router.shape[-1]
    tp, ts = bp * sp, bs * ss
    t = tp + ts
    ncp, ncs = tp // CHUNK, ts // CHUNK

    ada = _ada(jnp.concatenate([c_prompt, c_sample], axis=0), w_ada, b_ada)
    per_chunk = jnp.concatenate([jnp.repeat(ada[:bp], sp // CHUNK, axis=0), ada[bp:]], axis=0)
    sh1, sc1, gt1, sh2, sc2, gt2 = [m[:, None, :] for m in jnp.split(per_chunk, 6, axis=-1)]

    xp3 = x_prompt.reshape(ncp, CHUNK, d)
    xs3 = x_sample.reshape(ncs, CHUNK, d)
    h3 = _norm1(xp3, xs3, g_norm1, sc1, sh1)
    h_all = h3.reshape(t, d)

    o = 0
    w_cq = w_in[:, o:o + ql]; o += ql
    w_ckv = w_in[:, o:o + kvl]; o += kvl
    w_kr = w_in[:, o:o + ROPE_DIM]; o += ROPE_DIM
    w_z = w_in[:, o:o + ci]; o += ci
    w_xbc = w_in[:, o:o + cc]; o += cc
    w_dt = w_in[:, o:o + nh]; o += nh
    w_gab = w_in[:, o:o + 2 * d]
    zc = lambda n_: jnp.zeros((d, n_), F32)
    w_lat = jnp.concatenate([w_cq, w_ckv, w_kr, zc(LANES - ROPE_DIM), _rot_half_cols(w_kr),
                             zc(LANES - ROPE_DIM), w_dt, zc(LANES - nh)], axis=1).astype(BF16)

    pos = jnp.concatenate([jnp.tile(jnp.arange(sp), bp), jnp.tile(past + jnp.arange(ss), bs)])
    cos128, sin128 = _rope_tables(pos)

    cqn, lat_all, latk, kr_all, dt_all, dtt_all = _lat(
        h_all, w_lat, w_dt.T.astype(BF16), g_cq, g_ckv, cos128, sin128, dt_bias)
    z_all = _mm(h_all, w_z.astype(BF16), BF16, "proj_z")
    xbc_all = _mm(h_all, w_xbc.astype(BF16), BF16, "proj_xbc")
    gates = _mm(h_all, w_gab.astype(BF16), BF16, "proj_gates", act="sigmoid")

    tail = CONV_W - 1
    h_tail = jnp.concatenate([h3[:ncp].reshape(bp, sp, d)[:, sp - tail:, :].reshape(bp * tail, d),
                              h3[ncp:][:, CHUNK - tail:, :].reshape(bs * tail, d)], axis=0)
    conv_tail = _mm(h_tail, w_xbc.astype(BF16), F32, "proj_conv_tail")
    conv_p = conv_tail[:bp * tail].reshape(bp, tail, cc)
    conv_s = conv_tail[bp * tail:].reshape(bs, tail, cc)

    wq_a = jnp.concatenate([w_uq, jnp.zeros((ql, heads, HEAD_PAD - QK_DIM), F32)], axis=-1)
    wq_a = wq_a.reshape(ql, heads * HEAD_PAD).astype(BF16)
    wq_b = jnp.concatenate([_rot_half_cols(w_uq[..., NOPE_DIM:]),
                            jnp.zeros((ql, heads, LANES - ROPE_DIM), F32)], axis=-1)
    wq_b = wq_b.reshape(ql, heads * LANES).astype(BF16)
    gq = g_qn * g_kn * (ATTN_SCALE * math.log2(math.e))
    g_nope = gq[:NOPE_DIM].reshape(1, LANES)
    g_rope = jnp.concatenate([gq[NOPE_DIM:], jnp.zeros((LANES - ROPE_DIM,), F32)]).reshape(1, LANES)
    q_all = _qproj(cqn, wq_a, wq_b, cos128, sin128, g_nope, g_rope, heads)

    w_uk2 = w_uk.reshape(kvl, heads * NOPE_DIM)
    w_uv2 = w_uv.reshape(kvl, heads * V_DIM).astype(BF16)
    w_kv = jnp.concatenate([w_uk2.astype(BF16), w_uv2], axis=1)
    k_p, v_p = _kvproj(latk, w_kv, tp, heads, kvl)
    a_p = _attn_prompt(q_all, k_p, v_p, bp, sp, heads)
    a_s = _attn_sample(past_lat, past_kr, latk, q_all, w_uk2.T.astype(BF16), w_uv2, tp, heads)

    eh = jnp.repeat(jnp.eye(nh, dtype=BF16), M_HEADDIM, axis=1)
    dsk = jnp.repeat(d_skip, M_HEADDIM).reshape(1, ci)
    lcp = _pick(sp, (256, 128))
    assert sp % lcp == 0 and lcp % LANES == 0
    ncq = sp // lcp
    dt3_p = dt_all[:tp].reshape(bp * ncq, lcp, nh)
    dtt3_p = dtt_all[:, :tp].reshape(nh, bp * ncq, lcp).transpose(1, 0, 2)
    lcs = CHUNK
    dt3_s = dt_all[tp:].reshape(bs, CHUNK, nh)
    dtt3_s = dtt_all[:, tp:].reshape(nh, bs, CHUNK).transpose(1, 0, 2)
    pad_conv = lambda c0: jnp.concatenate(
        [jnp.zeros((c0.shape[0], SUBLANES - tail, cc), F32), c0], axis=1)
    m_p, ssm_p = _ssd(z_all, xbc_all, dt3_p, dtt3_p, jnp.zeros((bp, SUBLANES, cc), F32),
                      jnp.zeros((bp, ci, D_STATE), F32), conv_w, conv_b, a_log, dsk, g_ssm, eh,
                      0, bp, ncq, lcp, lcp, groups)
    m_s, ssm_s = _ssd(z_all, xbc_all, dt3_s, dtt3_s, pad_conv(conv_s0),
                      ssm_s0.reshape(bs, ci, D_STATE), conv_w, conv_b, a_log, dsk, g_ssm, eh,
                      tp, bs, 1, lcs, CHUNK, groups)

    mixed = _mix(a_p, a_s, m_p, m_s, gates, w_pa.astype(BF16), w_pb.astype(BF16))
    wr = jnp.concatenate([w_router, jnp.zeros((d, LANES - nexp), F32)], axis=1)
    wr_hi = wr.astype(BF16)
    wr_lo = (wr - wr_hi.astype(F32)).astype(BF16)
    br = jnp.concatenate([b_router, jnp.zeros((LANES - nexp,), F32)]).reshape(1, LANES)
    x1, h2, ti, tpr = _post(mixed, w_out.astype(BF16), xp3, xs3, gt1, sc2, sh2, g_norm2, wr_hi, wr_lo, br, nexp)

    assert nexp <= LANES
    bm = MOE_BLOCK
    n_blocks = -(-(t * TOP_K + nexp * (bm - 1)) // bm)
    tb = _pick(math.gcd(tp, ts), (256, 128))
    dest3, meta = _moe_slots(ti[:, :SUBLANES].T, bm, tb)
    block_e, next_e, n_used, zstart = _block_layout(meta, nexp, bm, n_blocks)
    xg = _moe_dispatch(zstart, dest3, h2, n_blocks * bm, bm)
    glu = _moe_gateup(block_e, next_e, n_used, xg, w_gate, w_up, b_gate, b_up)
    yb = _moe_down(block_e, next_e, n_used, glu, w_down, b_down)
    y_p = _moe_combine(dest3, yb, x1, gt2, tpr, 0, tp)
    y_s = _moe_combine(dest3, yb, x1, gt2, tpr, tp, ts)

    return (y_p.reshape(bp, sp, d), y_s.reshape(bs, ss, d),
            lat_all[:tp].reshape(bp, sp, kvl), kr_all[:tp].reshape(bp, sp, ROPE_DIM),
            ssm_p.reshape(bp, nh, M_HEADDIM, D_STATE), conv_p,
            lat_all[tp:].reshape(bs, ss, kvl), kr_all[tp:].reshape(bs, ss, ROPE_DIM),
            ssm_s.reshape(bs, nh, M_HEADDIM, D_STATE), conv_s)


def kernel(x_prompt, x_sample, cache_mla_latent, cache_mla_krope, state_ssm, state_conv, c_prompt, c_sample,
           w_ada, b_ada, g_norm1, w_in, g_cq, g_ckv, w_uq, w_uk, w_uv, g_qn, g_kn, conv_w, conv_b, dt_bias,
           a_log, d_skip, g_ssm, w_pa, w_pb, w_out, g_norm2, w_router, b_router, w_gate, b_gate, w_up, b_up,
           w_down, b_down):
    depth = w_ada.shape[0]
    assert depth == 1, "single-layer encoder"
    weights = (w_ada, b_ada, g_norm1, w_in, g_cq, g_ckv, w_uq, w_uk, w_uv, g_qn, g_kn, conv_w, conv_b, dt_bias,
               a_log, d_skip, g_ssm, w_pa, w_pb, w_out, g_norm2, w_router, b_router, w_gate, b_gate, w_up, b_up,
               w_down, b_down)
    outs = _layer(x_prompt, x_sample, cache_mla_latent[0], cache_mla_krope[0], state_ssm[0], state_conv[0],
                  c_prompt, c_sample, *[w[0] for w in weights])
    y_p, y_s = outs[0], outs[1]
    return (y_p, y_s) + tuple(o[None] for o in outs[2:6]) + tuple(o[None] for o in outs[6:])
```

```python
import functools
import math

import jax
import jax.numpy as jnp
from jax import lax
from jax.experimental import pallas as pl
from jax.experimental.pallas import tpu as pltpu

F32 = jnp.float32
BF16 = jnp.bfloat16
I32 = jnp.int32

CHUNK = 64
NOPE_DIM = 128
ROPE_DIM = 64
QK_DIM = NOPE_DIM + ROPE_DIM
V_DIM = 128
HEAD_PAD = 256
ROPE_THETA = 10000.0
ATTN_SCALE = QK_DIM ** -0.5
M_HEADDIM = 64
D_STATE = 128
CONV_W = 4
TOP_K = 4
SWIGLU_LIMIT = 7.0
SWIGLU_ALPHA = 1.702
EPS = 1e-6

LANES = 128
SUBLANES = 8
VMEM_LIMIT = 56 * 1024 * 1024

MOE_BLOCK = 256
WEIGHT_DMA_PRIORITY = 1
NEG_BIG = -1e30

_NT = (((1,), (1,)), ((), ()))
_TN = (((0,), (0,)), ((), ()))


def _cp(*sem):
    return pltpu.CompilerParams(dimension_semantics=sem, vmem_limit_bytes=VMEM_LIMIT)


def _pick(n, prefs):
    for p in prefs:
        if n % p == 0:
            return p
    return n


def _dot(a, b):
    return jnp.dot(a, b, preferred_element_type=F32)


def _split3(v):
    hi = v.astype(BF16)
    r1 = v - hi.astype(F32)
    mid = r1.astype(BF16)
    lo = (r1 - mid.astype(F32)).astype(BF16)
    return hi, mid, lo


def _silu(x):
    return x * jax.nn.sigmoid(x)


def _softplus(x):
    return jnp.maximum(x, 0.0) + jnp.log1p(jnp.exp(-jnp.abs(x)))


def _ada_kernel(c_ref, w_ref, b_ref, o_ref):
    s = _silu(c_ref[...]).astype(BF16)
    o_ref[...] = _dot(s, w_ref[...].astype(BF16)) + b_ref[...]


def _ada(c_all, w_ada, b_ada):
    r, d = c_all.shape
    n = w_ada.shape[1]
    tn = _pick(n, (1024, 512, 256, 128))
    return pl.pallas_call(
        _ada_kernel,
        out_shape=jax.ShapeDtypeStruct((r, n), F32),
        grid=(n // tn,),
        in_specs=[pl.BlockSpec((r, d), lambda j: (0, 0)),
                  pl.BlockSpec((d, tn), lambda j: (0, j)),
                  pl.BlockSpec((1, tn), lambda j: (0, j))],
        out_specs=pl.BlockSpec((r, tn), lambda j: (0, j)),
        compiler_params=_cp("arbitrary"),
        name="ada",
    )(c_all, w_ada, b_ada.reshape(1, n))


def _norm1_kernel(xp_ref, xs_ref, g_ref, sc_ref, sh_ref, h_ref, *, npb):
    i = pl.program_id(0)
    x = jnp.where(i < npb, xp_ref[...], xs_ref[...])
    xn = x * lax.rsqrt(jnp.mean(x * x, axis=-1, keepdims=True) + EPS)
    h_ref[...] = (xn * g_ref[...] * (1.0 + sc_ref[...]) + sh_ref[...]).astype(BF16)


def _norm1(xp3, xs3, g, sc, sh):
    ncp, _, d = xp3.shape
    ncs = xs3.shape[0]
    nch = ncp + ncs
    gc = _pick(math.gcd(ncp, ncs), (4, 2, 1))
    npb = ncp // gc
    blk = (gc, CHUNK, d)
    mod = pl.BlockSpec((gc, 1, d), lambda i: (i, 0, 0))
    return pl.pallas_call(
        functools.partial(_norm1_kernel, npb=npb),
        out_shape=jax.ShapeDtypeStruct((nch, CHUNK, d), BF16),
        grid=(nch // gc,),
        in_specs=[pl.BlockSpec(blk, lambda i: (jnp.minimum(i, npb - 1), 0, 0)),
                  pl.BlockSpec(blk, lambda i: (jnp.maximum(i - npb, 0), 0, 0)),
                  pl.BlockSpec((1, 1, d), lambda i: (0, 0, 0)),
                  mod, mod],
        out_specs=pl.BlockSpec(blk, lambda i: (i, 0, 0)),
        compiler_params=_cp("arbitrary"),
        name="norm1",
    )(xp3, xs3, g.reshape(1, 1, d), sc, sh)


def _mm_kernel(x_ref, w_ref, o_ref, *, act):
    acc = _dot(x_ref[...], w_ref[...])
    if act == "sigmoid":
        acc = jax.nn.sigmoid(acc)
    o_ref[...] = acc.astype(o_ref.dtype)


def _mm(x, w, out_dtype, name, act=None):
    m, k = x.shape
    n = w.shape[1]
    tm = _pick(m, (1024, 512, 256))
    tn = _pick(n, (1024, 512, 256, 128))
    return pl.pallas_call(
        functools.partial(_mm_kernel, act=act),
        out_shape=jax.ShapeDtypeStruct((m, n), out_dtype),
        grid=(m // tm, n // tn),
        in_specs=[pl.BlockSpec((tm, k), lambda i, j: (i, 0)),
                  pl.BlockSpec((k, tn), lambda i, j: (0, j))],
        out_specs=pl.BlockSpec((tm, tn), lambda i, j: (i, j)),
        compiler_params=_cp("arbitrary", "arbitrary"),
        name=name,
    )(x, w)


def _lat_kernel(h_ref, w_ref, wdt_ref, gcq_ref, gckv_ref, cos_ref, sin_ref, dtb_ref, dtbc_ref,
                cqn_ref, lat_ref, latk_ref, kr_ref, dt_ref, dtt_ref, *, ql, kvl):
    h = h_ref[...]
    acc = _dot(h, w_ref[...])
    cq = acc[:, :ql]
    cqn = cq * lax.rsqrt(jnp.mean(cq * cq, axis=-1, keepdims=True) + EPS) * gcq_ref[...]
    cqn_ref[...] = cqn.astype(BF16)
    ckv = acc[:, ql:ql + kvl]
    lat = ckv * lax.rsqrt(jnp.mean(ckv * ckv, axis=-1, keepdims=True) + EPS) * gckv_ref[...]
    lat_ref[...] = lat
    o = ql + kvl
    kr128 = acc[:, o:o + LANES] * cos_ref[...] + acc[:, o + LANES:o + 2 * LANES] * sin_ref[...]
    kr_ref[...] = kr128[:, :ROPE_DIM]
    latk_ref[:, :kvl] = lat.astype(BF16)
    latk_ref[:, kvl:] = kr128.astype(BF16)
    nh = dt_ref.shape[-1]
    dt_ref[...] = _softplus(acc[:, o + 2 * LANES:o + 2 * LANES + nh] + dtb_ref[...])
    dtt = lax.dot_general(wdt_ref[...], h, _NT, preferred_element_type=F32)
    dtt_ref[...] = _softplus(dtt + dtbc_ref[...])


def _lat(h_all, w_lat, w_dt_t, g_cq, g_ckv, cos128, sin128, dt_bias):
    t, d = h_all.shape
    ql, kvl = g_cq.shape[-1], g_ckv.shape[-1]
    nh = dt_bias.shape[-1]
    nl = w_lat.shape[1]
    tm = _pick(t, (512, 256, 128))
    row = lambda w: pl.BlockSpec((tm, w), lambda i: (i, 0))
    const = lambda a, b: pl.BlockSpec((a, b), lambda i: (0, 0))
    return pl.pallas_call(
        functools.partial(_lat_kernel, ql=ql, kvl=kvl),
        out_shape=(jax.ShapeDtypeStruct((t, ql), BF16),
                   jax.ShapeDtypeStruct((t, kvl), F32),
                   jax.ShapeDtypeStruct((t, kvl + LANES), BF16),
                   jax.ShapeDtypeStruct((t, ROPE_DIM), F32),
                   jax.ShapeDtypeStruct((t, nh), F32),
                   jax.ShapeDtypeStruct((nh, t), F32)),
        grid=(t // tm,),
        in_specs=[row(d), const(d, nl), const(nh, d), const(1, ql), const(1, kvl),
                  row(LANES), row(LANES), const(1, nh), const(nh, 1)],
        out_specs=(row(ql), row(kvl), row(kvl + LANES), row(ROPE_DIM), row(nh),
                   pl.BlockSpec((nh, tm), lambda i: (0, i))),
        compiler_params=_cp("arbitrary"),
        name="latent_proj",
    )(h_all, w_lat, w_dt_t, g_cq.reshape(1, ql), g_ckv.reshape(1, kvl), cos128, sin128,
      dt_bias.reshape(1, nh), dt_bias.reshape(nh, 1))


def _q_kernel(c_ref, wa_ref, wb_ref, cos_ref, sin_ref, gn_ref, gr_ref, o_ref, *, heads):
    c = c_ref[...]
    a = _dot(c, wa_ref[...])
    b = _dot(c, wb_ref[...])
    cos, sin = cos_ref[...], sin_ref[...]
    for h in range(heads):
        nope = a[:, h * HEAD_PAD:h * HEAD_PAD + LANES]
        rope = a[:, h * HEAD_PAD + LANES:(h + 1) * HEAD_PAD] * cos + b[:, h * LANES:(h + 1) * LANES] * sin
        ss = (jnp.sum(nope * nope, axis=-1, keepdims=True)
              + jnp.sum(rope * rope, axis=-1, keepdims=True)) * (1.0 / QK_DIM)
        r = lax.rsqrt(ss + EPS)
        o_ref[:, h * HEAD_PAD:h * HEAD_PAD + LANES] = (nope * r * gn_ref[...]).astype(BF16)
        o_ref[:, h * HEAD_PAD + LANES:(h + 1) * HEAD_PAD] = (rope * r * gr_ref[...]).astype(BF16)


def _qproj(cqn, wq_a, wq_b, cos128, sin128, g_nope, g_rope, heads):
    t, ql = cqn.shape
    tm = _pick(t, (256, 128))
    row = lambda w: pl.BlockSpec((tm, w), lambda i: (i, 0))
    const = lambda a, b: pl.BlockSpec((a, b), lambda i: (0, 0))
    return pl.pallas_call(
        functools.partial(_q_kernel, heads=heads),
        out_shape=jax.ShapeDtypeStruct((t, heads * HEAD_PAD), BF16),
        grid=(t // tm,),
        in_specs=[row(ql), const(ql, heads * HEAD_PAD), const(ql, heads * LANES),
                  row(LANES), row(LANES), const(1, LANES), const(1, LANES)],
        out_specs=row(heads * HEAD_PAD),
        compiler_params=_cp("arbitrary"),
        name="q_proj",
    )(cqn, wq_a, wq_b, cos128, sin128, g_nope, g_rope)


def _kv_kernel(lat_ref, kr_ref, w_ref, k_ref, v_ref, *, heads):
    acc = _dot(lat_ref[...], w_ref[...])
    kr = kr_ref[...].astype(F32)
    kr2 = jnp.sum(kr * kr, axis=-1, keepdims=True)
    for h in range(heads):
        kn = acc[:, h * LANES:(h + 1) * LANES]
        ss = (jnp.sum(kn * kn, axis=-1, keepdims=True) + kr2) * (1.0 / QK_DIM)
        r = lax.rsqrt(ss + EPS)
        k_ref[:, h * HEAD_PAD:h * HEAD_PAD + LANES] = (kn * r).astype(BF16)
        k_ref[:, h * HEAD_PAD + LANES:(h + 1) * HEAD_PAD] = (kr * r).astype(BF16)
    v_ref[...] = acc[:, heads * LANES:].astype(BF16)


def _kvproj(latk, w_kv, tp, heads, kvl):
    tm = _pick(tp, (256, 128))
    return pl.pallas_call(
        functools.partial(_kv_kernel, heads=heads),
        out_shape=(jax.ShapeDtypeStruct((tp, heads * HEAD_PAD), BF16),
                   jax.ShapeDtypeStruct((tp, heads * V_DIM), BF16)),
        grid=(tp // tm,),
        in_specs=[pl.BlockSpec((tm, kvl), lambda i: (i, 0)),
                  pl.BlockSpec((tm, LANES), lambda i: (i, kvl // LANES)),
                  pl.BlockSpec((kvl, 2 * heads * LANES), lambda i: (0, 0))],
        out_specs=(pl.BlockSpec((tm, heads * HEAD_PAD), lambda i: (i, 0)),
                   pl.BlockSpec((tm, heads * V_DIM), lambda i: (i, 0))),
        compiler_params=_cp("arbitrary"),
        name="kv_proj",
    )(latk, latk, w_kv)


def _attn_p_kernel(q_ref, k_ref, v_ref, o_ref, m_ref, l_ref, acc_ref, *, tq, hg):
    qi = pl.program_id(2)
    m_ref[...] = jnp.full(m_ref.shape, -jnp.inf, F32)
    l_ref[...] = jnp.zeros(l_ref.shape, F32)
    acc_ref[...] = jnp.zeros(acc_ref.shape, F32)

    def tile(j, masked):
        ks = pl.multiple_of(j * tq, tq)
        for g in range(hg):
            q = q_ref[:, g * HEAD_PAD:(g + 1) * HEAD_PAD]
            k = k_ref[pl.ds(ks, tq), g * HEAD_PAD:(g + 1) * HEAD_PAD]
            v = v_ref[pl.ds(ks, tq), g * V_DIM:(g + 1) * V_DIM]
            s = lax.dot_general(q, k, _NT, preferred_element_type=F32)
            if masked:
                rc = lax.broadcasted_iota(I32, (tq, tq), 0) // CHUNK
                cc = lax.broadcasted_iota(I32, (tq, tq), 1) // CHUNK
                s = jnp.where(cc <= rc, s, -jnp.inf)
            m_prev = m_ref[g]
            m_new = jnp.maximum(m_prev, jnp.max(s, axis=-1, keepdims=True))
            alpha = jnp.exp2(m_prev - m_new)
            p = jnp.exp2(s - jnp.tile(m_new, (1, tq // LANES)))
            l_ref[g] = alpha * l_ref[g] + jnp.sum(p, axis=-1, keepdims=True)
            acc_ref[g] = alpha * acc_ref[g] + _dot(p.astype(BF16), v)
            m_ref[g] = m_new

    def body(j, carry):
        tile(j, False)
        return carry

    lax.fori_loop(0, qi, body, 0)
    tile(qi, True)
    for g in range(hg):
        o_ref[:, g * V_DIM:(g + 1) * V_DIM] = (acc_ref[g] / l_ref[g]).astype(o_ref.dtype)


def _attn_prompt(q_all, k_p, v_p, batch, seq, heads):
    tq = _pick(seq, (512, 256, 128, 64))
    nq = seq // tq
    hg = _pick(heads, (4, 2, 1))
    return pl.pallas_call(
        functools.partial(_attn_p_kernel, tq=tq, hg=hg),
        out_shape=jax.ShapeDtypeStruct((batch * seq, heads * V_DIM), BF16),
        grid=(batch, heads // hg, nq),
        in_specs=[pl.BlockSpec((tq, hg * HEAD_PAD), lambda b, h, qi: (b * nq + qi, h)),
                  pl.BlockSpec((seq, hg * HEAD_PAD), lambda b, h, qi: (b, h)),
                  pl.BlockSpec((seq, hg * V_DIM), lambda b, h, qi: (b, h))],
        out_specs=pl.BlockSpec((tq, hg * V_DIM), lambda b, h, qi: (b * nq + qi, h)),
        scratch_shapes=[pltpu.VMEM((hg, tq, LANES), F32), pltpu.VMEM((hg, tq, LANES), F32),
                        pltpu.VMEM((hg, tq, V_DIM), F32)],
        compiler_params=_cp("arbitrary", "arbitrary", "arbitrary"),
        name="attn_prompt",
    )(q_all, k_p, v_p)


def _attn_s_kernel(pl_ref, pk_ref, nl_ref, q_ref, wukt_ref, wuv_ref, o_ref,
                   qabs_ref, qr_ref, m_ref, l_ref, acc_ref, *, heads, kvl, nkb):
    kb = pl.program_id(1)

    @pl.when(kb == 0)
    def _():
        for h in range(heads):
            qn = q_ref[:, h * HEAD_PAD:h * HEAD_PAD + LANES]
            qabs_ref[h * CHUNK:(h + 1) * CHUNK, :] = _dot(
                qn, wukt_ref[h * LANES:(h + 1) * LANES, :]).astype(BF16)
            qr_ref[h * CHUNK:(h + 1) * CHUNK, :] = q_ref[:, h * HEAD_PAD + LANES:(h + 1) * HEAD_PAD]
        m_ref[...] = jnp.full(m_ref.shape, -jnp.inf, F32)
        l_ref[...] = jnp.zeros(l_ref.shape, F32)
        acc_ref[...] = jnp.zeros(acc_ref.shape, F32)

    def block(xl, krf):
        rtop = lax.dot_general(wukt_ref[...], xl, _NT, preferred_element_type=F32)
        sq = krf * krf
        sq_hi = sq.astype(BF16)
        sq_lo = (sq - sq_hi.astype(F32)).astype(BF16)
        ones = jnp.ones((SUBLANES, ROPE_DIM), BF16)
        kr2 = (lax.dot_general(ones, sq_hi, _NT, preferred_element_type=F32)
               + lax.dot_general(ones, sq_lo, _NT, preferred_element_type=F32))[0:1, :]
        s = (lax.dot_general(qabs_ref[...], xl, _NT, preferred_element_type=F32)
             + lax.dot_general(qr_ref[:, :ROPE_DIM], krf.astype(BF16), _NT,
                               preferred_element_type=F32))
        parts = []
        for h in range(heads):
            rt = rtop[h * LANES:(h + 1) * LANES, :]
            kn2 = jnp.sum(rt * rt, axis=0, keepdims=True)
            r = lax.rsqrt((kn2 + kr2) * (1.0 / QK_DIM) + EPS)
            parts.append(s[h * CHUNK:(h + 1) * CHUNK, :] * r)
        s = jnp.concatenate(parts, axis=0)
        n = s.shape[1]
        m_prev = m_ref[...]
        m_new = jnp.maximum(m_prev, jnp.max(s, axis=-1, keepdims=True))
        alpha = jnp.exp2(m_prev - m_new)
        m_wide = jnp.tile(m_new, (1, n // LANES)) if n >= LANES else m_new[:, :n]
        p = jnp.exp2(s - m_wide)
        l_ref[...] = alpha * l_ref[...] + jnp.sum(p, axis=-1, keepdims=True)
        acc_ref[...] = jnp.tile(alpha, (1, kvl // LANES)) * acc_ref[...] + _dot(p.astype(BF16), xl)
        m_ref[...] = m_new

    @pl.when(kb < nkb)
    def _():
        block(pl_ref[0].astype(BF16), pk_ref[0])

    @pl.when(kb == nkb)
    def _():
        block(nl_ref[:, :kvl], nl_ref[:, kvl:kvl + ROPE_DIM].astype(F32))
        o = (acc_ref[...] / jnp.tile(l_ref[...], (1, kvl // LANES))).astype(BF16)
        for h in range(heads):
            o_ref[:, h * V_DIM:(h + 1) * V_DIM] = _dot(
                o[h * CHUNK:(h + 1) * CHUNK, :], wuv_ref[:, h * V_DIM:(h + 1) * V_DIM]).astype(o_ref.dtype)


def _attn_sample(past_lat, past_kr, latk, q_all, w_uk_t, w_uv2, tp, heads):
    bs, past, kvl = past_lat.shape
    tk = _pick(past, (512, 256, 128))
    nkb = past // tk
    c0 = tp // CHUNK
    pidx = lambda b, kb: (b, jnp.minimum(kb, nkb - 1), 0)
    hq = heads * CHUNK
    return pl.pallas_call(
        functools.partial(_attn_s_kernel, heads=heads, kvl=kvl, nkb=nkb),
        out_shape=jax.ShapeDtypeStruct((bs * CHUNK, heads * V_DIM), BF16),
        grid=(bs, nkb + 1),
        in_specs=[pl.BlockSpec((1, tk, kvl), pidx),
                  pl.BlockSpec((1, tk, ROPE_DIM), pidx),
                  pl.BlockSpec((CHUNK, kvl + LANES), lambda b, kb: (c0 + b, 0)),
                  pl.BlockSpec((CHUNK, heads * HEAD_PAD), lambda b, kb: (c0 + b, 0)),
                  pl.BlockSpec((heads * LANES, kvl), lambda b, kb: (0, 0)),
                  pl.BlockSpec((kvl, heads * V_DIM), lambda b, kb: (0, 0))],
        out_specs=pl.BlockSpec((CHUNK, heads * V_DIM), lambda b, kb: (b, 0)),
        scratch_shapes=[pltpu.VMEM((hq, kvl), BF16), pltpu.VMEM((hq, LANES), BF16),
                        pltpu.VMEM((hq, LANES), F32), pltpu.VMEM((hq, LANES), F32),
                        pltpu.VMEM((hq, kvl), F32)],
        compiler_params=_cp("arbitrary", "arbitrary"),
        name="attn_sample",
    )(past_lat, past_kr, latk, q_all, w_uk_t, w_uv2)


def _ssd_kernel(z_ref, x_ref, dt_ref, dtt_ref, conv0_ref, ssm0_ref, cw_ref, cb_ref, alr_ref, alc_ref,
                dsk_ref, gs_ref, eh_ref, o_ref, st_out_ref, xs_ref, st_ref,
                *, lc, lr, nh, groups):
    c = pl.program_id(1)
    p = M_HEADDIM
    n = D_STATE
    ci = nh * p
    k8 = nh // groups
    gw = k8 * p

    @pl.when(c == 0)
    def _():
        xs_ref[0:SUBLANES, :] = conv0_ref[0]
        st_ref[...] = ssm0_ref[0]

    xs_ref[SUBLANES:SUBLANES + lr, :] = x_ref[...].astype(F32)
    if lc > lr:
        xs_ref[SUBLANES + lr:SUBLANES + lc, :] = jnp.zeros((lc - lr, xs_ref.shape[1]), F32)

    def conv(lo, hi):
        u = xs_ref[SUBLANES - 3:SUBLANES - 3 + lc, lo:hi] * cw_ref[0:1, lo:hi]
        for tap in range(1, CONV_W):
            u = u + xs_ref[SUBLANES - 3 + tap:SUBLANES - 3 + tap + lc, lo:hi] * cw_ref[tap:tap + 1, lo:hi]
        return _silu(u + cb_ref[:, lo:hi])

    dt = dt_ref[0]
    dtt = dtt_ref[0]
    a_row = -jnp.exp(alr_ref[...])
    a_col = -jnp.exp(alc_ref[...])
    ri = lax.broadcasted_iota(I32, (lc, lc), 0)
    cidx = lax.broadcasted_iota(I32, (lc, lc), 1)
    tri = ri >= cidx
    tril = jnp.where(tri, 1.0, 0.0).astype(BF16)
    triu = jnp.where(ri <= cidx, 1.0, 0.0).astype(BF16)
    cs = sum(_dot(tril, piece) for piece in _split3(dt * a_row))
    cst = sum(_dot(piece, triu) for piece in _split3(dtt * a_col))
    exp_cs = jnp.exp(cs)
    w_end = jnp.exp(cs[lc - 1:lc, :] - cs)
    stacked = jnp.concatenate([dt, exp_cs, w_end], axis=0)
    eh = eh_ref[...]
    expanded = sum(_dot(piece, eh) for piece in _split3(stacked))
    dt_e, ecs_e, wend_e = expanded[:lc], expanded[lc:2 * lc], expanded[2 * lc:]
    cdec = jnp.exp(cst[:, lc - 1:lc])
    lane_lo = lax.broadcasted_iota(I32, (lc, LANES), 1) < p

    for g in range(groups):
        gs = slice(g * gw, (g + 1) * gw)
        xg = conv(g * gw, (g + 1) * gw)
        bg = conv(ci + g * n, ci + (g + 1) * n).astype(BF16)
        cg = conv(ci + groups * n + g * n, ci + groups * n + (g + 1) * n).astype(BF16)
        cbm = lax.dot_general(cg, bg, _NT, preferred_element_type=F32)
        xdt = xg * dt_e[:, gs]
        xdt_b = xdt.astype(BF16)
        pairs = []
        for q in range(k8 // 2):
            x2 = xdt_b[:, q * LANES:(q + 1) * LANES]
            ys = []
            for s in range(2):
                h = g * k8 + 2 * q + s
                seg = cs[:, h:h + 1] - cst[h:h + 1, :]
                dec = jnp.exp(jnp.where(tri, seg, -jnp.inf))
                ys.append(_dot((cbm * dec).astype(BF16), x2))
            pairs.append(jnp.where(lane_lo, ys[0], ys[1]))
        y_diag = jnp.concatenate(pairs, axis=1)
        sg = st_ref[g * gw:(g + 1) * gw, :]
        y_off = lax.dot_general(cg, sg.astype(BF16), _NT, preferred_element_type=F32) * ecs_e[:, gs]
        y = y_diag + y_off + xg * dsk_ref[:, gs]
        xw = (xdt * wend_e[:, gs]).astype(BF16)
        upd = lax.dot_general(xw, bg, _TN, preferred_element_type=F32)
        for k in range(k8):
            h = g * k8 + k
            rows = slice(g * gw + k * p, g * gw + (k + 1) * p)
            st_ref[rows, :] = st_ref[rows, :] * cdec[h:h + 1, :] + upd[k * p:(k + 1) * p, :]
        zg = z_ref[:, gs].astype(F32)
        u2 = y[:lr] * _silu(zg)
        ms = jnp.mean(u2 * u2, axis=-1, keepdims=True)
        o_ref[:, gs] = (u2 * lax.rsqrt(ms + EPS) * gs_ref[:, gs]).astype(o_ref.dtype)

    xs_ref[0:SUBLANES, :] = xs_ref[lr:lr + SUBLANES, :]

    @pl.when(c == pl.num_programs(1) - 1)
    def _():
        st_out_ref[0] = st_ref[...]


def _ssd(z_all, xbc_all, dt3, dtt3, conv0p, ssm0, conv_w, conv_b, a_log, dsk, g_ssm, eh,
         row0, nseq, nchunk, lc, lr, groups):
    ci = z_all.shape[1]
    cc = xbc_all.shape[1]
    nh = ci // M_HEADDIM
    rb0 = row0 // lr
    rowblk = lambda w: pl.BlockSpec((lr, w), lambda b, c: (rb0 + b * nchunk + c, 0))
    seq3 = lambda a, b_: pl.BlockSpec((1, a, b_), lambda b, c: (b * nchunk + c, 0, 0))
    perb = lambda a, b_: pl.BlockSpec((1, a, b_), lambda b, c: (b, 0, 0))
    const = lambda a, b_: pl.BlockSpec((a, b_), lambda b, c: (0, 0))
    return pl.pallas_call(
        functools.partial(_ssd_kernel, lc=lc, lr=lr, nh=nh, groups=groups),
        out_shape=(jax.ShapeDtypeStruct((nseq * nchunk * lr, ci), BF16),
                   jax.ShapeDtypeStruct((nseq, ci, D_STATE), F32)),
        grid=(nseq, nchunk),
        in_specs=[rowblk(ci), rowblk(cc), seq3(lc, nh), seq3(nh, lc),
                  perb(SUBLANES, cc), perb(ci, D_STATE),
                  const(CONV_W, cc), const(1, cc), const(1, nh), const(nh, 1),
                  const(1, ci), const(1, ci), const(nh, ci)],
        out_specs=(pl.BlockSpec((lr, ci), lambda b, c: (b * nchunk + c, 0)),
                   perb(ci, D_STATE)),
        scratch_shapes=[pltpu.VMEM((lc + SUBLANES, cc), F32), pltpu.VMEM((ci, D_STATE), F32)],
        compiler_params=_cp("arbitrary", "arbitrary"),
        name="ssd",
    )(z_all, xbc_all, dt3, dtt3, conv0p, ssm0, conv_w, conv_b.reshape(1, cc),
      a_log.reshape(1, nh), a_log.reshape(nh, 1), dsk, g_ssm.reshape(1, ci), eh)


def _mix_kernel(ap_ref, as_ref, mp_ref, ms_ref, ga_ref, gb_ref, wpa_ref, wpb_ref, o_ref, *, npb):
    i = pl.program_id(1)
    a = jnp.where(i < npb, ap_ref[...], as_ref[...])
    m = jnp.where(i < npb, mp_ref[...], ms_ref[...])
    pa = _dot(a, wpa_ref[...])
    pb = _dot(m, wpb_ref[...])
    o_ref[...] = (ga_ref[...].astype(F32) * pa + gb_ref[...].astype(F32) * pb).astype(o_ref.dtype)


def _mix(a_p, a_s, m_p, m_s, gates, w_pa, w_pb):
    tp, hv = a_p.shape
    ts = a_s.shape[0]
    ci = m_p.shape[1]
    d = w_pa.shape[1]
    tm = _pick(math.gcd(tp, ts), (256, 128, 64))
    tn = _pick(d, (1024, 512, 256, 128))
    npb = tp // tm
    nj = d // tn
    pidx = lambda j, i: (jnp.minimum(i, npb - 1), 0)
    sidx = lambda j, i: (jnp.maximum(i - npb, 0), 0)
    return pl.pallas_call(
        functools.partial(_mix_kernel, npb=npb),
        out_shape=jax.ShapeDtypeStruct((tp + ts, d), BF16),
        grid=(nj, (tp + ts) // tm),
        in_specs=[pl.BlockSpec((tm, hv), pidx), pl.BlockSpec((tm, hv), sidx),
                  pl.BlockSpec((tm, ci), pidx), pl.BlockSpec((tm, ci), sidx),
                  pl.BlockSpec((tm, tn), lambda j, i: (i, j)),
                  pl.BlockSpec((tm, tn), lambda j, i: (i, nj + j)),
                  pl.BlockSpec((hv, tn), lambda j, i: (0, j)),
                  pl.BlockSpec((ci, tn), lambda j, i: (0, j))],
        out_specs=pl.BlockSpec((tm, tn), lambda j, i: (i, j)),
        compiler_params=_cp("arbitrary", "arbitrary"),
        name="branch_mix",
    )(a_p, a_s, m_p, m_s, gates, gates, w_pa, w_pb)


def _pack_bf16_pairs(x):
    n = x.shape[1] // 2
    lo = pltpu.bitcast(x[:, :n].astype(BF16).astype(F32), jnp.uint32)
    hi = pltpu.bitcast(x[:, n:].astype(BF16).astype(F32), jnp.uint32)
    return hi | (lo >> 16)


def _unpack_bf16_pairs(u):
    lo = pltpu.bitcast(u << 16, F32)
    hi = pltpu.bitcast(u & jnp.uint32(0xFFFF0000), F32)
    return jnp.concatenate([lo, hi], axis=1)


def _post_kernel(mx_ref, wout_ref, xp_ref, xs_ref, gt1_ref, sc2_ref, sh2_ref, g2_ref, wrh_ref, wrl_ref, br_ref,
                 x1_ref, h2p_ref, ti_ref, tp_ref, h2_ref, *, nexp, npb):
    is_prompt = pl.program_id(0) < npb
    o = _dot(mx_ref[...], wout_ref[...])
    tm = o.shape[0]
    for ch in range(tm // CHUNK):
        rows = slice(ch * CHUNK, (ch + 1) * CHUNK)
        x1 = jnp.where(is_prompt, xp_ref[ch], xs_ref[ch]) + gt1_ref[ch] * o[rows, :]
        x1_ref[rows, :] = x1
        xn = x1 * lax.rsqrt(jnp.mean(x1 * x1, axis=-1, keepdims=True) + EPS)
        h2_ref[rows, :] = xn * g2_ref[...] * (1.0 + sc2_ref[ch]) + sh2_ref[ch]
    h2 = h2_ref[...]
    h2p_ref[...] = _pack_bf16_pairs(h2)
    hh = h2.astype(BF16)
    hl = (h2 - hh.astype(F32)).astype(BF16)
    logits = _dot(hh, wrh_ref[...]) + _dot(hh, wrl_ref[...]) + _dot(hl, wrh_ref[...]) + br_ref[...]
    lane = lax.broadcasted_iota(I32, logits.shape, 1)
    logits = jnp.where(lane < nexp, logits, NEG_BIG)
    vals, idxs = [], []
    for _ in range(TOP_K):
        m = jnp.max(logits, axis=-1, keepdims=True)
        idx = jnp.min(jnp.where(logits == m, lane, LANES), axis=-1, keepdims=True)
        vals.append(m)
        idxs.append(idx)
        logits = jnp.where(lane == idx, 2.0 * NEG_BIG, logits)
    es = [jnp.exp(v - vals[0]) for v in vals]
    den = es[0]
    for e in es[1:]:
        den = den + e
    ti = jnp.zeros(lane.shape, I32)
    tpv = jnp.zeros(lane.shape, F32)
    for k in range(TOP_K):
        ti = jnp.where(lane == k, idxs[k], ti)
        tpv = jnp.where(lane == k, es[k] / den, tpv)
    ti_ref[...] = ti
    tp_ref[...] = tpv


def _post(mixed, w_out, xp3, xs3, gt1, sc2, sh2, g2, wr_hi, wr_lo, br, nexp):
    t, d = mixed.shape
    ncp, ncs = xp3.shape[0], xs3.shape[0]
    nc = _pick(math.gcd(ncp, ncs), (4, 2, 1))
    tm = nc * CHUNK
    npb = ncp // nc
    row = lambda w: pl.BlockSpec((tm, w), lambda i: (i, 0))
    mod = pl.BlockSpec((nc, 1, d), lambda i: (i, 0, 0))
    const = lambda a, b: pl.BlockSpec((a, b), lambda i: (0, 0))
    return pl.pallas_call(
        functools.partial(_post_kernel, nexp=nexp, npb=npb),
        out_shape=(jax.ShapeDtypeStruct((t, d), F32), jax.ShapeDtypeStruct((t, d // 2), jnp.uint32),
                   jax.ShapeDtypeStruct((t, LANES), I32), jax.ShapeDtypeStruct((t, LANES), F32)),
        grid=(t // tm,),
        in_specs=[row(d), const(d, d),
                  pl.BlockSpec((nc, CHUNK, d), lambda i: (jnp.minimum(i, npb - 1), 0, 0)),
                  pl.BlockSpec((nc, CHUNK, d), lambda i: (jnp.maximum(i - npb, 0), 0, 0)),
                  mod, mod, mod, const(1, d),
                  const(d, LANES), const(d, LANES), const(1, LANES)],
        out_specs=(row(d), row(d // 2), row(LANES), row(LANES)),
        scratch_shapes=[pltpu.VMEM((tm, d), F32)],
        compiler_params=_cp("arbitrary"),
        name="post_mix",
    )(mixed, w_out, xp3, xs3, gt1, sc2, sh2, g2.reshape(1, d), wr_hi, wr_lo, br)


def _slots_kernel(ti_ref, dest_ref, meta_ref, run_ref, *, bm, tb):
    ph = pl.program_id(0)
    i = pl.program_id(1)
    eid = lax.broadcasted_iota(I32, (LANES, tb), 0)
    onehots = [jnp.where(eid == ti_ref[k:k + 1, :], 1.0, 0.0) for k in range(TOP_K)]
    osum = onehots[0]
    for oh in onehots[1:]:
        osum = osum + oh
    blk_cnt = jnp.sum(osum, axis=1, keepdims=True)

    @pl.when(jnp.logical_and(ph == 0, i == 0))
    def _():
        run_ref[...] = jnp.zeros(run_ref.shape, F32)

    @pl.when(ph == 0)
    def _():
        run_ref[...] = run_ref[...] + blk_cnt

    @pl.when(jnp.logical_and(ph == 1, i == 0))
    def _():
        cnt = run_ref[...]
        nblk = jnp.ceil(cnt * (1.0 / bm))
        r = lax.broadcasted_iota(I32, (LANES, LANES), 0)
        c = lax.broadcasted_iota(I32, (LANES, LANES), 1)
        lstrict = jnp.where(c < r, 1.0, 0.0).astype(BF16)
        start_blk = _dot(lstrict, jnp.broadcast_to(nblk, (LANES, LANES)).astype(BF16))
        lane = lax.broadcasted_iota(I32, (LANES, LANES), 1)
        meta_ref[...] = jnp.where(lane == 0, cnt, jnp.where(lane == 1, start_blk, 0.0))
        run_ref[...] = start_blk[:, 0:1] * float(bm)

    @pl.when(ph == 1)
    def _():
        r = lax.broadcasted_iota(I32, (tb, tb), 0)
        c = lax.broadcasted_iota(I32, (tb, tb), 1)
        ustrict = jnp.where(r < c, 1.0, 0.0).astype(BF16)
        base = run_ref[...] + _dot(osum.astype(BF16), ustrict)
        for k in range(TOP_K):
            dest_ref[0, k:k + 1, :] = jnp.sum(onehots[k] * base, axis=0, keepdims=True).astype(I32)
        dest_ref[0, TOP_K:, :] = jnp.zeros((SUBLANES - TOP_K, tb), I32)
        run_ref[...] = run_ref[...] + blk_cnt


def _moe_slots(ti_t, bm, tb):
    t = ti_t.shape[1]
    nt = t // tb
    return pl.pallas_call(
        functools.partial(_slots_kernel, bm=bm, tb=tb),
        out_shape=(jax.ShapeDtypeStruct((nt, SUBLANES, tb), I32),
                   jax.ShapeDtypeStruct((LANES, LANES), F32)),
        grid=(2, nt),
        in_specs=[pl.BlockSpec((SUBLANES, tb), lambda ph, i: (0, i))],
        out_specs=(pl.BlockSpec((1, SUBLANES, tb), lambda ph, i: (i * ph, 0, 0)),
                   pl.BlockSpec((LANES, LANES), lambda ph, i: (0, 0))),
        scratch_shapes=[pltpu.VMEM((LANES, 1), F32)],
        compiler_params=_cp("arbitrary", "arbitrary"),
        name="moe_slots",
    )(ti_t)


def _dispatch_kernel(zs_ref, idx_ref, h_ref, xg_ref, zbuf_ref, idx_smem, isem, sem, *, nz, bm, tb):
    i = pl.program_id(0)

    def zero_fill(e):
        return pltpu.make_async_copy(zbuf_ref, xg_ref.at[pl.ds(pl.multiple_of(zs_ref[e], bm), bm)], sem)

    @pl.when(i == 0)
    def _():
        zbuf_ref[...] = jnp.zeros(zbuf_ref.shape, zbuf_ref.dtype)
        for e in range(nz):
            pl.when(zs_ref[e] >= 0)(lambda e=e: zero_fill(e).start())
        for e in range(nz):
            pl.when(zs_ref[e] >= 0)(lambda e=e: zero_fill(e).wait())

    icp = pltpu.make_async_copy(idx_ref.at[0], idx_smem, isem)
    icp.start()
    icp.wait()

    def start(r, carry):
        for k in range(TOP_K):
            pltpu.make_async_copy(h_ref.at[pl.ds(r, 1)], xg_ref.at[pl.ds(idx_smem[k, r], 1)],
                                  sem).start(priority=k % 2)
        return carry

    lax.fori_loop(0, tb, start, 0, unroll=True)
    for k in range(TOP_K):
        pltpu.make_async_copy(h_ref, xg_ref.at[pl.ds(0, tb)], sem).wait()


def _moe_dispatch(zstart, dest3, h2, n_slots, bm):
    nt, _, tb = dest3.shape
    d = h2.shape[1]
    return pl.pallas_call(
        functools.partial(_dispatch_kernel, nz=zstart.shape[0], bm=bm, tb=tb),
        out_shape=jax.ShapeDtypeStruct((n_slots, d), h2.dtype),
        grid_spec=pltpu.PrefetchScalarGridSpec(
            num_scalar_prefetch=1,
            grid=(nt,),
            in_specs=[pl.BlockSpec((1, SUBLANES, tb), lambda i, zs: (i, 0, 0)),
                      pl.BlockSpec((tb, d), lambda i, zs: (i, 0))],
            out_specs=pl.BlockSpec(memory_space=pl.ANY),
            scratch_shapes=[pltpu.VMEM((bm, d), h2.dtype), pltpu.SMEM((SUBLANES, tb), I32),
                            pltpu.SemaphoreType.DMA, pltpu.SemaphoreType.DMA]),
        compiler_params=_cp("arbitrary"),
        name="moe_dispatch",
    )(zstart, dest3, h2)


def _weight_stream(w_refs, wbuf_ref, wb_refs, sem, cnt_ref, be_ref, nx_ref, used, tw):
    c = pl.program_id(0)
    b = pl.program_id(1)
    nc = pl.num_programs(0)

    def fetch(e, cc, slot):
        col = pl.ds(pl.multiple_of(cc * tw, tw), tw)
        return [pltpu.make_async_copy(w.at[e, :, col], wbuf_ref.at[slot, m], sem.at[slot])
                for m, w in enumerate(w_refs)]

    @pl.when(jnp.logical_and(c == 0, b == 0))
    def _():
        cnt_ref[0] = 0
        for cp in fetch(be_ref[0], 0, 0):
            cp.start(priority=WEIGHT_DMA_PRIORITY)

    changed = jnp.logical_or(b == 0, be_ref[b] != be_ref[jnp.maximum(b - 1, 0)])

    @pl.when(jnp.logical_and(used, changed))
    def _():
        slot = cnt_ref[0] & 1
        for cp in fetch(be_ref[b], c, slot):
            cp.wait()
        for m, wb in enumerate(wb_refs):
            wb[...] = wbuf_ref[slot, m].astype(BF16)
        nxt = nx_ref[b]
        same_chunk = nxt >= 0
        nxt_e = jnp.where(same_chunk, nxt, be_ref[0])
        nxt_c = jnp.where(same_chunk, c, c + 1)

        @pl.when(jnp.logical_or(same_chunk, c + 1 < nc))
        def _():
            for cp in fetch(nxt_e, nxt_c, 1 - slot):
                cp.start(priority=WEIGHT_DMA_PRIORITY)

        cnt_ref[0] = cnt_ref[0] + 1


def _gateup_kernel(be_ref, nx_ref, nu_ref, x_ref, bg_ref, bu_ref, wg_ref, wu_ref, o_ref,
                   wbuf_ref, wgb_ref, wub_ref, sem, cnt_ref, *, tf):
    used = pl.program_id(1) < nu_ref[0]
    _weight_stream([wg_ref, wu_ref], wbuf_ref, [wgb_ref, wub_ref], sem, cnt_ref, be_ref, nx_ref, used, tf)

    @pl.when(used)
    def _():
        x = _unpack_bf16_pairs(x_ref[...]).astype(BF16)
        gate = jnp.minimum(_dot(x, wgb_ref[...]) + bg_ref[0], SWIGLU_LIMIT)
        up = jnp.clip(_dot(x, wub_ref[...]) + bu_ref[0], -SWIGLU_LIMIT, SWIGLU_LIMIT)
        glu = gate * jax.nn.sigmoid(SWIGLU_ALPHA * gate)
        o_ref[...] = ((up + 1.0) * glu).astype(o_ref.dtype)

    @pl.when(jnp.logical_not(used))
    def _():
        o_ref[...] = jnp.zeros(o_ref.shape, o_ref.dtype)


def _moe_gateup(block_e, next_e, n_used, xg, w_gate, w_up, b_gate, b_up):
    ns, dh = xg.shape
    nexp, d, ff = w_gate.shape
    assert d == 2 * dh
    bm = MOE_BLOCK
    nb = ns // bm
    tf = _pick(ff, (1024, 512, 256, 128))
    bspec = pl.BlockSpec((1, 1, tf), lambda c, b, be, nx, nu: (be[b], 0, c))
    hbm = pl.BlockSpec(memory_space=pl.ANY)
    return pl.pallas_call(
        functools.partial(_gateup_kernel, tf=tf),
        out_shape=jax.ShapeDtypeStruct((ns, ff), BF16),
        grid_spec=pltpu.PrefetchScalarGridSpec(
            num_scalar_prefetch=3,
            grid=(ff // tf, nb),
            in_specs=[pl.BlockSpec((bm, dh), lambda c, b, be, nx, nu: (jnp.minimum(b, nu[0] - 1), 0)),
                      bspec, bspec, hbm, hbm],
            out_specs=pl.BlockSpec((bm, tf), lambda c, b, be, nx, nu: (b, c)),
            scratch_shapes=[pltpu.VMEM((2, 2, d, tf), F32), pltpu.VMEM((d, tf), BF16),
                            pltpu.VMEM((d, tf), BF16), pltpu.SemaphoreType.DMA((2,)),
                            pltpu.SMEM((1,), I32)]),
        compiler_params=_cp("arbitrary", "arbitrary"),
        name="moe_gateup",
    )(block_e, next_e, n_used, xg, b_gate.reshape(nexp, 1, ff), b_up.reshape(nexp, 1, ff), w_gate, w_up)


def _down_kernel(be_ref, nx_ref, nu_ref, g_ref, bd_ref, wd_ref, o_ref, wbuf_ref, wdb_ref, sem, cnt_ref, *, tn):
    used = pl.program_id(1) < nu_ref[0]
    _weight_stream([wd_ref], wbuf_ref, [wdb_ref], sem, cnt_ref, be_ref, nx_ref, used, tn)

    @pl.when(used)
    def _():
        o_ref[...] = _pack_bf16_pairs(_dot(g_ref[...], wdb_ref[...]) + bd_ref[0])

    @pl.when(jnp.logical_not(used))
    def _():
        o_ref[...] = jnp.zeros(o_ref.shape, o_ref.dtype)


def _moe_down(block_e, next_e, n_used, glu, w_down, b_down):
    ns, ff = glu.shape
    nexp, _, d = w_down.shape
    bm = MOE_BLOCK
    nb = ns // bm
    tn = d
    return pl.pallas_call(
        functools.partial(_down_kernel, tn=tn),
        out_shape=jax.ShapeDtypeStruct((ns, d // 2), jnp.uint32),
        grid_spec=pltpu.PrefetchScalarGridSpec(
            num_scalar_prefetch=3,
            grid=(d // tn, nb),
            in_specs=[pl.BlockSpec((bm, ff), lambda c, b, be, nx, nu: (jnp.minimum(b, nu[0] - 1), 0)),
                      pl.BlockSpec((1, 1, tn), lambda c, b, be, nx, nu: (be[b], 0, c)),
                      pl.BlockSpec(memory_space=pl.ANY)],
            out_specs=pl.BlockSpec((bm, tn // 2), lambda c, b, be, nx, nu: (b, c)),
            scratch_shapes=[pltpu.VMEM((2, 1, ff, tn), F32), pltpu.VMEM((ff, tn), BF16),
                            pltpu.SemaphoreType.DMA((2,)), pltpu.SMEM((1,), I32)]),
        compiler_params=_cp("arbitrary", "arbitrary"),
        name="moe_down",
    )(block_e, next_e, n_used, glu, b_down.reshape(nexp, 1, d), w_down)


def _combine_kernel(idx_ref, yb_ref, x1_ref, gt2_ref, p_ref, o_ref, buf_ref, idx_smem, isem, sem, *, tc):
    icp = pltpu.make_async_copy(idx_ref.at[0], idx_smem, isem)
    icp.start()
    icp.wait()

    def start(r, carry):
        for k in range(TOP_K):
            pltpu.make_async_copy(yb_ref.at[pl.ds(idx_smem[k, r], 1)],
                                  buf_ref.at[pl.ds(k * tc + r, 1)], sem).start(priority=k % 2)
        return carry

    lax.fori_loop(0, tc, start, 0, unroll=True)
    pltpu.make_async_copy(yb_ref.at[pl.ds(0, TOP_K * tc)], buf_ref, sem).wait()
    ff = p_ref[:, 0:1] * _unpack_bf16_pairs(buf_ref[0:tc, :])
    for k in range(1, TOP_K):
        ff = ff + p_ref[:, k:k + 1] * _unpack_bf16_pairs(buf_ref[k * tc:(k + 1) * tc, :])
    for ch in range(tc // CHUNK):
        rows = slice(ch * CHUNK, (ch + 1) * CHUNK)
        o_ref[rows, :] = x1_ref[rows, :] + gt2_ref[ch] * ff[rows, :]


def _moe_combine(dest3, yb, x1, gt2, top_p, row0, nrows):
    d = 2 * yb.shape[1]
    tc = dest3.shape[2]
    nc = tc // CHUNK
    rb0 = row0 // tc
    return pl.pallas_call(
        functools.partial(_combine_kernel, tc=tc),
        out_shape=jax.ShapeDtypeStruct((nrows, d), F32),
        grid=(nrows // tc,),
        in_specs=[pl.BlockSpec((1, SUBLANES, tc), lambda i: (rb0 + i, 0, 0)),
                  pl.BlockSpec(memory_space=pl.ANY),
                  pl.BlockSpec((tc, d), lambda i: (rb0 + i, 0)),
                  pl.BlockSpec((nc, 1, d), lambda i: (rb0 + i, 0, 0)),
                  pl.BlockSpec((tc, LANES), lambda i: (rb0 + i, 0))],
        out_specs=pl.BlockSpec((tc, d), lambda i: (i, 0)),
        scratch_shapes=[pltpu.VMEM((TOP_K * tc, d // 2), jnp.uint32), pltpu.SMEM((SUBLANES, tc), I32),
                        pltpu.SemaphoreType.DMA, pltpu.SemaphoreType.DMA],
        compiler_params=_cp("arbitrary"),
        name="moe_combine",
    )(dest3, yb, x1, gt2, top_p)


def _block_layout(meta, nexp, bm, n_blocks):
    counts = meta[:nexp, 0]
    start_blk = meta[:nexp, 1].astype(I32)
    nblk = jnp.ceil(counts * (1.0 / bm)).astype(I32)
    end_blk = start_blk + nblk
    n_used = jnp.sum(nblk)
    blk = jnp.minimum(jnp.arange(n_blocks), n_used - 1)
    block_e = jnp.minimum(jnp.sum(end_blk[None, :] <= blk[:, None], axis=1), nexp - 1).astype(I32)
    onehot_e = block_e[:, None] == jnp.arange(nexp)[None, :]
    run_end = jnp.sum(jnp.where(onehot_e, end_blk[None, :], 0), axis=1)
    at_end = run_end[:, None] == jnp.arange(n_blocks)[None, :]
    next_e = jnp.where(run_end < n_used, jnp.sum(jnp.where(at_end, block_e[None, :], 0), axis=1), -1).astype(I32)
    last = jnp.where(nblk > 0, (end_blk - 1) * bm, -1)
    spare = n_used + jnp.arange(nexp)
    spare = jnp.where(spare < n_blocks, spare * bm, -1)
    zstart = jnp.concatenate([last, spare]).astype(I32)
    return block_e, next_e, n_used.reshape(1).astype(I32), zstart


def _rope_tables(pos):
    half = ROPE_DIM // 2
    inv = ROPE_THETA ** (-jnp.arange(half, dtype=F32) / half)
    ang = pos.astype(F32)[:, None] * inv[None, :]
    z = jnp.zeros((pos.shape[0], LANES - ROPE_DIM), F32)
    cos, sin = jnp.cos(ang), jnp.sin(ang)
    return jnp.concatenate([cos, cos, z], axis=1), jnp.concatenate([sin, sin, z], axis=1)


def _rot_half_cols(w):
    half = ROPE_DIM // 2
    return jnp.concatenate([-w[..., half:], w[..., :half]], axis=-1)


def _layer(x_prompt, x_sample, past_lat, past_kr, ssm_s0, conv_s0, c_prompt, c_sample,
           w_ada, b_ada, g_norm1, w_in, g_cq, g_ckv, w_uq, w_uk, w_uv, g_qn, g_kn, conv_w, conv_b, dt_bias,
           a_log, d_skip, g_ssm, w_pa, w_pb, w_out, g_norm2, w_router, b_router, w_gate, b_gate, w_up, b_up,
           w_down, b_down):
    bp, sp, d = x_prompt.shape
    bs, ss, _ = x_sample.shape
    assert ss == CHUNK and sp % CHUNK == 0
    past = past_lat.shape[1]
    ql, kvl = g_cq.shape[-1], g_ckv.shape[-1]
    heads = w_uq.shape[1]
    ci = g_ssm.shape[-1]
    cc = conv_w.shape[-1]
    nh = ci // M_HEADDIM
    groups = (cc - ci) // (2 * D_STATE)
    nexp = w_router.shape[-1]
    tp, ts = bp * sp, bs * ss
    t = tp + ts
    ncp, ncs = tp // CHUNK, ts // CHUNK

    ada = _ada(jnp.concatenate([c_prompt, c_sample], axis=0), w_ada, b_ada)
    per_chunk = jnp.concatenate([jnp.repeat(ada[:bp], sp // CHUNK, axis=0), ada[bp:]], axis=0)
    sh1, sc1, gt1, sh2, sc2, gt2 = [m[:, None, :] for m in jnp.split(per_chunk, 6, axis=-1)]

    xp3 = x_prompt.reshape(ncp, CHUNK, d)
    xs3 = x_sample.reshape(ncs, CHUNK, d)
    h3 = _norm1(xp3, xs3, g_norm1, sc1, sh1)
    h_all = h3.reshape(t, d)

    o = 0
    w_cq = w_in[:, o:o + ql]; o += ql
    w_ckv = w_in[:, o:o + kvl]; o += kvl
    w_kr = w_in[:, o:o + ROPE_DIM]; o += ROPE_DIM
    w_z = w_in[:, o:o + ci]; o += ci
    w_xbc = w_in[:, o:o + cc]; o += cc
    w_dt = w_in[:, o:o + nh]; o += nh
    w_gab = w_in[:, o:o + 2 * d]
    zc = lambda n_: jnp.zeros((d, n_), F32)
    w_lat = jnp.concatenate([w_cq, w_ckv, w_kr, zc(LANES - ROPE_DIM), _rot_half_cols(w_kr),
                             zc(LANES - ROPE_DIM), w_dt, zc(LANES - nh)], axis=1).astype(BF16)

    cos_p, sin_p = _rope_tables(jnp.arange(sp))
    cos_s, sin_s = _rope_tables(past + jnp.arange(ss))
    cos128 = jnp.concatenate([jnp.tile(cos_p, (bp, 1)), jnp.tile(cos_s, (bs, 1))], axis=0)
    sin128 = jnp.concatenate([jnp.tile(sin_p, (bp, 1)), jnp.tile(sin_s, (bs, 1))], axis=0)

    cqn, lat_all, latk, kr_all, dt_all, dtt_all = _lat(
        h_all, w_lat, w_dt.T.astype(BF16), g_cq, g_ckv, cos128, sin128, dt_bias)
    z_all = _mm(h_all, w_z.astype(BF16), BF16, "proj_z")
    xbc_all = _mm(h_all, w_xbc.astype(BF16), BF16, "proj_xbc")
    gates = _mm(h_all, w_gab.astype(BF16), BF16, "proj_gates", act="sigmoid")

    tail = CONV_W - 1
    h_tail = jnp.concatenate([h3[:ncp].reshape(bp, sp, d)[:, sp - tail:, :].reshape(bp * tail, d),
                              h3[ncp:][:, CHUNK - tail:, :].reshape(bs * tail, d)], axis=0)
    conv_tail = _mm(h_tail, w_xbc.astype(BF16), F32, "proj_conv_tail")
    conv_p = conv_tail[:bp * tail].reshape(bp, tail, cc)
    conv_s = conv_tail[bp * tail:].reshape(bs, tail, cc)

    wq_a = jnp.concatenate([w_uq, jnp.zeros((ql, heads, HEAD_PAD - QK_DIM), F32)], axis=-1)
    wq_a = wq_a.reshape(ql, heads * HEAD_PAD).astype(BF16)
    wq_b = jnp.concatenate([_rot_half_cols(w_uq[..., NOPE_DIM:]),
                            jnp.zeros((ql, heads, LANES - ROPE_DIM), F32)], axis=-1)
    wq_b = wq_b.reshape(ql, heads * LANES).astype(BF16)
    gq = g_qn * g_kn * (ATTN_SCALE * math.log2(math.e))
    g_nope = gq[:NOPE_DIM].reshape(1, LANES)
    g_rope = jnp.concatenate([gq[NOPE_DIM:], jnp.zeros((LANES - ROPE_DIM,), F32)]).reshape(1, LANES)
    q_all = _qproj(cqn, wq_a, wq_b, cos128, sin128, g_nope, g_rope, heads)

    w_uk2 = w_uk.reshape(kvl, heads * NOPE_DIM)
    w_uv2 = w_uv.reshape(kvl, heads * V_DIM).astype(BF16)
    w_kv = jnp.concatenate([w_uk2.astype(BF16), w_uv2], axis=1)
    k_p, v_p = _kvproj(latk, w_kv, tp, heads, kvl)
    a_p = _attn_prompt(q_all, k_p, v_p, bp, sp, heads)
    a_s = _attn_sample(past_lat, past_kr, latk, q_all, w_uk2.T.astype(BF16), w_uv2, tp, heads)

    eh = jnp.repeat(jnp.eye(nh, dtype=BF16), M_HEADDIM, axis=1)
    dsk = jnp.repeat(d_skip, M_HEADDIM).reshape(1, ci)
    lcp = _pick(sp, (256, 128))
    assert sp % lcp == 0 and lcp % LANES == 0
    ncq = sp // lcp
    dt3_p = dt_all[:tp].reshape(bp * ncq, lcp, nh)
    dtt3_p = dtt_all[:, :tp].reshape(nh, bp * ncq, lcp).transpose(1, 0, 2)
    lcs = CHUNK
    dt3_s = dt_all[tp:].reshape(bs, CHUNK, nh)
    dtt3_s = dtt_all[:, tp:].reshape(nh, bs, CHUNK).transpose(1, 0, 2)
    pad_conv = lambda c0: jnp.concatenate(
        [jnp.zeros((c0.shape[0], SUBLANES - tail, cc), F32), c0], axis=1)
    m_p, ssm_p = _ssd(z_all, xbc_all, dt3_p, dtt3_p, jnp.zeros((bp, SUBLANES, cc), F32),
                      jnp.zeros((bp, ci, D_STATE), F32), conv_w, conv_b, a_log, dsk, g_ssm, eh,
                      0, bp, ncq, lcp, lcp, groups)
    m_s, ssm_s = _ssd(z_all, xbc_all, dt3_s, dtt3_s, pad_conv(conv_s0),
                      ssm_s0.reshape(bs, ci, D_STATE), conv_w, conv_b, a_log, dsk, g_ssm, eh,
                      tp, bs, 1, lcs, CHUNK, groups)

    mixed = _mix(a_p, a_s, m_p, m_s, gates, w_pa.astype(BF16), w_pb.astype(BF16))
    wr = jnp.concatenate([w_router, jnp.zeros((d, LANES - nexp), F32)], axis=1)
    wr_hi = wr.astype(BF16)
    wr_lo = (wr - wr_hi.astype(F32)).astype(BF16)
    br = jnp.concatenate([b_router, jnp.zeros((LANES - nexp,), F32)]).reshape(1, LANES)
    x1, h2, ti, tpr = _post(mixed, w_out.astype(BF16), xp3, xs3, gt1, sc2, sh2, g_norm2, wr_hi, wr_lo, br, nexp)

    assert nexp <= LANES
    bm = MOE_BLOCK
    n_blocks = -(-(t * TOP_K + nexp * (bm - 1)) // bm)
    tb = _pick(math.gcd(tp, ts), (256, 128))
    dest3, meta = _moe_slots(ti[:, :SUBLANES].T, bm, tb)
    block_e, next_e, n_used, zstart = _block_layout(meta, nexp, bm, n_blocks)
    xg = _moe_dispatch(zstart, dest3, h2, n_blocks * bm, bm)
    glu = _moe_gateup(block_e, next_e, n_used, xg, w_gate, w_up, b_gate, b_up)
    yb = _moe_down(block_e, next_e, n_used, glu, w_down, b_down)
    y_p = _moe_combine(dest3, yb, x1, gt2, tpr, 0, tp)
    y_s = _moe_combine(dest3, yb, x1, gt2, tpr, tp, ts)

    return (y_p.reshape(bp, sp, d), y_s.reshape(bs, ss, d),
            lat_all[:tp].reshape(bp, sp, kvl), kr_all[:tp].reshape(bp, sp, ROPE_DIM),
            ssm_p.reshape(bp, nh, M_HEADDIM, D_STATE), conv_p,
            lat_all[tp:].reshape(bs, ss, kvl), kr_all[tp:].reshape(bs, ss, ROPE_DIM),
            ssm_s.reshape(bs, nh, M_HEADDIM, D_STATE), conv_s)


def kernel(x_prompt, x_sample, cache_mla_latent, cache_mla_krope, state_ssm, state_conv, c_prompt, c_sample,
           w_ada, b_ada, g_norm1, w_in, g_cq, g_ckv, w_uq, w_uk, w_uv, g_qn, g_kn, conv_w, conv_b, dt_bias,
           a_log, d_skip, g_ssm, w_pa, w_pb, w_out, g_norm2, w_router, b_router, w_gate, b_gate, w_up, b_up,
           w_down, b_down):
    depth = w_ada.shape[0]
    assert depth == 1, "single-layer encoder"
    weights = (w_ada, b_ada, g_norm1, w_in, g_cq, g_ckv, w_uq, w_uk, w_uv, g_qn, g_kn, conv_w, conv_b, dt_bias,
               a_log, d_skip, g_ssm, w_pa, w_pb, w_out, g_norm2, w_router, b_router, w_gate, b_gate, w_up, b_up,
               w_down, b_down)
    drop = lambda a: a.reshape(a.shape[1:])
    outs = _layer(x_prompt, x_sample, drop(cache_mla_latent), drop(cache_mla_krope), drop(state_ssm),
                  drop(state_conv), c_prompt, c_sample, *[drop(w) for w in weights])
    return outs[:2] + tuple(o.reshape((1,) + o.shape) for o in outs[2:])
```

```python
import functools
import math

import jax
import jax.numpy as jnp
from jax import lax
from jax.experimental import pallas as pl
from jax.experimental.pallas import tpu as pltpu

F32 = jnp.float32
BF16 = jnp.bfloat16
I32 = jnp.int32

CHUNK = 64
NOPE_DIM = 128
ROPE_DIM = 64
QK_DIM = NOPE_DIM + ROPE_DIM
V_DIM = 128
HEAD_PAD = 256
ROPE_THETA = 10000.0
ATTN_SCALE = QK_DIM ** -0.5
M_HEADDIM = 64
D_STATE = 128
CONV_W = 4
TOP_K = 4
SWIGLU_LIMIT = 7.0
SWIGLU_ALPHA = 1.702
EPS = 1e-6

LANES = 128
SUBLANES = 8
VMEM_LIMIT = 56 * 1024 * 1024

MOE_BLOCK = 256
WEIGHT_DMA_PRIORITY = 1
NEG_BIG = -1e30

_NT = (((1,), (1,)), ((), ()))
_TN = (((0,), (0,)), ((), ()))


def _cp(*sem):
    return pltpu.CompilerParams(dimension_semantics=sem, vmem_limit_bytes=VMEM_LIMIT)


def _pick(n, prefs):
    for p in prefs:
        if n % p == 0:
            return p
    return n


def _dot(a, b):
    return jnp.dot(a, b, preferred_element_type=F32)


def _split3(v):
    hi = v.astype(BF16)
    r1 = v - hi.astype(F32)
    mid = r1.astype(BF16)
    lo = (r1 - mid.astype(F32)).astype(BF16)
    return hi, mid, lo


def _silu(x):
    return x * jax.nn.sigmoid(x)


def _softplus(x):
    return jnp.maximum(x, 0.0) + jnp.log1p(jnp.exp(-jnp.abs(x)))


def _ada_kernel(c_ref, w_ref, b_ref, o_ref):
    s = _silu(c_ref[...]).astype(BF16)
    o_ref[...] = _dot(s, w_ref[...].astype(BF16)) + b_ref[...]


def _ada(c_all, w_ada, b_ada):
    r, d = c_all.shape
    n = w_ada.shape[1]
    tn = _pick(n, (1024, 512, 256, 128))
    return pl.pallas_call(
        _ada_kernel,
        out_shape=jax.ShapeDtypeStruct((r, n), F32),
        grid=(n // tn,),
        in_specs=[pl.BlockSpec((r, d), lambda j: (0, 0)),
                  pl.BlockSpec((d, tn), lambda j: (0, j)),
                  pl.BlockSpec((1, tn), lambda j: (0, j))],
        out_specs=pl.BlockSpec((r, tn), lambda j: (0, j)),
        compiler_params=_cp("arbitrary"),
        name="ada",
    )(c_all, w_ada, b_ada.reshape(1, n))


def _norm1_kernel(xp_ref, xs_ref, g_ref, sc_ref, sh_ref, h_ref, *, npb):
    i = pl.program_id(0)
    x = jnp.where(i < npb, xp_ref[...], xs_ref[...])
    xn = x * lax.rsqrt(jnp.mean(x * x, axis=-1, keepdims=True) + EPS)
    h_ref[...] = (xn * g_ref[...] * (1.0 + sc_ref[...]) + sh_ref[...]).astype(BF16)


def _norm1(xp3, xs3, g, sc, sh):
    ncp, _, d = xp3.shape
    ncs = xs3.shape[0]
    nch = ncp + ncs
    gc = _pick(math.gcd(ncp, ncs), (4, 2, 1))
    npb = ncp // gc
    blk = (gc, CHUNK, d)
    mod = pl.BlockSpec((gc, 1, d), lambda i: (i, 0, 0))
    return pl.pallas_call(
        functools.partial(_norm1_kernel, npb=npb),
        out_shape=jax.ShapeDtypeStruct((nch, CHUNK, d), BF16),
        grid=(nch // gc,),
        in_specs=[pl.BlockSpec(blk, lambda i: (jnp.minimum(i, npb - 1), 0, 0)),
                  pl.BlockSpec(blk, lambda i: (jnp.maximum(i - npb, 0), 0, 0)),
                  pl.BlockSpec((1, 1, d), lambda i: (0, 0, 0)),
                  mod, mod],
        out_specs=pl.BlockSpec(blk, lambda i: (i, 0, 0)),
        compiler_params=_cp("arbitrary"),
        name="norm1",
    )(xp3, xs3, g.reshape(1, 1, d), sc, sh)


def _mm_kernel(x_ref, w_ref, o_ref, *, act):
    acc = _dot(x_ref[...], w_ref[...])
    if act == "sigmoid":
        acc = jax.nn.sigmoid(acc)
    o_ref[...] = acc.astype(o_ref.dtype)


def _mm(x, w, out_dtype, name, act=None):
    m, k = x.shape
    n = w.shape[1]
    tm = _pick(m, (1024, 512, 256))
    tn = _pick(n, (1024, 512, 256, 128))
    return pl.pallas_call(
        functools.partial(_mm_kernel, act=act),
        out_shape=jax.ShapeDtypeStruct((m, n), out_dtype),
        grid=(m // tm, n // tn),
        in_specs=[pl.BlockSpec((tm, k), lambda i, j: (i, 0)),
                  pl.BlockSpec((k, tn), lambda i, j: (0, j))],
        out_specs=pl.BlockSpec((tm, tn), lambda i, j: (i, j)),
        compiler_params=_cp("arbitrary", "arbitrary"),
        name=name,
    )(x, w)


def _lat_kernel(h_ref, w_ref, wdt_ref, gcq_ref, gckv_ref, cos_ref, sin_ref, dtb_ref, dtbc_ref,
                cqn_ref, lat_ref, latk_ref, kr_ref, dt_ref, dtt_ref, *, ql, kvl):
    h = h_ref[...]
    acc = _dot(h, w_ref[...])
    cq = acc[:, :ql]
    cqn = cq * lax.rsqrt(jnp.mean(cq * cq, axis=-1, keepdims=True) + EPS) * gcq_ref[...]
    cqn_ref[...] = cqn.astype(BF16)
    ckv = acc[:, ql:ql + kvl]
    lat = ckv * lax.rsqrt(jnp.mean(ckv * ckv, axis=-1, keepdims=True) + EPS) * gckv_ref[...]
    lat_ref[...] = lat
    o = ql + kvl
    kr128 = acc[:, o:o + LANES] * cos_ref[...] + acc[:, o + LANES:o + 2 * LANES] * sin_ref[...]
    kr_ref[...] = kr128[:, :ROPE_DIM]
    latk_ref[:, :kvl] = lat.astype(BF16)
    latk_ref[:, kvl:] = kr128.astype(BF16)
    nh = dt_ref.shape[-1]
    dt_ref[...] = _softplus(acc[:, o + 2 * LANES:o + 2 * LANES + nh] + dtb_ref[...])
    dtt = lax.dot_general(wdt_ref[...], h, _NT, preferred_element_type=F32)
    dtt_ref[...] = _softplus(dtt + dtbc_ref[...])


def _lat(h_all, w_lat, w_dt_t, g_cq, g_ckv, cos128, sin128, dt_bias):
    t, d = h_all.shape
    ql, kvl = g_cq.shape[-1], g_ckv.shape[-1]
    nh = dt_bias.shape[-1]
    nl = w_lat.shape[1]
    tm = _pick(t, (512, 256, 128))
    row = lambda w: pl.BlockSpec((tm, w), lambda i: (i, 0))
    const = lambda a, b: pl.BlockSpec((a, b), lambda i: (0, 0))
    return pl.pallas_call(
        functools.partial(_lat_kernel, ql=ql, kvl=kvl),
        out_shape=(jax.ShapeDtypeStruct((t, ql), BF16),
                   jax.ShapeDtypeStruct((t, kvl), F32),
                   jax.ShapeDtypeStruct((t, kvl + LANES), BF16),
                   jax.ShapeDtypeStruct((t, ROPE_DIM), F32),
                   jax.ShapeDtypeStruct((t, nh), F32),
                   jax.ShapeDtypeStruct((nh, t), F32)),
        grid=(t // tm,),
        in_specs=[row(d), const(d, nl), const(nh, d), const(1, ql), const(1, kvl),
                  row(LANES), row(LANES), const(1, nh), const(nh, 1)],
        out_specs=(row(ql), row(kvl), row(kvl + LANES), row(ROPE_DIM), row(nh),
                   pl.BlockSpec((nh, tm), lambda i: (0, i))),
        compiler_params=_cp("arbitrary"),
        name="latent_proj",
    )(h_all, w_lat, w_dt_t, g_cq.reshape(1, ql), g_ckv.reshape(1, kvl), cos128, sin128,
      dt_bias.reshape(1, nh), dt_bias.reshape(nh, 1))


def _q_kernel(c_ref, wa_ref, wb_ref, cos_ref, sin_ref, gn_ref, gr_ref, o_ref, *, heads):
    c = c_ref[...]
    a = _dot(c, wa_ref[...])
    b = _dot(c, wb_ref[...])
    cos, sin = cos_ref[...], sin_ref[...]
    for h in range(heads):
        nope = a[:, h * HEAD_PAD:h * HEAD_PAD + LANES]
        rope = a[:, h * HEAD_PAD + LANES:(h + 1) * HEAD_PAD] * cos + b[:, h * LANES:(h + 1) * LANES] * sin
        ss = (jnp.sum(nope * nope, axis=-1, keepdims=True)
              + jnp.sum(rope * rope, axis=-1, keepdims=True)) * (1.0 / QK_DIM)
        r = lax.rsqrt(ss + EPS)
        o_ref[:, h * HEAD_PAD:h * HEAD_PAD + LANES] = (nope * r * gn_ref[...]).astype(BF16)
        o_ref[:, h * HEAD_PAD + LANES:(h + 1) * HEAD_PAD] = (rope * r * gr_ref[...]).astype(BF16)


def _qproj(cqn, wq_a, wq_b, cos128, sin128, g_nope, g_rope, heads):
    t, ql = cqn.shape
    tm = _pick(t, (256, 128))
    row = lambda w: pl.BlockSpec((tm, w), lambda i: (i, 0))
    const = lambda a, b: pl.BlockSpec((a, b), lambda i: (0, 0))
    return pl.pallas_call(
        functools.partial(_q_kernel, heads=heads),
        out_shape=jax.ShapeDtypeStruct((t, heads * HEAD_PAD), BF16),
        grid=(t // tm,),
        in_specs=[row(ql), const(ql, heads * HEAD_PAD), const(ql, heads * LANES),
                  row(LANES), row(LANES), const(1, LANES), const(1, LANES)],
        out_specs=row(heads * HEAD_PAD),
        compiler_params=_cp("arbitrary"),
        name="q_proj",
    )(cqn, wq_a, wq_b, cos128, sin128, g_nope, g_rope)


def _kv_kernel(lat_ref, kr_ref, w_ref, k_ref, v_ref, *, heads):
    acc = _dot(lat_ref[...], w_ref[...])
    kr = kr_ref[...].astype(F32)
    kr2 = jnp.sum(kr * kr, axis=-1, keepdims=True)
    for h in range(heads):
        kn = acc[:, h * LANES:(h + 1) * LANES]
        ss = (jnp.sum(kn * kn, axis=-1, keepdims=True) + kr2) * (1.0 / QK_DIM)
        r = lax.rsqrt(ss + EPS)
        k_ref[:, h * HEAD_PAD:h * HEAD_PAD + LANES] = (kn * r).astype(BF16)
        k_ref[:, h * HEAD_PAD + LANES:(h + 1) * HEAD_PAD] = (kr * r).astype(BF16)
    v_ref[...] = acc[:, heads * LANES:].astype(BF16)


def _kvproj(latk, w_kv, tp, heads, kvl):
    tm = _pick(tp, (256, 128))
    return pl.pallas_call(
        functools.partial(_kv_kernel, heads=heads),
        out_shape=(jax.ShapeDtypeStruct((tp, heads * HEAD_PAD), BF16),
                   jax.ShapeDtypeStruct((tp, heads * V_DIM), BF16)),
        grid=(tp // tm,),
        in_specs=[pl.BlockSpec((tm, kvl), lambda i: (i, 0)),
                  pl.BlockSpec((tm, LANES), lambda i: (i, kvl // LANES)),
                  pl.BlockSpec((kvl, 2 * heads * LANES), lambda i: (0, 0))],
        out_specs=(pl.BlockSpec((tm, heads * HEAD_PAD), lambda i: (i, 0)),
                   pl.BlockSpec((tm, heads * V_DIM), lambda i: (i, 0))),
        compiler_params=_cp("arbitrary"),
        name="kv_proj",
    )(latk, latk, w_kv)


def _attn_p_kernel(q_ref, k_ref, v_ref, o_ref, m_ref, l_ref, acc_ref, *, tq, hg):
    qi = pl.program_id(2)
    m_ref[...] = jnp.full(m_ref.shape, -jnp.inf, F32)
    l_ref[...] = jnp.zeros(l_ref.shape, F32)
    acc_ref[...] = jnp.zeros(acc_ref.shape, F32)

    def tile(j, masked):
        ks = pl.multiple_of(j * tq, tq)
        for g in range(hg):
            q = q_ref[:, g * HEAD_PAD:(g + 1) * HEAD_PAD]
            k = k_ref[pl.ds(ks, tq), g * HEAD_PAD:(g + 1) * HEAD_PAD]
            v = v_ref[pl.ds(ks, tq), g * V_DIM:(g + 1) * V_DIM]
            s = lax.dot_general(q, k, _NT, preferred_element_type=F32)
            if masked:
                rc = lax.broadcasted_iota(I32, (tq, tq), 0) // CHUNK
                cc = lax.broadcasted_iota(I32, (tq, tq), 1) // CHUNK
                s = jnp.where(cc <= rc, s, -jnp.inf)
            m_prev = m_ref[g]
            m_new = jnp.maximum(m_prev, jnp.max(s, axis=-1, keepdims=True))
            alpha = jnp.exp2(m_prev - m_new)
            p = jnp.exp2(s - jnp.tile(m_new, (1, tq // LANES)))
            l_ref[g] = alpha * l_ref[g] + jnp.sum(p, axis=-1, keepdims=True)
            acc_ref[g] = alpha * acc_ref[g] + _dot(p.astype(BF16), v)
            m_ref[g] = m_new

    def body(j, carry):
        tile(j, False)
        return carry

    lax.fori_loop(0, qi, body, 0)
    tile(qi, True)
    for g in range(hg):
        o_ref[:, g * V_DIM:(g + 1) * V_DIM] = (acc_ref[g] / l_ref[g]).astype(o_ref.dtype)


def _attn_prompt(q_all, k_p, v_p, batch, seq, heads):
    tq = _pick(seq, (512, 256, 128, 64))
    nq = seq // tq
    hg = _pick(heads, (4, 2, 1))
    return pl.pallas_call(
        functools.partial(_attn_p_kernel, tq=tq, hg=hg),
        out_shape=jax.ShapeDtypeStruct((batch * seq, heads * V_DIM), BF16),
        grid=(batch, heads // hg, nq),
        in_specs=[pl.BlockSpec((tq, hg * HEAD_PAD), lambda b, h, qi: (b * nq + qi, h)),
                  pl.BlockSpec((seq, hg * HEAD_PAD), lambda b, h, qi: (b, h)),
                  pl.BlockSpec((seq, hg * V_DIM), lambda b, h, qi: (b, h))],
        out_specs=pl.BlockSpec((tq, hg * V_DIM), lambda b, h, qi: (b * nq + qi, h)),
        scratch_shapes=[pltpu.VMEM((hg, tq, LANES), F32), pltpu.VMEM((hg, tq, LANES), F32),
                        pltpu.VMEM((hg, tq, V_DIM), F32)],
        compiler_params=_cp("arbitrary", "arbitrary", "arbitrary"),
        name="attn_prompt",
    )(q_all, k_p, v_p)


def _attn_s_kernel(pl_ref, pk_ref, nl_ref, q_ref, wukt_ref, wuv_ref, o_ref,
                   qabs_ref, qr_ref, m_ref, l_ref, acc_ref, *, heads, kvl, nkb):
    kb = pl.program_id(1)

    @pl.when(kb == 0)
    def _():
        for h in range(heads):
            qn = q_ref[:, h * HEAD_PAD:h * HEAD_PAD + LANES]
            qabs_ref[h * CHUNK:(h + 1) * CHUNK, :] = _dot(
                qn, wukt_ref[h * LANES:(h + 1) * LANES, :]).astype(BF16)
            qr_ref[h * CHUNK:(h + 1) * CHUNK, :] = q_ref[:, h * HEAD_PAD + LANES:(h + 1) * HEAD_PAD]
        m_ref[...] = jnp.full(m_ref.shape, -jnp.inf, F32)
        l_ref[...] = jnp.zeros(l_ref.shape, F32)
        acc_ref[...] = jnp.zeros(acc_ref.shape, F32)

    def block(xl, krf):
        rtop = lax.dot_general(wukt_ref[...], xl, _NT, preferred_element_type=F32)
        sq = krf * krf
        sq_hi = sq.astype(BF16)
        sq_lo = (sq - sq_hi.astype(F32)).astype(BF16)
        ones = jnp.ones((SUBLANES, ROPE_DIM), BF16)
        kr2 = (lax.dot_general(ones, sq_hi, _NT, preferred_element_type=F32)
               + lax.dot_general(ones, sq_lo, _NT, preferred_element_type=F32))[0:1, :]
        s = (lax.dot_general(qabs_ref[...], xl, _NT, preferred_element_type=F32)
             + lax.dot_general(qr_ref[:, :ROPE_DIM], krf.astype(BF16), _NT,
                               preferred_element_type=F32))
        parts = []
        for h in range(heads):
            rt = rtop[h * LANES:(h + 1) * LANES, :]
            kn2 = jnp.sum(rt * rt, axis=0, keepdims=True)
            r = lax.rsqrt((kn2 + kr2) * (1.0 / QK_DIM) + EPS)
            parts.append(s[h * CHUNK:(h + 1) * CHUNK, :] * r)
        s = jnp.concatenate(parts, axis=0)
        n = s.shape[1]
        m_prev = m_ref[...]
        m_new = jnp.maximum(m_prev, jnp.max(s, axis=-1, keepdims=True))
        alpha = jnp.exp2(m_prev - m_new)
        m_wide = jnp.tile(m_new, (1, n // LANES)) if n >= LANES else m_new[:, :n]
        p = jnp.exp2(s - m_wide)
        l_ref[...] = alpha * l_ref[...] + jnp.sum(p, axis=-1, keepdims=True)
        acc_ref[...] = jnp.tile(alpha, (1, kvl // LANES)) * acc_ref[...] + _dot(p.astype(BF16), xl)
        m_ref[...] = m_new

    @pl.when(kb < nkb)
    def _():
        block(pl_ref[0].astype(BF16), pk_ref[0])

    @pl.when(kb == nkb)
    def _():
        block(nl_ref[:, :kvl], nl_ref[:, kvl:kvl + ROPE_DIM].astype(F32))
        o = (acc_ref[...] / jnp.tile(l_ref[...], (1, kvl // LANES))).astype(BF16)
        for h in range(heads):
            o_ref[:, h * V_DIM:(h + 1) * V_DIM] = _dot(
                o[h * CHUNK:(h + 1) * CHUNK, :], wuv_ref[:, h * V_DIM:(h + 1) * V_DIM]).astype(o_ref.dtype)


def _attn_sample(past_lat, past_kr, latk, q_all, w_uk_t, w_uv2, tp, heads):
    bs, past, kvl = past_lat.shape
    tk = _pick(past, (512, 256, 128))
    nkb = past // tk
    c0 = tp // CHUNK
    pidx = lambda b, kb: (b, jnp.minimum(kb, nkb - 1), 0)
    hq = heads * CHUNK
    return pl.pallas_call(
        functools.partial(_attn_s_kernel, heads=heads, kvl=kvl, nkb=nkb),
        out_shape=jax.ShapeDtypeStruct((bs * CHUNK, heads * V_DIM), BF16),
        grid=(bs, nkb + 1),
        in_specs=[pl.BlockSpec((1, tk, kvl), pidx),
                  pl.BlockSpec((1, tk, ROPE_DIM), pidx),
                  pl.BlockSpec((CHUNK, kvl + LANES), lambda b, kb: (c0 + b, 0)),
                  pl.BlockSpec((CHUNK, heads * HEAD_PAD), lambda b, kb: (c0 + b, 0)),
                  pl.BlockSpec((heads * LANES, kvl), lambda b, kb: (0, 0)),
                  pl.BlockSpec((kvl, heads * V_DIM), lambda b, kb: (0, 0))],
        out_specs=pl.BlockSpec((CHUNK, heads * V_DIM), lambda b, kb: (b, 0)),
        scratch_shapes=[pltpu.VMEM((hq, kvl), BF16), pltpu.VMEM((hq, LANES), BF16),
                        pltpu.VMEM((hq, LANES), F32), pltpu.VMEM((hq, LANES), F32),
                        pltpu.VMEM((hq, kvl), F32)],
        compiler_params=_cp("arbitrary", "arbitrary"),
        name="attn_sample",
    )(past_lat, past_kr, latk, q_all, w_uk_t, w_uv2)


def _ssd_kernel(z_ref, x_ref, dt_ref, dtt_ref, conv0_ref, ssm0_ref, cw_ref, cb_ref, alr_ref, alc_ref,
                dsk_ref, gs_ref, eh_ref, o_ref, st_out_ref, xs_ref, st_ref,
                *, lc, lr, nh, groups):
    c = pl.program_id(1)
    p = M_HEADDIM
    n = D_STATE
    ci = nh * p
    k8 = nh // groups
    gw = k8 * p

    @pl.when(c == 0)
    def _():
        xs_ref[0:SUBLANES, :] = conv0_ref[0]
        st_ref[...] = ssm0_ref[0].reshape(st_ref.shape)

    xs_ref[SUBLANES:SUBLANES + lr, :] = x_ref[...].astype(F32)
    if lc > lr:
        xs_ref[SUBLANES + lr:SUBLANES + lc, :] = jnp.zeros((lc - lr, xs_ref.shape[1]), F32)

    def conv(lo, hi):
        u = xs_ref[SUBLANES - 3:SUBLANES - 3 + lc, lo:hi] * cw_ref[0:1, lo:hi]
        for tap in range(1, CONV_W):
            u = u + xs_ref[SUBLANES - 3 + tap:SUBLANES - 3 + tap + lc, lo:hi] * cw_ref[tap:tap + 1, lo:hi]
        return _silu(u + cb_ref[:, lo:hi])

    dt = dt_ref[0]
    dtt = dtt_ref[0]
    a_row = -jnp.exp(alr_ref[...])
    a_col = -jnp.exp(alc_ref[...])
    ri = lax.broadcasted_iota(I32, (lc, lc), 0)
    cidx = lax.broadcasted_iota(I32, (lc, lc), 1)
    tri = ri >= cidx
    tril = jnp.where(tri, 1.0, 0.0).astype(BF16)
    triu = jnp.where(ri <= cidx, 1.0, 0.0).astype(BF16)
    cs = sum(_dot(tril, piece) for piece in _split3(dt * a_row))
    cst = sum(_dot(piece, triu) for piece in _split3(dtt * a_col))
    exp_cs = jnp.exp(cs)
    w_end = jnp.exp(cs[lc - 1:lc, :] - cs)
    stacked = jnp.concatenate([dt, exp_cs, w_end], axis=0)
    eh = eh_ref[...]
    expanded = sum(_dot(piece, eh) for piece in _split3(stacked))
    dt_e, ecs_e, wend_e = expanded[:lc], expanded[lc:2 * lc], expanded[2 * lc:]
    cdec = jnp.exp(cst[:, lc - 1:lc])
    lane_lo = lax.broadcasted_iota(I32, (lc, LANES), 1) < p

    for g in range(groups):
        gs = slice(g * gw, (g + 1) * gw)
        xg = conv(g * gw, (g + 1) * gw)
        bg = conv(ci + g * n, ci + (g + 1) * n).astype(BF16)
        cg = conv(ci + groups * n + g * n, ci + groups * n + (g + 1) * n).astype(BF16)
        cbm = lax.dot_general(cg, bg, _NT, preferred_element_type=F32)
        xdt = xg * dt_e[:, gs]
        xdt_b = xdt.astype(BF16)
        pairs = []
        for q in range(k8 // 2):
            x2 = xdt_b[:, q * LANES:(q + 1) * LANES]
            ys = []
            for s in range(2):
                h = g * k8 + 2 * q + s
                seg = cs[:, h:h + 1] - cst[h:h + 1, :]
                dec = jnp.exp(jnp.where(tri, seg, -jnp.inf))
                ys.append(_dot((cbm * dec).astype(BF16), x2))
            pairs.append(jnp.where(lane_lo, ys[0], ys[1]))
        y_diag = jnp.concatenate(pairs, axis=1)
        sg = st_ref[g * gw:(g + 1) * gw, :]
        y_off = lax.dot_general(cg, sg.astype(BF16), _NT, preferred_element_type=F32) * ecs_e[:, gs]
        y = y_diag + y_off + xg * dsk_ref[:, gs]
        xw = (xdt * wend_e[:, gs]).astype(BF16)
        upd = lax.dot_general(xw, bg, _TN, preferred_element_type=F32)
        for k in range(k8):
            h = g * k8 + k
            rows = slice(g * gw + k * p, g * gw + (k + 1) * p)
            st_ref[rows, :] = st_ref[rows, :] * cdec[h:h + 1, :] + upd[k * p:(k + 1) * p, :]
        zg = z_ref[:, gs].astype(F32)
        u2 = y[:lr] * _silu(zg)
        ms = jnp.mean(u2 * u2, axis=-1, keepdims=True)
        o_ref[:, gs] = (u2 * lax.rsqrt(ms + EPS) * gs_ref[:, gs]).astype(o_ref.dtype)

    xs_ref[0:SUBLANES, :] = xs_ref[lr:lr + SUBLANES, :]

    @pl.when(c == pl.num_programs(1) - 1)
    def _():
        st_out_ref[0] = st_ref[...].reshape(st_out_ref.shape[1:])


def _ssd(z_all, xbc_all, dt3, dtt3, conv0p, ssm0, conv_w, conv_b, a_log, dsk, g_ssm, eh,
         row0, nseq, nchunk, lc, lr, groups):
    ci = z_all.shape[1]
    cc = xbc_all.shape[1]
    nh = ci // M_HEADDIM
    rb0 = row0 // lr
    rowblk = lambda w: pl.BlockSpec((lr, w), lambda b, c: (rb0 + b * nchunk + c, 0))
    seq3 = lambda a, b_: pl.BlockSpec((1, a, b_), lambda b, c: (b * nchunk + c, 0, 0))
    perb = lambda a, b_: pl.BlockSpec((1, a, b_), lambda b, c: (b, 0, 0))
    state = pl.BlockSpec((1, nh, M_HEADDIM, D_STATE), lambda b, c: (b, 0, 0, 0))
    const = lambda a, b_: pl.BlockSpec((a, b_), lambda b, c: (0, 0))
    return pl.pallas_call(
        functools.partial(_ssd_kernel, lc=lc, lr=lr, nh=nh, groups=groups),
        out_shape=(jax.ShapeDtypeStruct((nseq * nchunk * lr, ci), BF16),
                   jax.ShapeDtypeStruct((nseq, nh, M_HEADDIM, D_STATE), F32)),
        grid=(nseq, nchunk),
        in_specs=[rowblk(ci), rowblk(cc), seq3(lc, nh), seq3(nh, lc),
                  perb(SUBLANES, cc), state,
                  const(CONV_W, cc), const(1, cc), const(1, nh), const(nh, 1),
                  const(1, ci), const(1, ci), const(nh, ci)],
        out_specs=(pl.BlockSpec((lr, ci), lambda b, c: (b * nchunk + c, 0)), state),
        scratch_shapes=[pltpu.VMEM((lc + SUBLANES, cc), F32), pltpu.VMEM((ci, D_STATE), F32)],
        compiler_params=_cp("arbitrary", "arbitrary"),
        name="ssd",
    )(z_all, xbc_all, dt3, dtt3, conv0p, ssm0, conv_w, conv_b.reshape(1, cc),
      a_log.reshape(1, nh), a_log.reshape(nh, 1), dsk, g_ssm.reshape(1, ci), eh)


def _mix_kernel(ap_ref, as_ref, mp_ref, ms_ref, ga_ref, gb_ref, wpa_ref, wpb_ref, o_ref, *, npb):
    i = pl.program_id(1)
    a = jnp.where(i < npb, ap_ref[...], as_ref[...])
    m = jnp.where(i < npb, mp_ref[...], ms_ref[...])
    pa = _dot(a, wpa_ref[...])
    pb = _dot(m, wpb_ref[...])
    o_ref[...] = (ga_ref[...].astype(F32) * pa + gb_ref[...].astype(F32) * pb).astype(o_ref.dtype)


def _mix(a_p, a_s, m_p, m_s, gates, w_pa, w_pb):
    tp, hv = a_p.shape
    ts = a_s.shape[0]
    ci = m_p.shape[1]
    d = w_pa.shape[1]
    tm = _pick(math.gcd(tp, ts), (256, 128, 64))
    tn = _pick(d, (1024, 512, 256, 128))
    npb = tp // tm
    nj = d // tn
    pidx = lambda j, i: (jnp.minimum(i, npb - 1), 0)
    sidx = lambda j, i: (jnp.maximum(i - npb, 0), 0)
    return pl.pallas_call(
        functools.partial(_mix_kernel, npb=npb),
        out_shape=jax.ShapeDtypeStruct((tp + ts, d), BF16),
        grid=(nj, (tp + ts) // tm),
        in_specs=[pl.BlockSpec((tm, hv), pidx), pl.BlockSpec((tm, hv), sidx),
                  pl.BlockSpec((tm, ci), pidx), pl.BlockSpec((tm, ci), sidx),
                  pl.BlockSpec((tm, tn), lambda j, i: (i, j)),
                  pl.BlockSpec((tm, tn), lambda j, i: (i, nj + j)),
                  pl.BlockSpec((hv, tn), lambda j, i: (0, j)),
                  pl.BlockSpec((ci, tn), lambda j, i: (0, j))],
        out_specs=pl.BlockSpec((tm, tn), lambda j, i: (i, j)),
        compiler_params=_cp("arbitrary", "arbitrary"),
        name="branch_mix",
    )(a_p, a_s, m_p, m_s, gates, gates, w_pa, w_pb)


def _pack_bf16_pairs(x):
    n = x.shape[1] // 2
    lo = pltpu.bitcast(x[:, :n].astype(BF16).astype(F32), jnp.uint32)
    hi = pltpu.bitcast(x[:, n:].astype(BF16).astype(F32), jnp.uint32)
    return hi | (lo >> 16)


def _unpack_bf16_pairs(u):
    lo = pltpu.bitcast(u << 16, F32)
    hi = pltpu.bitcast(u & jnp.uint32(0xFFFF0000), F32)
    return jnp.concatenate([lo, hi], axis=1)


def _post_kernel(mx_ref, wout_ref, xp_ref, xs_ref, gt1_ref, sc2_ref, sh2_ref, g2_ref, wrh_ref, wrl_ref, br_ref,
                 x1_ref, h2p_ref, ti_ref, tp_ref, h2_ref, *, nexp, npb):
    is_prompt = pl.program_id(0) < npb
    o = _dot(mx_ref[...], wout_ref[...])
    tm = o.shape[0]
    for ch in range(tm // CHUNK):
        rows = slice(ch * CHUNK, (ch + 1) * CHUNK)
        x1 = jnp.where(is_prompt, xp_ref[ch], xs_ref[ch]) + gt1_ref[ch] * o[rows, :]
        x1_ref[rows, :] = x1
        xn = x1 * lax.rsqrt(jnp.mean(x1 * x1, axis=-1, keepdims=True) + EPS)
        h2_ref[rows, :] = xn * g2_ref[...] * (1.0 + sc2_ref[ch]) + sh2_ref[ch]
    h2 = h2_ref[...]
    h2p_ref[...] = _pack_bf16_pairs(h2)
    hh = h2.astype(BF16)
    hl = (h2 - hh.astype(F32)).astype(BF16)
    logits = _dot(hh, wrh_ref[...]) + _dot(hh, wrl_ref[...]) + _dot(hl, wrh_ref[...]) + br_ref[...]
    lane = lax.broadcasted_iota(I32, logits.shape, 1)
    logits = jnp.where(lane < nexp, logits, NEG_BIG)
    vals, idxs = [], []
    for _ in range(TOP_K):
        m = jnp.max(logits, axis=-1, keepdims=True)
        idx = jnp.min(jnp.where(logits == m, lane, LANES), axis=-1, keepdims=True)
        vals.append(m)
        idxs.append(idx)
        logits = jnp.where(lane == idx, 2.0 * NEG_BIG, logits)
    es = [jnp.exp(v - vals[0]) for v in vals]
    den = es[0]
    for e in es[1:]:
        den = den + e
    ti = jnp.zeros(lane.shape, I32)
    tpv = jnp.zeros(lane.shape, F32)
    for k in range(TOP_K):
        ti = jnp.where(lane == k, idxs[k], ti)
        tpv = jnp.where(lane == k, es[k] / den, tpv)
    ti_ref[...] = ti.T
    tp_ref[...] = tpv


def _post(mixed, w_out, xp3, xs3, gt1, sc2, sh2, g2, wr_hi, wr_lo, br, nexp):
    t, d = mixed.shape
    ncp, ncs = xp3.shape[0], xs3.shape[0]
    nc = _pick(math.gcd(ncp, ncs), (4, 2, 1))
    tm = nc * CHUNK
    npb = ncp // nc
    row = lambda w: pl.BlockSpec((tm, w), lambda i: (i, 0))
    mod = pl.BlockSpec((nc, 1, d), lambda i: (i, 0, 0))
    const = lambda a, b: pl.BlockSpec((a, b), lambda i: (0, 0))
    return pl.pallas_call(
        functools.partial(_post_kernel, nexp=nexp, npb=npb),
        out_shape=(jax.ShapeDtypeStruct((t, d), F32), jax.ShapeDtypeStruct((t, d // 2), jnp.uint32),
                   jax.ShapeDtypeStruct((LANES, t), I32), jax.ShapeDtypeStruct((t, LANES), F32)),
        grid=(t // tm,),
        in_specs=[row(d), const(d, d),
                  pl.BlockSpec((nc, CHUNK, d), lambda i: (jnp.minimum(i, npb - 1), 0, 0)),
                  pl.BlockSpec((nc, CHUNK, d), lambda i: (jnp.maximum(i - npb, 0), 0, 0)),
                  mod, mod, mod, const(1, d),
                  const(d, LANES), const(d, LANES), const(1, LANES)],
        out_specs=(row(d), row(d // 2), pl.BlockSpec((LANES, tm), lambda i: (0, i)), row(LANES)),
        scratch_shapes=[pltpu.VMEM((tm, d), F32)],
        compiler_params=_cp("arbitrary"),
        name="post_mix",
    )(mixed, w_out, xp3, xs3, gt1, sc2, sh2, g2.reshape(1, d), wr_hi, wr_lo, br)


def _slots_kernel(ti_ref, dest_ref, meta_ref, run_ref, *, bm, tb):
    ph = pl.program_id(0)
    i = pl.program_id(1)
    eid = lax.broadcasted_iota(I32, (LANES, tb), 0)
    onehots = [jnp.where(eid == ti_ref[k:k + 1, :], 1.0, 0.0) for k in range(TOP_K)]
    osum = onehots[0]
    for oh in onehots[1:]:
        osum = osum + oh
    blk_cnt = jnp.sum(osum, axis=1, keepdims=True)

    @pl.when(jnp.logical_and(ph == 0, i == 0))
    def _():
        run_ref[...] = jnp.zeros(run_ref.shape, F32)

    @pl.when(ph == 0)
    def _():
        run_ref[...] = run_ref[...] + blk_cnt

    @pl.when(jnp.logical_and(ph == 1, i == 0))
    def _():
        cnt = run_ref[...]
        nblk = jnp.ceil(cnt * (1.0 / bm))
        r = lax.broadcasted_iota(I32, (LANES, LANES), 0)
        c = lax.broadcasted_iota(I32, (LANES, LANES), 1)
        lstrict = jnp.where(c < r, 1.0, 0.0).astype(BF16)
        start_blk = _dot(lstrict, jnp.broadcast_to(nblk, (LANES, LANES)).astype(BF16))
        lane = lax.broadcasted_iota(I32, (LANES, LANES), 1)
        meta_ref[...] = jnp.where(lane == 0, cnt, jnp.where(lane == 1, start_blk, 0.0))
        run_ref[...] = start_blk[:, 0:1] * float(bm)

    @pl.when(ph == 1)
    def _():
        r = lax.broadcasted_iota(I32, (tb, tb), 0)
        c = lax.broadcasted_iota(I32, (tb, tb), 1)
        ustrict = jnp.where(r < c, 1.0, 0.0).astype(BF16)
        base = run_ref[...] + _dot(osum.astype(BF16), ustrict)
        for k in range(TOP_K):
            dest_ref[0, k:k + 1, :] = jnp.sum(onehots[k] * base, axis=0, keepdims=True).astype(I32)
        dest_ref[0, TOP_K:, :] = jnp.zeros((SUBLANES - TOP_K, tb), I32)
        run_ref[...] = run_ref[...] + blk_cnt


def _moe_slots(ti_t, bm, tb):
    t = ti_t.shape[1]
    nt = t // tb
    return pl.pallas_call(
        functools.partial(_slots_kernel, bm=bm, tb=tb),
        out_shape=(jax.ShapeDtypeStruct((nt, SUBLANES, tb), I32),
                   jax.ShapeDtypeStruct((LANES, LANES), F32)),
        grid=(2, nt),
        in_specs=[pl.BlockSpec((SUBLANES, tb), lambda ph, i: (0, i))],
        out_specs=(pl.BlockSpec((1, SUBLANES, tb), lambda ph, i: (i * ph, 0, 0)),
                   pl.BlockSpec((LANES, LANES), lambda ph, i: (0, 0))),
        scratch_shapes=[pltpu.VMEM((LANES, 1), F32)],
        compiler_params=_cp("arbitrary", "arbitrary"),
        name="moe_slots",
    )(ti_t)


def _dispatch_kernel(zs_ref, idx_ref, h_ref, xg_ref, zbuf_ref, idx_smem, isem, sem, *, nz, bm, tb):
    i = pl.program_id(0)

    def zero_fill(e):
        return pltpu.make_async_copy(zbuf_ref, xg_ref.at[pl.ds(pl.multiple_of(zs_ref[e], bm), bm)], sem)

    @pl.when(i == 0)
    def _():
        zbuf_ref[...] = jnp.zeros(zbuf_ref.shape, zbuf_ref.dtype)
        for e in range(nz):
            pl.when(zs_ref[e] >= 0)(lambda e=e: zero_fill(e).start())
        for e in range(nz):
            pl.when(zs_ref[e] >= 0)(lambda e=e: zero_fill(e).wait())

    icp = pltpu.make_async_copy(idx_ref.at[0], idx_smem, isem)
    icp.start()
    icp.wait()

    def start(r, carry):
        for k in range(TOP_K):
            pltpu.make_async_copy(h_ref.at[pl.ds(r, 1)], xg_ref.at[pl.ds(idx_smem[k, r], 1)],
                                  sem).start(priority=k % 2)
        return carry

    lax.fori_loop(0, tb, start, 0, unroll=True)
    for k in range(TOP_K):
        pltpu.make_async_copy(h_ref, xg_ref.at[pl.ds(0, tb)], sem).wait()


def _moe_dispatch(zstart, dest3, h2, n_slots, bm):
    nt, _, tb = dest3.shape
    d = h2.shape[1]
    return pl.pallas_call(
        functools.partial(_dispatch_kernel, nz=zstart.shape[0], bm=bm, tb=tb),
        out_shape=jax.ShapeDtypeStruct((n_slots, d), h2.dtype),
        grid_spec=pltpu.PrefetchScalarGridSpec(
            num_scalar_prefetch=1,
            grid=(nt,),
            in_specs=[pl.BlockSpec((1, SUBLANES, tb), lambda i, zs: (i, 0, 0)),
                      pl.BlockSpec((tb, d), lambda i, zs: (i, 0))],
            out_specs=pl.BlockSpec(memory_space=pl.ANY),
            scratch_shapes=[pltpu.VMEM((bm, d), h2.dtype), pltpu.SMEM((SUBLANES, tb), I32),
                            pltpu.SemaphoreType.DMA, pltpu.SemaphoreType.DMA]),
        compiler_params=_cp("arbitrary"),
        name="moe_dispatch",
    )(zstart, dest3, h2)


def _weight_stream(w_refs, wbuf_ref, wb_refs, sem, cnt_ref, be_ref, nx_ref, used, tw):
    c = pl.program_id(0)
    b = pl.program_id(1)
    nc = pl.num_programs(0)

    def fetch(e, cc, slot):
        col = pl.ds(pl.multiple_of(cc * tw, tw), tw)
        return [pltpu.make_async_copy(w.at[e, :, col], wbuf_ref.at[slot, m], sem.at[slot])
                for m, w in enumerate(w_refs)]

    @pl.when(jnp.logical_and(c == 0, b == 0))
    def _():
        cnt_ref[0] = 0
        for cp in fetch(be_ref[0], 0, 0):
            cp.start(priority=WEIGHT_DMA_PRIORITY)

    changed = jnp.logical_or(b == 0, be_ref[b] != be_ref[jnp.maximum(b - 1, 0)])

    @pl.when(jnp.logical_and(used, changed))
    def _():
        slot = cnt_ref[0] & 1
        for cp in fetch(be_ref[b], c, slot):
            cp.wait()
        for m, wb in enumerate(wb_refs):
            wb[...] = wbuf_ref[slot, m].astype(BF16)
        nxt = nx_ref[b]
        same_chunk = nxt >= 0
        nxt_e = jnp.where(same_chunk, nxt, be_ref[0])
        nxt_c = jnp.where(same_chunk, c, c + 1)

        @pl.when(jnp.logical_or(same_chunk, c + 1 < nc))
        def _():
            for cp in fetch(nxt_e, nxt_c, 1 - slot):
                cp.start(priority=WEIGHT_DMA_PRIORITY)

        cnt_ref[0] = cnt_ref[0] + 1


def _gateup_kernel(be_ref, nx_ref, nu_ref, x_ref, bg_ref, bu_ref, wg_ref, wu_ref, o_ref,
                   wbuf_ref, wgb_ref, wub_ref, sem, cnt_ref, *, tf):
    used = pl.program_id(1) < nu_ref[0]
    _weight_stream([wg_ref, wu_ref], wbuf_ref, [wgb_ref, wub_ref], sem, cnt_ref, be_ref, nx_ref, used, tf)

    @pl.when(used)
    def _():
        x = _unpack_bf16_pairs(x_ref[...]).astype(BF16)
        gate = jnp.minimum(_dot(x, wgb_ref[...]) + bg_ref[0], SWIGLU_LIMIT)
        up = jnp.clip(_dot(x, wub_ref[...]) + bu_ref[0], -SWIGLU_LIMIT, SWIGLU_LIMIT)
        glu = gate * jax.nn.sigmoid(SWIGLU_ALPHA * gate)
        o_ref[...] = ((up + 1.0) * glu).astype(o_ref.dtype)

    @pl.when(jnp.logical_not(used))
    def _():
        o_ref[...] = jnp.zeros(o_ref.shape, o_ref.dtype)


def _moe_gateup(block_e, next_e, n_used, xg, w_gate, w_up, b_gate, b_up):
    ns, dh = xg.shape
    nexp, d, ff = w_gate.shape
    assert d == 2 * dh
    bm = MOE_BLOCK
    nb = ns // bm
    tf = _pick(ff, (1024, 512, 256, 128))
    bspec = pl.BlockSpec((1, 1, tf), lambda c, b, be, nx, nu: (be[b], 0, c))
    hbm = pl.BlockSpec(memory_space=pl.ANY)
    return pl.pallas_call(
        functools.partial(_gateup_kernel, tf=tf),
        out_shape=jax.ShapeDtypeStruct((ns, ff), BF16),
        grid_spec=pltpu.PrefetchScalarGridSpec(
            num_scalar_prefetch=3,
            grid=(ff // tf, nb),
            in_specs=[pl.BlockSpec((bm, dh), lambda c, b, be, nx, nu: (jnp.minimum(b, nu[0] - 1), 0)),
                      bspec, bspec, hbm, hbm],
            out_specs=pl.BlockSpec((bm, tf), lambda c, b, be, nx, nu: (b, c)),
            scratch_shapes=[pltpu.VMEM((2, 2, d, tf), F32), pltpu.VMEM((d, tf), BF16),
                            pltpu.VMEM((d, tf), BF16), pltpu.SemaphoreType.DMA((2,)),
                            pltpu.SMEM((1,), I32)]),
        compiler_params=_cp("arbitrary", "arbitrary"),
        name="moe_gateup",
    )(block_e, next_e, n_used, xg, b_gate.reshape(nexp, 1, ff), b_up.reshape(nexp, 1, ff), w_gate, w_up)


def _down_kernel(be_ref, nx_ref, nu_ref, g_ref, bd_ref, wd_ref, o_ref, wbuf_ref, wdb_ref, sem, cnt_ref, *, tn):
    used = pl.program_id(1) < nu_ref[0]
    _weight_stream([wd_ref], wbuf_ref, [wdb_ref], sem, cnt_ref, be_ref, nx_ref, used, tn)

    @pl.when(used)
    def _():
        o_ref[...] = _pack_bf16_pairs(_dot(g_ref[...], wdb_ref[...]) + bd_ref[0])

    @pl.when(jnp.logical_not(used))
    def _():
        o_ref[...] = jnp.zeros(o_ref.shape, o_ref.dtype)


def _moe_down(block_e, next_e, n_used, glu, w_down, b_down):
    ns, ff = glu.shape
    nexp, _, d = w_down.shape
    bm = MOE_BLOCK
    nb = ns // bm
    tn = d
    return pl.pallas_call(
        functools.partial(_down_kernel, tn=tn),
        out_shape=jax.ShapeDtypeStruct((ns, d // 2), jnp.uint32),
        grid_spec=pltpu.PrefetchScalarGridSpec(
            num_scalar_prefetch=3,
            grid=(d // tn, nb),
            in_specs=[pl.BlockSpec((bm, ff), lambda c, b, be, nx, nu: (jnp.minimum(b, nu[0] - 1), 0)),
                      pl.BlockSpec((1, 1, tn), lambda c, b, be, nx, nu: (be[b], 0, c)),
                      pl.BlockSpec(memory_space=pl.ANY)],
            out_specs=pl.BlockSpec((bm, tn // 2), lambda c, b, be, nx, nu: (b, c)),
            scratch_shapes=[pltpu.VMEM((2, 1, ff, tn), F32), pltpu.VMEM((ff, tn), BF16),
                            pltpu.SemaphoreType.DMA((2,)), pltpu.SMEM((1,), I32)]),
        compiler_params=_cp("arbitrary", "arbitrary"),
        name="moe_down",
    )(block_e, next_e, n_used, glu, b_down.reshape(nexp, 1, d), w_down)


def _combine_kernel(idx_ref, yb_ref, x1_ref, gt2_ref, p_ref, o_ref, buf_ref, idx_smem, isem, sem, *, tc):
    icp = pltpu.make_async_copy(idx_ref.at[0], idx_smem, isem)
    icp.start()
    icp.wait()

    def start(r, carry):
        for k in range(TOP_K):
            pltpu.make_async_copy(yb_ref.at[pl.ds(idx_smem[k, r], 1)],
                                  buf_ref.at[pl.ds(k * tc + r, 1)], sem).start(priority=k % 2)
        return carry

    lax.fori_loop(0, tc, start, 0, unroll=True)
    pltpu.make_async_copy(yb_ref.at[pl.ds(0, TOP_K * tc)], buf_ref, sem).wait()
    ff = p_ref[:, 0:1] * _unpack_bf16_pairs(buf_ref[0:tc, :])
    for k in range(1, TOP_K):
        ff = ff + p_ref[:, k:k + 1] * _unpack_bf16_pairs(buf_ref[k * tc:(k + 1) * tc, :])
    for ch in range(tc // CHUNK):
        rows = slice(ch * CHUNK, (ch + 1) * CHUNK)
        o_ref[rows, :] = x1_ref[rows, :] + gt2_ref[ch] * ff[rows, :]


def _moe_combine(dest3, yb, x1, gt2, top_p, row0, nrows):
    d = 2 * yb.shape[1]
    tc = dest3.shape[2]
    nc = tc // CHUNK
    rb0 = row0 // tc
    return pl.pallas_call(
        functools.partial(_combine_kernel, tc=tc),
        out_shape=jax.ShapeDtypeStruct((nrows, d), F32),
        grid=(nrows // tc,),
        in_specs=[pl.BlockSpec((1, SUBLANES, tc), lambda i: (rb0 + i, 0, 0)),
                  pl.BlockSpec(memory_space=pl.ANY),
                  pl.BlockSpec((tc, d), lambda i: (rb0 + i, 0)),
                  pl.BlockSpec((nc, 1, d), lambda i: (rb0 + i, 0, 0)),
                  pl.BlockSpec((tc, LANES), lambda i: (rb0 + i, 0))],
        out_specs=pl.BlockSpec((tc, d), lambda i: (i, 0)),
        scratch_shapes=[pltpu.VMEM((TOP_K * tc, d // 2), jnp.uint32), pltpu.SMEM((SUBLANES, tc), I32),
                        pltpu.SemaphoreType.DMA, pltpu.SemaphoreType.DMA],
        compiler_params=_cp("arbitrary"),
        name="moe_combine",
    )(dest3, yb, x1, gt2, top_p)


def _block_layout(meta, nexp, bm, n_blocks):
    counts = meta[:nexp, 0]
    start_blk = meta[:nexp, 1].astype(I32)
    nblk = jnp.ceil(counts * (1.0 / bm)).astype(I32)
    end_blk = start_blk + nblk
    n_used = jnp.sum(nblk)
    blk = jnp.minimum(jnp.arange(n_blocks), n_used - 1)
    block_e = jnp.minimum(jnp.sum(end_blk[None, :] <= blk[:, None], axis=1), nexp - 1).astype(I32)
    onehot_e = block_e[:, None] == jnp.arange(nexp)[None, :]
    run_end = jnp.sum(jnp.where(onehot_e, end_blk[None, :], 0), axis=1)
    at_end = run_end[:, None] == jnp.arange(n_blocks)[None, :]
    next_e = jnp.where(run_end < n_used, jnp.sum(jnp.where(at_end, block_e[None, :], 0), axis=1), -1).astype(I32)
    last = jnp.where(nblk > 0, (end_blk - 1) * bm, -1)
    spare = n_used + jnp.arange(nexp)
    spare = jnp.where(spare < n_blocks, spare * bm, -1)
    zstart = jnp.concatenate([last, spare]).astype(I32)
    return block_e, next_e, n_used.reshape(1).astype(I32), zstart


def _rope_tables(pos):
    half = ROPE_DIM // 2
    inv = ROPE_THETA ** (-jnp.arange(half, dtype=F32) / half)
    ang = pos.astype(F32)[:, None] * inv[None, :]
    z = jnp.zeros((pos.shape[0], LANES - ROPE_DIM), F32)
    cos, sin = jnp.cos(ang), jnp.sin(ang)
    return jnp.concatenate([cos, cos, z], axis=1), jnp.concatenate([sin, sin, z], axis=1)


def _rot_half_cols(w):
    half = ROPE_DIM // 2
    return jnp.concatenate([-w[..., half:], w[..., :half]], axis=-1)


def _layer(x_prompt, x_sample, past_lat, past_kr, ssm_s0, conv_s0, c_prompt, c_sample,
           w_ada, b_ada, g_norm1, w_in, g_cq, g_ckv, w_uq, w_uk, w_uv, g_qn, g_kn, conv_w, conv_b, dt_bias,
           a_log, d_skip, g_ssm, w_pa, w_pb, w_out, g_norm2, w_router, b_router, w_gate, b_gate, w_up, b_up,
           w_down, b_down):
    bp, sp, d = x_prompt.shape
    bs, ss, _ = x_sample.shape
    assert ss == CHUNK and sp % CHUNK == 0
    past = past_lat.shape[1]
    ql, kvl = g_cq.shape[-1], g_ckv.shape[-1]
    heads = w_uq.shape[1]
    ci = g_ssm.shape[-1]
    cc = conv_w.shape[-1]
    nh = ci // M_HEADDIM
    groups = (cc - ci) // (2 * D_STATE)
    nexp = w_router.shape[-1]
    tp, ts = bp * sp, bs * ss
    t = tp + ts
    ncp, ncs = tp // CHUNK, ts // CHUNK

    ada = _ada(jnp.concatenate([c_prompt, c_sample], axis=0), w_ada, b_ada)
    per_chunk = jnp.concatenate([jnp.repeat(ada[:bp], sp // CHUNK, axis=0), ada[bp:]], axis=0)
    sh1, sc1, gt1, sh2, sc2, gt2 = [m[:, None, :] for m in jnp.split(per_chunk, 6, axis=-1)]

    xp3 = x_prompt.reshape(ncp, CHUNK, d)
    xs3 = x_sample.reshape(ncs, CHUNK, d)
    h3 = _norm1(xp3, xs3, g_norm1, sc1, sh1)
    h_all = h3.reshape(t, d)

    o = 0
    w_cq = w_in[:, o:o + ql]; o += ql
    w_ckv = w_in[:, o:o + kvl]; o += kvl
    w_kr = w_in[:, o:o + ROPE_DIM]; o += ROPE_DIM
    w_z = w_in[:, o:o + ci]; o += ci
    w_xbc = w_in[:, o:o + cc]; o += cc
    w_dt = w_in[:, o:o + nh]; o += nh
    w_gab = w_in[:, o:o + 2 * d]
    zc = lambda n_: jnp.zeros((d, n_), F32)
    w_lat = jnp.concatenate([w_cq, w_ckv, w_kr, zc(LANES - ROPE_DIM), _rot_half_cols(w_kr),
                             zc(LANES - ROPE_DIM), w_dt, zc(LANES - nh)], axis=1).astype(BF16)

    cos_p, sin_p = _rope_tables(jnp.arange(sp))
    cos_s, sin_s = _rope_tables(past + jnp.arange(ss))
    cos128 = jnp.concatenate([jnp.tile(cos_p, (bp, 1)), jnp.tile(cos_s, (bs, 1))], axis=0)
    sin128 = jnp.concatenate([jnp.tile(sin_p, (bp, 1)), jnp.tile(sin_s, (bs, 1))], axis=0)

    cqn, lat_all, latk, kr_all, dt_all, dtt_all = _lat(
        h_all, w_lat, w_dt.T.astype(BF16), g_cq, g_ckv, cos128, sin128, dt_bias)
    z_all = _mm(h_all, w_z.astype(BF16), BF16, "proj_z")
    xbc_all = _mm(h_all, w_xbc.astype(BF16), BF16, "proj_xbc")
    gates = _mm(h_all, w_gab.astype(BF16), BF16, "proj_gates", act="sigmoid")

    tail = CONV_W - 1
    cps = sp // CHUNK
    h_tail = jnp.concatenate([h3[cps - 1:ncp:cps, CHUNK - tail:, :].reshape(bp * tail, d),
                              h3[ncp:, CHUNK - tail:, :].reshape(bs * tail, d)], axis=0)
    conv_tail = _mm(h_tail, w_xbc.astype(BF16), F32, "proj_conv_tail")
    conv_p = conv_tail[:bp * tail].reshape(bp, tail, cc)
    conv_s = conv_tail[bp * tail:].reshape(bs, tail, cc)

    wq_a = jnp.concatenate([w_uq, jnp.zeros((ql, heads, HEAD_PAD - QK_DIM), F32)], axis=-1)
    wq_a = wq_a.reshape(ql, heads * HEAD_PAD).astype(BF16)
    wq_b = jnp.concatenate([_rot_half_cols(w_uq[..., NOPE_DIM:]),
                            jnp.zeros((ql, heads, LANES - ROPE_DIM), F32)], axis=-1)
    wq_b = wq_b.reshape(ql, heads * LANES).astype(BF16)
    gq = g_qn * g_kn * (ATTN_SCALE * math.log2(math.e))
    g_nope = gq[:NOPE_DIM].reshape(1, LANES)
    g_rope = jnp.concatenate([gq[NOPE_DIM:], jnp.zeros((LANES - ROPE_DIM,), F32)]).reshape(1, LANES)
    q_all = _qproj(cqn, wq_a, wq_b, cos128, sin128, g_nope, g_rope, heads)

    w_uk2 = w_uk.reshape(kvl, heads * NOPE_DIM)
    w_uv2 = w_uv.reshape(kvl, heads * V_DIM).astype(BF16)
    w_kv = jnp.concatenate([w_uk2.astype(BF16), w_uv2], axis=1)
    k_p, v_p = _kvproj(latk, w_kv, tp, heads, kvl)
    a_p = _attn_prompt(q_all, k_p, v_p, bp, sp, heads)
    a_s = _attn_sample(past_lat, past_kr, latk, q_all, w_uk2.T.astype(BF16), w_uv2, tp, heads)

    eh = jnp.repeat(jnp.eye(nh, dtype=BF16), M_HEADDIM, axis=1)
    dsk = jnp.repeat(d_skip, M_HEADDIM).reshape(1, ci)
    lcp = _pick(sp, (256, 128))
    assert sp % lcp == 0 and lcp % LANES == 0
    ncq = sp // lcp
    dt3_p = dt_all[:tp].reshape(bp * ncq, lcp, nh)
    dtt3_p = dtt_all[:, :tp].reshape(nh, bp * ncq, lcp).transpose(1, 0, 2)
    lcs = CHUNK
    dt3_s = dt_all[tp:].reshape(bs, CHUNK, nh)
    dtt3_s = dtt_all[:, tp:].reshape(nh, bs, CHUNK).transpose(1, 0, 2)
    pad_conv = lambda c0: jnp.concatenate(
        [jnp.zeros((c0.shape[0], SUBLANES - tail, cc), F32), c0], axis=1)
    m_p, ssm_p = _ssd(z_all, xbc_all, dt3_p, dtt3_p, jnp.zeros((bp, SUBLANES, cc), F32),
                      jnp.zeros((bp, nh, M_HEADDIM, D_STATE), F32), conv_w, conv_b, a_log, dsk, g_ssm, eh,
                      0, bp, ncq, lcp, lcp, groups)
    m_s, ssm_s = _ssd(z_all, xbc_all, dt3_s, dtt3_s, pad_conv(conv_s0),
                      ssm_s0, conv_w, conv_b, a_log, dsk, g_ssm, eh,
                      tp, bs, 1, lcs, CHUNK, groups)

    mixed = _mix(a_p, a_s, m_p, m_s, gates, w_pa.astype(BF16), w_pb.astype(BF16))
    wr = jnp.concatenate([w_router, jnp.zeros((d, LANES - nexp), F32)], axis=1)
    wr_hi = wr.astype(BF16)
    wr_lo = (wr - wr_hi.astype(F32)).astype(BF16)
    br = jnp.concatenate([b_router, jnp.zeros((LANES - nexp,), F32)]).reshape(1, LANES)
    x1, h2, ti, tpr = _post(mixed, w_out.astype(BF16), xp3, xs3, gt1, sc2, sh2, g_norm2, wr_hi, wr_lo, br, nexp)

    assert nexp <= LANES
    bm = MOE_BLOCK
    n_blocks = -(-(t * TOP_K + nexp * (bm - 1)) // bm)
    tb = _pick(math.gcd(tp, ts), (256, 128))
    dest3, meta = _moe_slots(ti, bm, tb)
    block_e, next_e, n_used, zstart = _block_layout(meta, nexp, bm, n_blocks)
    xg = _moe_dispatch(zstart, dest3, h2, n_blocks * bm, bm)
    glu = _moe_gateup(block_e, next_e, n_used, xg, w_gate, w_up, b_gate, b_up)
    yb = _moe_down(block_e, next_e, n_used, glu, w_down, b_down)
    y_p = _moe_combine(dest3, yb, x1, gt2, tpr, 0, tp)
    y_s = _moe_combine(dest3, yb, x1, gt2, tpr, tp, ts)

    return (y_p.reshape(bp, sp, d), y_s.reshape(bs, ss, d),
            lat_all[:tp].reshape(bp, sp, kvl), kr_all[:tp].reshape(bp, sp, ROPE_DIM),
            ssm_p, conv_p,
            lat_all[tp:].reshape(bs, ss, kvl), kr_all[tp:].reshape(bs, ss, ROPE_DIM),
            ssm_s, conv_s)


def kernel(x_prompt, x_sample, cache_mla_latent, cache_mla_krope, state_ssm, state_conv, c_prompt, c_sample,
           w_ada, b_ada, g_norm1, w_in, g_cq, g_ckv, w_uq, w_uk, w_uv, g_qn, g_kn, conv_w, conv_b, dt_bias,
           a_log, d_skip, g_ssm, w_pa, w_pb, w_out, g_norm2, w_router, b_router, w_gate, b_gate, w_up, b_up,
           w_down, b_down):
    depth = w_ada.shape[0]
    assert depth == 1, "single-layer encoder"
    weights = (w_ada, b_ada, g_norm1, w_in, g_cq, g_ckv, w_uq, w_uk, w_uv, g_qn, g_kn, conv_w, conv_b, dt_bias,
               a_log, d_skip, g_ssm, w_pa, w_pb, w_out, g_norm2, w_router, b_router, w_gate, b_gate, w_up, b_up,
               w_down, b_down)
    drop = lambda a: a.reshape(a.shape[1:])
    outs = _layer(x_prompt, x_sample, drop(cache_mla_latent), drop(cache_mla_krope), drop(state_ssm),
                  drop(state_conv), c_prompt, c_sample, *[drop(w) for w in weights])
    return outs[:2] + tuple(o.reshape((1,) + o.shape) for o in outs[2:])
```

```python
import functools
import math

import jax
import jax.numpy as jnp
from jax import lax
from jax.experimental import pallas as pl
from jax.experimental.pallas import tpu as pltpu

F32 = jnp.float32
BF16 = jnp.bfloat16
I32 = jnp.int32

CHUNK = 64
NOPE_DIM = 128
ROPE_DIM = 64
QK_DIM = NOPE_DIM + ROPE_DIM
V_DIM = 128
HEAD_PAD = 256
ROPE_THETA = 10000.0
ATTN_SCALE = QK_DIM ** -0.5
M_HEADDIM = 64
D_STATE = 128
CONV_W = 4
TOP_K = 4
SWIGLU_LIMIT = 7.0
SWIGLU_ALPHA = 1.702
EPS = 1e-6

LANES = 128
SUBLANES = 8
VMEM_LIMIT = 56 * 1024 * 1024

MOE_BLOCK = 256
WEIGHT_DMA_PRIORITY = 1
NEG_BIG = -1e30

_NT = (((1,), (1,)), ((), ()))
_TN = (((0,), (0,)), ((), ()))


def _cp(*sem):
    return pltpu.CompilerParams(dimension_semantics=sem, vmem_limit_bytes=VMEM_LIMIT)


def _pick(n, prefs):
    for p in prefs:
        if n % p == 0:
            return p
    return n


def _dot(a, b):
    return jnp.dot(a, b, preferred_element_type=F32)


def _split3(v):
    hi = v.astype(BF16)
    r1 = v - hi.astype(F32)
    mid = r1.astype(BF16)
    lo = (r1 - mid.astype(F32)).astype(BF16)
    return hi, mid, lo


def _silu(x):
    return x * jax.nn.sigmoid(x)


def _softplus(x):
    return jnp.maximum(x, 0.0) + jnp.log1p(jnp.exp(-jnp.abs(x)))


def _ada_kernel(c_ref, w_ref, b_ref, o_ref):
    s = _silu(c_ref[...]).astype(BF16)
    o_ref[...] = _dot(s, w_ref[...].astype(BF16)) + b_ref[...]


def _ada(c_all, w_ada, b_ada):
    r, d = c_all.shape
    n = w_ada.shape[1]
    tn = _pick(n, (1024, 512, 256, 128))
    return pl.pallas_call(
        _ada_kernel,
        out_shape=jax.ShapeDtypeStruct((r, n), F32),
        grid=(n // tn,),
        in_specs=[pl.BlockSpec((r, d), lambda j: (0, 0)),
                  pl.BlockSpec((d, tn), lambda j: (0, j)),
                  pl.BlockSpec((1, tn), lambda j: (0, j))],
        out_specs=pl.BlockSpec((r, tn), lambda j: (0, j)),
        compiler_params=_cp("arbitrary"),
        name="ada",
    )(c_all, w_ada, b_ada.reshape(1, n))


def _norm1_kernel(xp_ref, xs_ref, g_ref, sc_ref, sh_ref, h_ref, *, npb):
    i = pl.program_id(0)
    x = jnp.where(i < npb, xp_ref[0].reshape(xs_ref.shape), xs_ref[...])
    xn = x * lax.rsqrt(jnp.mean(x * x, axis=-1, keepdims=True) + EPS)
    h_ref[...] = (xn * g_ref[...] * (1.0 + sc_ref[...]) + sh_ref[...]).astype(BF16)


def _prompt_rows_spec(xp, rows, npb):
    per_seq = xp.shape[1] // rows
    def idx(i):
        ip = jnp.minimum(i, npb - 1)
        return (ip // per_seq, ip % per_seq, 0)
    return pl.BlockSpec((1, rows, xp.shape[2]), idx)


def _norm1(xp, xs3, g, sc, sh):
    bp, sp, d = xp.shape
    ncp = bp * sp // CHUNK
    ncs = xs3.shape[0]
    nch = ncp + ncs
    gc = _pick(math.gcd(sp // CHUNK, ncs), (4, 2, 1))
    npb = ncp // gc
    blk = (gc, CHUNK, d)
    mod = pl.BlockSpec((gc, 1, d), lambda i: (i, 0, 0))
    return pl.pallas_call(
        functools.partial(_norm1_kernel, npb=npb),
        out_shape=jax.ShapeDtypeStruct((nch, CHUNK, d), BF16),
        grid=(nch // gc,),
        in_specs=[_prompt_rows_spec(xp, gc * CHUNK, npb),
                  pl.BlockSpec(blk, lambda i: (jnp.maximum(i - npb, 0), 0, 0)),
                  pl.BlockSpec((1, 1, d), lambda i: (0, 0, 0)),
                  mod, mod],
        out_specs=pl.BlockSpec(blk, lambda i: (i, 0, 0)),
        compiler_params=_cp("arbitrary"),
        name="norm1",
    )(xp, xs3, g.reshape(1, 1, d), sc, sh)


def _mm_kernel(x_ref, w_ref, o_ref, *, act):
    acc = _dot(x_ref[...], w_ref[...])
    if act == "sigmoid":
        acc = jax.nn.sigmoid(acc)
    o_ref[...] = acc.astype(o_ref.dtype)


def _mm(x, w, out_dtype, name, act=None):
    m, k = x.shape
    n = w.shape[1]
    tm = _pick(m, (1024, 512, 256))
    tn = _pick(n, (1024, 512, 256, 128))
    return pl.pallas_call(
        functools.partial(_mm_kernel, act=act),
        out_shape=jax.ShapeDtypeStruct((m, n), out_dtype),
        grid=(m // tm, n // tn),
        in_specs=[pl.BlockSpec((tm, k), lambda i, j: (i, 0)),
                  pl.BlockSpec((k, tn), lambda i, j: (0, j))],
        out_specs=pl.BlockSpec((tm, tn), lambda i, j: (i, j)),
        compiler_params=_cp("arbitrary", "arbitrary"),
        name=name,
    )(x, w)


def _lat_kernel(h_ref, w_ref, wdt_ref, gcq_ref, gckv_ref, cos_ref, sin_ref, dtb_ref, dtbc_ref,
                cqn_ref, lat_ref, latk_ref, kr_ref, dt_ref, dtt_ref, *, ql, kvl):
    h = h_ref[...]
    acc = _dot(h, w_ref[...])
    cq = acc[:, :ql]
    cqn = cq * lax.rsqrt(jnp.mean(cq * cq, axis=-1, keepdims=True) + EPS) * gcq_ref[...]
    cqn_ref[...] = cqn.astype(BF16)
    ckv = acc[:, ql:ql + kvl]
    lat = ckv * lax.rsqrt(jnp.mean(ckv * ckv, axis=-1, keepdims=True) + EPS) * gckv_ref[...]
    lat_ref[...] = lat
    o = ql + kvl
    kr128 = acc[:, o:o + LANES] * cos_ref[...] + acc[:, o + LANES:o + 2 * LANES] * sin_ref[...]
    kr_ref[...] = kr128[:, :ROPE_DIM]
    latk_ref[:, :kvl] = lat.astype(BF16)
    latk_ref[:, kvl:] = kr128.astype(BF16)
    nh = dt_ref.shape[-1]
    dt_ref[...] = _softplus(acc[:, o + 2 * LANES:o + 2 * LANES + nh] + dtb_ref[...])
    dtt = lax.dot_general(wdt_ref[...], h, _NT, preferred_element_type=F32)
    dtt_ref[...] = _softplus(dtt + dtbc_ref[...])


def _lat(h_all, w_lat, w_dt_t, g_cq, g_ckv, cos128, sin128, dt_bias):
    t, d = h_all.shape
    ql, kvl = g_cq.shape[-1], g_ckv.shape[-1]
    nh = dt_bias.shape[-1]
    nl = w_lat.shape[1]
    tm = _pick(t, (512, 256, 128))
    row = lambda w: pl.BlockSpec((tm, w), lambda i: (i, 0))
    const = lambda a, b: pl.BlockSpec((a, b), lambda i: (0, 0))
    return pl.pallas_call(
        functools.partial(_lat_kernel, ql=ql, kvl=kvl),
        out_shape=(jax.ShapeDtypeStruct((t, ql), BF16),
                   jax.ShapeDtypeStruct((t, kvl), F32),
                   jax.ShapeDtypeStruct((t, kvl + LANES), BF16),
                   jax.ShapeDtypeStruct((t, ROPE_DIM), F32),
                   jax.ShapeDtypeStruct((t, nh), F32),
                   jax.ShapeDtypeStruct((nh, t), F32)),
        grid=(t // tm,),
        in_specs=[row(d), const(d, nl), const(nh, d), const(1, ql), const(1, kvl),
                  row(LANES), row(LANES), const(1, nh), const(nh, 1)],
        out_specs=(row(ql), row(kvl), row(kvl + LANES), row(ROPE_DIM), row(nh),
                   pl.BlockSpec((nh, tm), lambda i: (0, i))),
        compiler_params=_cp("arbitrary"),
        name="latent_proj",
    )(h_all, w_lat, w_dt_t, g_cq.reshape(1, ql), g_ckv.reshape(1, kvl), cos128, sin128,
      dt_bias.reshape(1, nh), dt_bias.reshape(nh, 1))


def _q_kernel(c_ref, wa_ref, wb_ref, cos_ref, sin_ref, gn_ref, gr_ref, o_ref, *, heads):
    c = c_ref[...]
    a = _dot(c, wa_ref[...])
    b = _dot(c, wb_ref[...])
    cos, sin = cos_ref[...], sin_ref[...]
    for h in range(heads):
        nope = a[:, h * HEAD_PAD:h * HEAD_PAD + LANES]
        rope = a[:, h * HEAD_PAD + LANES:(h + 1) * HEAD_PAD] * cos + b[:, h * LANES:(h + 1) * LANES] * sin
        ss = (jnp.sum(nope * nope, axis=-1, keepdims=True)
              + jnp.sum(rope * rope, axis=-1, keepdims=True)) * (1.0 / QK_DIM)
        r = lax.rsqrt(ss + EPS)
        o_ref[:, h * HEAD_PAD:h * HEAD_PAD + LANES] = (nope * r * gn_ref[...]).astype(BF16)
        o_ref[:, h * HEAD_PAD + LANES:(h + 1) * HEAD_PAD] = (rope * r * gr_ref[...]).astype(BF16)


def _qproj(cqn, wq_a, wq_b, cos128, sin128, g_nope, g_rope, heads):
    t, ql = cqn.shape
    tm = _pick(t, (256, 128))
    row = lambda w: pl.BlockSpec((tm, w), lambda i: (i, 0))
    const = lambda a, b: pl.BlockSpec((a, b), lambda i: (0, 0))
    return pl.pallas_call(
        functools.partial(_q_kernel, heads=heads),
        out_shape=jax.ShapeDtypeStruct((t, heads * HEAD_PAD), BF16),
        grid=(t // tm,),
        in_specs=[row(ql), const(ql, heads * HEAD_PAD), const(ql, heads * LANES),
                  row(LANES), row(LANES), const(1, LANES), const(1, LANES)],
        out_specs=row(heads * HEAD_PAD),
        compiler_params=_cp("arbitrary"),
        name="q_proj",
    )(cqn, wq_a, wq_b, cos128, sin128, g_nope, g_rope)


def _kv_kernel(lat_ref, kr_ref, w_ref, k_ref, v_ref, *, heads):
    acc = _dot(lat_ref[...], w_ref[...])
    kr = kr_ref[...].astype(F32)
    kr2 = jnp.sum(kr * kr, axis=-1, keepdims=True)
    for h in range(heads):
        kn = acc[:, h * LANES:(h + 1) * LANES]
        ss = (jnp.sum(kn * kn, axis=-1, keepdims=True) + kr2) * (1.0 / QK_DIM)
        r = lax.rsqrt(ss + EPS)
        k_ref[:, h * HEAD_PAD:h * HEAD_PAD + LANES] = (kn * r).astype(BF16)
        k_ref[:, h * HEAD_PAD + LANES:(h + 1) * HEAD_PAD] = (kr * r).astype(BF16)
    v_ref[...] = acc[:, heads * LANES:].astype(BF16)


def _kvproj(latk, w_kv, tp, heads, kvl):
    tm = _pick(tp, (256, 128))
    return pl.pallas_call(
        functools.partial(_kv_kernel, heads=heads),
        out_shape=(jax.ShapeDtypeStruct((tp, heads * HEAD_PAD), BF16),
                   jax.ShapeDtypeStruct((tp, heads * V_DIM), BF16)),
        grid=(tp // tm,),
        in_specs=[pl.BlockSpec((tm, kvl), lambda i: (i, 0)),
                  pl.BlockSpec((tm, LANES), lambda i: (i, kvl // LANES)),
                  pl.BlockSpec((kvl, 2 * heads * LANES), lambda i: (0, 0))],
        out_specs=(pl.BlockSpec((tm, heads * HEAD_PAD), lambda i: (i, 0)),
                   pl.BlockSpec((tm, heads * V_DIM), lambda i: (i, 0))),
        compiler_params=_cp("arbitrary"),
        name="kv_proj",
    )(latk, latk, w_kv)


def _attn_p_kernel(q_ref, k_ref, v_ref, o_ref, m_ref, l_ref, acc_ref, *, tq, hg):
    qi = pl.program_id(2)
    m_ref[...] = jnp.full(m_ref.shape, -jnp.inf, F32)
    l_ref[...] = jnp.zeros(l_ref.shape, F32)
    acc_ref[...] = jnp.zeros(acc_ref.shape, F32)

    def tile(j, masked):
        ks = pl.multiple_of(j * tq, tq)
        for g in range(hg):
            q = q_ref[:, g * HEAD_PAD:(g + 1) * HEAD_PAD]
            k = k_ref[pl.ds(ks, tq), g * HEAD_PAD:(g + 1) * HEAD_PAD]
            v = v_ref[pl.ds(ks, tq), g * V_DIM:(g + 1) * V_DIM]
            s = lax.dot_general(q, k, _NT, preferred_element_type=F32)
            if masked:
                rc = lax.broadcasted_iota(I32, (tq, tq), 0) // CHUNK
                cc = lax.broadcasted_iota(I32, (tq, tq), 1) // CHUNK
                s = jnp.where(cc <= rc, s, -jnp.inf)
            m_prev = m_ref[g]
            m_new = jnp.maximum(m_prev, jnp.max(s, axis=-1, keepdims=True))
            alpha = jnp.exp2(m_prev - m_new)
            p = jnp.exp2(s - jnp.tile(m_new, (1, tq // LANES)))
            l_ref[g] = alpha * l_ref[g] + jnp.sum(p, axis=-1, keepdims=True)
            acc_ref[g] = alpha * acc_ref[g] + _dot(p.astype(BF16), v)
            m_ref[g] = m_new

    def body(j, carry):
        tile(j, False)
        return carry

    lax.fori_loop(0, qi, body, 0)
    tile(qi, True)
    for g in range(hg):
        o_ref[:, g * V_DIM:(g + 1) * V_DIM] = (acc_ref[g] / l_ref[g]).astype(o_ref.dtype)


def _attn_prompt(q_all, k_p, v_p, batch, seq, heads):
    tq = _pick(seq, (512, 256, 128, 64))
    nq = seq // tq
    hg = _pick(heads, (4, 2, 1))
    return pl.pallas_call(
        functools.partial(_attn_p_kernel, tq=tq, hg=hg),
        out_shape=jax.ShapeDtypeStruct((batch * seq, heads * V_DIM), BF16),
        grid=(batch, heads // hg, nq),
        in_specs=[pl.BlockSpec((tq, hg * HEAD_PAD), lambda b, h, qi: (b * nq + qi, h)),
                  pl.BlockSpec((seq, hg * HEAD_PAD), lambda b, h, qi: (b, h)),
                  pl.BlockSpec((seq, hg * V_DIM), lambda b, h, qi: (b, h))],
        out_specs=pl.BlockSpec((tq, hg * V_DIM), lambda b, h, qi: (b * nq + qi, h)),
        scratch_shapes=[pltpu.VMEM((hg, tq, LANES), F32), pltpu.VMEM((hg, tq, LANES), F32),
                        pltpu.VMEM((hg, tq, V_DIM), F32)],
        compiler_params=_cp("arbitrary", "arbitrary", "arbitrary"),
        name="attn_prompt",
    )(q_all, k_p, v_p)


def _attn_s_kernel(pl_ref, pk_ref, nl_ref, q_ref, wukt_ref, wuv_ref, o_ref,
                   qabs_ref, qr_ref, m_ref, l_ref, acc_ref, *, heads, kvl, nkb):
    kb = pl.program_id(1)

    @pl.when(kb == 0)
    def _():
        for h in range(heads):
            qn = q_ref[:, h * HEAD_PAD:h * HEAD_PAD + LANES]
            qabs_ref[h * CHUNK:(h + 1) * CHUNK, :] = _dot(
                qn, wukt_ref[h * LANES:(h + 1) * LANES, :]).astype(BF16)
            qr_ref[h * CHUNK:(h + 1) * CHUNK, :] = q_ref[:, h * HEAD_PAD + LANES:(h + 1) * HEAD_PAD]
        m_ref[...] = jnp.full(m_ref.shape, -jnp.inf, F32)
        l_ref[...] = jnp.zeros(l_ref.shape, F32)
        acc_ref[...] = jnp.zeros(acc_ref.shape, F32)

    def block(xl, krf):
        rtop = lax.dot_general(wukt_ref[...], xl, _NT, preferred_element_type=F32)
        sq = krf * krf
        sq_hi = sq.astype(BF16)
        sq_lo = (sq - sq_hi.astype(F32)).astype(BF16)
        ones = jnp.ones((SUBLANES, ROPE_DIM), BF16)
        kr2 = (lax.dot_general(ones, sq_hi, _NT, preferred_element_type=F32)
               + lax.dot_general(ones, sq_lo, _NT, preferred_element_type=F32))[0:1, :]
        s = (lax.dot_general(qabs_ref[...], xl, _NT, preferred_element_type=F32)
             + lax.dot_general(qr_ref[:, :ROPE_DIM], krf.astype(BF16), _NT,
                               preferred_element_type=F32))
        parts = []
        for h in range(heads):
            rt = rtop[h * LANES:(h + 1) * LANES, :]
            kn2 = jnp.sum(rt * rt, axis=0, keepdims=True)
            r = lax.rsqrt((kn2 + kr2) * (1.0 / QK_DIM) + EPS)
            parts.append(s[h * CHUNK:(h + 1) * CHUNK, :] * r)
        s = jnp.concatenate(parts, axis=0)
        n = s.shape[1]
        m_prev = m_ref[...]
        m_new = jnp.maximum(m_prev, jnp.max(s, axis=-1, keepdims=True))
        alpha = jnp.exp2(m_prev - m_new)
        m_wide = jnp.tile(m_new, (1, n // LANES)) if n >= LANES else m_new[:, :n]
        p = jnp.exp2(s - m_wide)
        l_ref[...] = alpha * l_ref[...] + jnp.sum(p, axis=-1, keepdims=True)
        acc_ref[...] = jnp.tile(alpha, (1, kvl // LANES)) * acc_ref[...] + _dot(p.astype(BF16), xl)
        m_ref[...] = m_new

    @pl.when(kb < nkb)
    def _():
        block(pl_ref[0].astype(BF16), pk_ref[0])

    @pl.when(kb == nkb)
    def _():
        block(nl_ref[:, :kvl], nl_ref[:, kvl:kvl + ROPE_DIM].astype(F32))
        o = (acc_ref[...] / jnp.tile(l_ref[...], (1, kvl // LANES))).astype(BF16)
        for h in range(heads):
            o_ref[:, h * V_DIM:(h + 1) * V_DIM] = _dot(
                o[h * CHUNK:(h + 1) * CHUNK, :], wuv_ref[:, h * V_DIM:(h + 1) * V_DIM]).astype(o_ref.dtype)


def _attn_sample(past_lat, past_kr, latk, q_all, w_uk_t, w_uv2, tp, heads):
    bs, past, kvl = past_lat.shape
    tk = _pick(past, (512, 256, 128))
    nkb = past // tk
    c0 = tp // CHUNK
    pidx = lambda b, kb: (b, jnp.minimum(kb, nkb - 1), 0)
    hq = heads * CHUNK
    return pl.pallas_call(
        functools.partial(_attn_s_kernel, heads=heads, kvl=kvl, nkb=nkb),
        out_shape=jax.ShapeDtypeStruct((bs * CHUNK, heads * V_DIM), BF16),
        grid=(bs, nkb + 1),
        in_specs=[pl.BlockSpec((1, tk, kvl), pidx),
                  pl.BlockSpec((1, tk, ROPE_DIM), pidx),
                  pl.BlockSpec((CHUNK, kvl + LANES), lambda b, kb: (c0 + b, 0)),
                  pl.BlockSpec((CHUNK, heads * HEAD_PAD), lambda b, kb: (c0 + b, 0)),
                  pl.BlockSpec((heads * LANES, kvl), lambda b, kb: (0, 0)),
                  pl.BlockSpec((kvl, heads * V_DIM), lambda b, kb: (0, 0))],
        out_specs=pl.BlockSpec((CHUNK, heads * V_DIM), lambda b, kb: (b, 0)),
        scratch_shapes=[pltpu.VMEM((hq, kvl), BF16), pltpu.VMEM((hq, LANES), BF16),
                        pltpu.VMEM((hq, LANES), F32), pltpu.VMEM((hq, LANES), F32),
                        pltpu.VMEM((hq, kvl), F32)],
        compiler_params=_cp("arbitrary", "arbitrary"),
        name="attn_sample",
    )(past_lat, past_kr, latk, q_all, w_uk_t, w_uv2)


def _ssd_kernel(z_ref, x_ref, dt_ref, dtt_ref, conv0_ref, ssm0_ref, cw_ref, cb_ref, alr_ref, alc_ref,
                dsk_ref, gs_ref, eh_ref, o_ref, st_out_ref, xs_ref, st_ref,
                *, lc, lr, nh, groups):
    c = pl.program_id(1)
    p = M_HEADDIM
    n = D_STATE
    ci = nh * p
    k8 = nh // groups
    gw = k8 * p

    @pl.when(c == 0)
    def _():
        xs_ref[0:SUBLANES, :] = conv0_ref[0]
        st_ref[...] = ssm0_ref[0].reshape(st_ref.shape)

    xs_ref[SUBLANES:SUBLANES + lr, :] = x_ref[...].astype(F32)
    if lc > lr:
        xs_ref[SUBLANES + lr:SUBLANES + lc, :] = jnp.zeros((lc - lr, xs_ref.shape[1]), F32)

    def conv(lo, hi):
        u = xs_ref[SUBLANES - 3:SUBLANES - 3 + lc, lo:hi] * cw_ref[0:1, lo:hi]
        for tap in range(1, CONV_W):
            u = u + xs_ref[SUBLANES - 3 + tap:SUBLANES - 3 + tap + lc, lo:hi] * cw_ref[tap:tap + 1, lo:hi]
        return _silu(u + cb_ref[:, lo:hi])

    dt = dt_ref[0]
    dtt = dtt_ref[0]
    a_row = -jnp.exp(alr_ref[...])
    a_col = -jnp.exp(alc_ref[...])
    ri = lax.broadcasted_iota(I32, (lc, lc), 0)
    cidx = lax.broadcasted_iota(I32, (lc, lc), 1)
    tri = ri >= cidx
    tril = jnp.where(tri, 1.0, 0.0).astype(BF16)
    triu = jnp.where(ri <= cidx, 1.0, 0.0).astype(BF16)
    cs = sum(_dot(tril, piece) for piece in _split3(dt * a_row))
    cst = sum(_dot(piece, triu) for piece in _split3(dtt * a_col))
    exp_cs = jnp.exp(cs)
    w_end = jnp.exp(cs[lc - 1:lc, :] - cs)
    stacked = jnp.concatenate([dt, exp_cs, w_end], axis=0)
    eh = eh_ref[...]
    expanded = sum(_dot(piece, eh) for piece in _split3(stacked))
    dt_e, ecs_e, wend_e = expanded[:lc], expanded[lc:2 * lc], expanded[2 * lc:]
    cdec = jnp.exp(cst[:, lc - 1:lc])
    lane_lo = lax.broadcasted_iota(I32, (lc, LANES), 1) < p

    for g in range(groups):
        gs = slice(g * gw, (g + 1) * gw)
        xg = conv(g * gw, (g + 1) * gw)
        bg = conv(ci + g * n, ci + (g + 1) * n).astype(BF16)
        cg = conv(ci + groups * n + g * n, ci + groups * n + (g + 1) * n).astype(BF16)
        cbm = lax.dot_general(cg, bg, _NT, preferred_element_type=F32)
        xdt = xg * dt_e[:, gs]
        xdt_b = xdt.astype(BF16)
        pairs = []
        for q in range(k8 // 2):
            x2 = xdt_b[:, q * LANES:(q + 1) * LANES]
            ys = []
            for s in range(2):
                h = g * k8 + 2 * q + s
                seg = cs[:, h:h + 1] - cst[h:h + 1, :]
                dec = jnp.exp(jnp.where(tri, seg, -jnp.inf))
                ys.append(_dot((cbm * dec).astype(BF16), x2))
            pairs.append(jnp.where(lane_lo, ys[0], ys[1]))
        y_diag = jnp.concatenate(pairs, axis=1)
        sg = st_ref[g * gw:(g + 1) * gw, :]
        y_off = lax.dot_general(cg, sg.astype(BF16), _NT, preferred_element_type=F32) * ecs_e[:, gs]
        y = y_diag + y_off + xg * dsk_ref[:, gs]
        xw = (xdt * wend_e[:, gs]).astype(BF16)
        upd = lax.dot_general(xw, bg, _TN, preferred_element_type=F32)
        for k in range(k8):
            h = g * k8 + k
            rows = slice(g * gw + k * p, g * gw + (k + 1) * p)
            st_ref[rows, :] = st_ref[rows, :] * cdec[h:h + 1, :] + upd[k * p:(k + 1) * p, :]
        zg = z_ref[:, gs].astype(F32)
        u2 = y[:lr] * _silu(zg)
        ms = jnp.mean(u2 * u2, axis=-1, keepdims=True)
        o_ref[:, gs] = (u2 * lax.rsqrt(ms + EPS) * gs_ref[:, gs]).astype(o_ref.dtype)

    xs_ref[0:SUBLANES, :] = xs_ref[lr:lr + SUBLANES, :]

    @pl.when(c == pl.num_programs(1) - 1)
    def _():
        st_out_ref[0] = st_ref[...].reshape(st_out_ref.shape[1:])


def _ssd(z_all, xbc_all, dt3, dtt3, conv0p, ssm0, conv_w, conv_b, a_log, dsk, g_ssm, eh,
         row0, nseq, nchunk, lc, lr, groups):
    ci = z_all.shape[1]
    cc = xbc_all.shape[1]
    nh = ci // M_HEADDIM
    rb0 = row0 // lr
    rowblk = lambda w: pl.BlockSpec((lr, w), lambda b, c: (rb0 + b * nchunk + c, 0))
    seq3 = lambda a, b_: pl.BlockSpec((1, a, b_), lambda b, c: (b * nchunk + c, 0, 0))
    perb = lambda a, b_: pl.BlockSpec((1, a, b_), lambda b, c: (b, 0, 0))
    state = pl.BlockSpec((1, nh, M_HEADDIM, D_STATE), lambda b, c: (b, 0, 0, 0))
    const = lambda a, b_: pl.BlockSpec((a, b_), lambda b, c: (0, 0))
    return pl.pallas_call(
        functools.partial(_ssd_kernel, lc=lc, lr=lr, nh=nh, groups=groups),
        out_shape=(jax.ShapeDtypeStruct((nseq * nchunk * lr, ci), BF16),
                   jax.ShapeDtypeStruct((nseq, nh, M_HEADDIM, D_STATE), F32)),
        grid=(nseq, nchunk),
        in_specs=[rowblk(ci), rowblk(cc), seq3(lc, nh), seq3(nh, lc),
                  perb(SUBLANES, cc), state,
                  const(CONV_W, cc), const(1, cc), const(1, nh), const(nh, 1),
                  const(1, ci), const(1, ci), const(nh, ci)],
        out_specs=(pl.BlockSpec((lr, ci), lambda b, c: (b * nchunk + c, 0)), state),
        scratch_shapes=[pltpu.VMEM((lc + SUBLANES, cc), F32), pltpu.VMEM((ci, D_STATE), F32)],
        compiler_params=_cp("arbitrary", "arbitrary"),
        name="ssd",
    )(z_all, xbc_all, dt3, dtt3, conv0p, ssm0, conv_w, conv_b.reshape(1, cc),
      a_log.reshape(1, nh), a_log.reshape(nh, 1), dsk, g_ssm.reshape(1, ci), eh)


def _mix_kernel(ap_ref, as_ref, mp_ref, ms_ref, ga_ref, gb_ref, wpa_ref, wpb_ref, o_ref, *, npb):
    i = pl.program_id(1)
    a = jnp.where(i < npb, ap_ref[...], as_ref[...])
    m = jnp.where(i < npb, mp_ref[...], ms_ref[...])
    pa = _dot(a, wpa_ref[...])
    pb = _dot(m, wpb_ref[...])
    o_ref[...] = (ga_ref[...].astype(F32) * pa + gb_ref[...].astype(F32) * pb).astype(o_ref.dtype)


def _mix(a_p, a_s, m_p, m_s, gates, w_pa, w_pb):
    tp, hv = a_p.shape
    ts = a_s.shape[0]
    ci = m_p.shape[1]
    d = w_pa.shape[1]
    tm = _pick(math.gcd(tp, ts), (256, 128, 64))
    tn = _pick(d, (1024, 512, 256, 128))
    npb = tp // tm
    nj = d // tn
    pidx = lambda j, i: (jnp.minimum(i, npb - 1), 0)
    sidx = lambda j, i: (jnp.maximum(i - npb, 0), 0)
    return pl.pallas_call(
        functools.partial(_mix_kernel, npb=npb),
        out_shape=jax.ShapeDtypeStruct((tp + ts, d), BF16),
        grid=(nj, (tp + ts) // tm),
        in_specs=[pl.BlockSpec((tm, hv), pidx), pl.BlockSpec((tm, hv), sidx),
                  pl.BlockSpec((tm, ci), pidx), pl.BlockSpec((tm, ci), sidx),
                  pl.BlockSpec((tm, tn), lambda j, i: (i, j)),
                  pl.BlockSpec((tm, tn), lambda j, i: (i, nj + j)),
                  pl.BlockSpec((hv, tn), lambda j, i: (0, j)),
                  pl.BlockSpec((ci, tn), lambda j, i: (0, j))],
        out_specs=pl.BlockSpec((tm, tn), lambda j, i: (i, j)),
        compiler_params=_cp("arbitrary", "arbitrary"),
        name="branch_mix",
    )(a_p, a_s, m_p, m_s, gates, gates, w_pa, w_pb)


def _pack_bf16_pairs(x):
    n = x.shape[1] // 2
    lo = pltpu.bitcast(x[:, :n].astype(BF16).astype(F32), jnp.uint32)
    hi = pltpu.bitcast(x[:, n:].astype(BF16).astype(F32), jnp.uint32)
    return hi | (lo >> 16)


def _unpack_bf16_pairs(u):
    lo = pltpu.bitcast(u << 16, F32)
    hi = pltpu.bitcast(u & jnp.uint32(0xFFFF0000), F32)
    return jnp.concatenate([lo, hi], axis=1)


def _post_kernel(mx_ref, wout_ref, xp_ref, xs_ref, gt1_ref, sc2_ref, sh2_ref, g2_ref, wrh_ref, wrl_ref, br_ref,
                 x1_ref, h2p_ref, ti_ref, tp_ref, h2_ref, *, nexp, npb):
    is_prompt = pl.program_id(0) < npb
    o = _dot(mx_ref[...], wout_ref[...])
    tm = o.shape[0]
    for ch in range(tm // CHUNK):
        rows = slice(ch * CHUNK, (ch + 1) * CHUNK)
        x1 = jnp.where(is_prompt, xp_ref[0, rows, :], xs_ref[ch]) + gt1_ref[ch] * o[rows, :]
        x1_ref[rows, :] = x1
        xn = x1 * lax.rsqrt(jnp.mean(x1 * x1, axis=-1, keepdims=True) + EPS)
        h2_ref[rows, :] = xn * g2_ref[...] * (1.0 + sc2_ref[ch]) + sh2_ref[ch]
    h2 = h2_ref[...]
    h2p_ref[...] = _pack_bf16_pairs(h2)
    hh = h2.astype(BF16)
    hl = (h2 - hh.astype(F32)).astype(BF16)
    logits = _dot(hh, wrh_ref[...]) + _dot(hh, wrl_ref[...]) + _dot(hl, wrh_ref[...]) + br_ref[...]
    lane = lax.broadcasted_iota(I32, logits.shape, 1)
    logits = jnp.where(lane < nexp, logits, NEG_BIG)
    vals, idxs = [], []
    for _ in range(TOP_K):
        m = jnp.max(logits, axis=-1, keepdims=True)
        idx = jnp.min(jnp.where(logits == m, lane, LANES), axis=-1, keepdims=True)
        vals.append(m)
        idxs.append(idx)
        logits = jnp.where(lane == idx, 2.0 * NEG_BIG, logits)
    es = [jnp.exp(v - vals[0]) for v in vals]
    den = es[0]
    for e in es[1:]:
        den = den + e
    ti = jnp.zeros(lane.shape, I32)
    tpv = jnp.zeros(lane.shape, F32)
    for k in range(TOP_K):
        ti = jnp.where(lane == k, idxs[k], ti)
        tpv = jnp.where(lane == k, es[k] / den, tpv)
    ti_ref[...] = ti.T
    tp_ref[...] = tpv


def _post(mixed, w_out, xp, xs3, gt1, sc2, sh2, g2, wr_hi, wr_lo, br, nexp):
    t, d = mixed.shape
    ncp, ncs = xp.shape[0] * xp.shape[1] // CHUNK, xs3.shape[0]
    nc = _pick(math.gcd(xp.shape[1] // CHUNK, ncs), (4, 2, 1))
    tm = nc * CHUNK
    npb = ncp // nc
    row = lambda w: pl.BlockSpec((tm, w), lambda i: (i, 0))
    mod = pl.BlockSpec((nc, 1, d), lambda i: (i, 0, 0))
    const = lambda a, b: pl.BlockSpec((a, b), lambda i: (0, 0))
    return pl.pallas_call(
        functools.partial(_post_kernel, nexp=nexp, npb=npb),
        out_shape=(jax.ShapeDtypeStruct((t, d), F32), jax.ShapeDtypeStruct((t, d // 2), jnp.uint32),
                   jax.ShapeDtypeStruct((LANES, t), I32), jax.ShapeDtypeStruct((t, LANES), F32)),
        grid=(t // tm,),
        in_specs=[row(d), const(d, d),
                  _prompt_rows_spec(xp, tm, npb),
                  pl.BlockSpec((nc, CHUNK, d), lambda i: (jnp.maximum(i - npb, 0), 0, 0)),
                  mod, mod, mod, const(1, d),
                  const(d, LANES), const(d, LANES), const(1, LANES)],
        out_specs=(row(d), row(d // 2), pl.BlockSpec((LANES, tm), lambda i: (0, i)), row(LANES)),
        scratch_shapes=[pltpu.VMEM((tm, d), F32)],
        compiler_params=_cp("arbitrary"),
        name="post_mix",
    )(mixed, w_out, xp, xs3, gt1, sc2, sh2, g2.reshape(1, d), wr_hi, wr_lo, br)


def _slots_kernel(ti_ref, dest_ref, meta_ref, run_ref, *, bm, tb):
    ph = pl.program_id(0)
    i = pl.program_id(1)
    eid = lax.broadcasted_iota(I32, (LANES, tb), 0)
    onehots = [jnp.where(eid == ti_ref[k:k + 1, :], 1.0, 0.0) for k in range(TOP_K)]
    osum = onehots[0]
    for oh in onehots[1:]:
        osum = osum + oh
    blk_cnt = jnp.sum(osum, axis=1, keepdims=True)

    @pl.when(jnp.logical_and(ph == 0, i == 0))
    def _():
        run_ref[...] = jnp.zeros(run_ref.shape, F32)

    @pl.when(ph == 0)
    def _():
        run_ref[...] = run_ref[...] + blk_cnt

    @pl.when(jnp.logical_and(ph == 1, i == 0))
    def _():
        cnt = run_ref[...]
        nblk = jnp.ceil(cnt * (1.0 / bm))
        r = lax.broadcasted_iota(I32, (LANES, LANES), 0)
        c = lax.broadcasted_iota(I32, (LANES, LANES), 1)
        lstrict = jnp.where(c < r, 1.0, 0.0).astype(BF16)
        start_blk = _dot(lstrict, jnp.broadcast_to(nblk, (LANES, LANES)).astype(BF16))
        lane = lax.broadcasted_iota(I32, (LANES, LANES), 1)
        meta_ref[...] = jnp.where(lane == 0, cnt, jnp.where(lane == 1, start_blk, 0.0))
        run_ref[...] = start_blk[:, 0:1] * float(bm)

    @pl.when(ph == 1)
    def _():
        r = lax.broadcasted_iota(I32, (tb, tb), 0)
        c = lax.broadcasted_iota(I32, (tb, tb), 1)
        ustrict = jnp.where(r < c, 1.0, 0.0).astype(BF16)
        base = run_ref[...] + _dot(osum.astype(BF16), ustrict)
        for k in range(TOP_K):
            dest_ref[0, k:k + 1, :] = jnp.sum(onehots[k] * base, axis=0, keepdims=True).astype(I32)
        dest_ref[0, TOP_K:, :] = jnp.zeros((SUBLANES - TOP_K, tb), I32)
        run_ref[...] = run_ref[...] + blk_cnt


def _moe_slots(ti_t, bm, tb):
    t = ti_t.shape[1]
    nt = t // tb
    return pl.pallas_call(
        functools.partial(_slots_kernel, bm=bm, tb=tb),
        out_shape=(jax.ShapeDtypeStruct((nt, SUBLANES, tb), I32),
                   jax.ShapeDtypeStruct((LANES, LANES), F32)),
        grid=(2, nt),
        in_specs=[pl.BlockSpec((SUBLANES, tb), lambda ph, i: (0, i))],
        out_specs=(pl.BlockSpec((1, SUBLANES, tb), lambda ph, i: (i * ph, 0, 0)),
                   pl.BlockSpec((LANES, LANES), lambda ph, i: (0, 0))),
        scratch_shapes=[pltpu.VMEM((LANES, 1), F32)],
        compiler_params=_cp("arbitrary", "arbitrary"),
        name="moe_slots",
    )(ti_t)


def _dispatch_kernel(zs_ref, idx_ref, h_ref, xg_ref, zbuf_ref, idx_smem, isem, sem, *, nz, bm, tb):
    i = pl.program_id(0)

    def zero_fill(e):
        return pltpu.make_async_copy(zbuf_ref, xg_ref.at[pl.ds(pl.multiple_of(zs_ref[e], bm), bm)], sem)

    @pl.when(i == 0)
    def _():
        zbuf_ref[...] = jnp.zeros(zbuf_ref.shape, zbuf_ref.dtype)
        for e in range(nz):
            pl.when(zs_ref[e] >= 0)(lambda e=e: zero_fill(e).start())
        for e in range(nz):
            pl.when(zs_ref[e] >= 0)(lambda e=e: zero_fill(e).wait())

    icp = pltpu.make_async_copy(idx_ref.at[0], idx_smem, isem)
    icp.start()
    icp.wait()

    def start(r, carry):
        for k in range(TOP_K):
            pltpu.make_async_copy(h_ref.at[pl.ds(r, 1)], xg_ref.at[pl.ds(idx_smem[k, r], 1)],
                                  sem).start(priority=k % 2)
        return carry

    lax.fori_loop(0, tb, start, 0, unroll=True)
    for k in range(TOP_K):
        pltpu.make_async_copy(h_ref, xg_ref.at[pl.ds(0, tb)], sem).wait()


def _moe_dispatch(zstart, dest3, h2, n_slots, bm):
    nt, _, tb = dest3.shape
    d = h2.shape[1]
    return pl.pallas_call(
        functools.partial(_dispatch_kernel, nz=zstart.shape[0], bm=bm, tb=tb),
        out_shape=jax.ShapeDtypeStruct((n_slots, d), h2.dtype),
        grid_spec=pltpu.PrefetchScalarGridSpec(
            num_scalar_prefetch=1,
            grid=(nt,),
            in_specs=[pl.BlockSpec((1, SUBLANES, tb), lambda i, zs: (i, 0, 0)),
                      pl.BlockSpec((tb, d), lambda i, zs: (i, 0))],
            out_specs=pl.BlockSpec(memory_space=pl.ANY),
            scratch_shapes=[pltpu.VMEM((bm, d), h2.dtype), pltpu.SMEM((SUBLANES, tb), I32),
                            pltpu.SemaphoreType.DMA, pltpu.SemaphoreType.DMA]),
        compiler_params=_cp("arbitrary"),
        name="moe_dispatch",
    )(zstart, dest3, h2)


def _weight_stream(w_refs, wbuf_ref, wb_refs, sem, cnt_ref, be_ref, nx_ref, used, tw):
    c = pl.program_id(0)
    b = pl.program_id(1)
    nc = pl.num_programs(0)

    def fetch(e, cc, slot):
        col = pl.ds(pl.multiple_of(cc * tw, tw), tw)
        return [pltpu.make_async_copy(w.at[e, :, col], wbuf_ref.at[slot, m], sem.at[slot])
                for m, w in enumerate(w_refs)]

    @pl.when(jnp.logical_and(c == 0, b == 0))
    def _():
        cnt_ref[0] = 0
        for cp in fetch(be_ref[0], 0, 0):
            cp.start(priority=WEIGHT_DMA_PRIORITY)

    changed = jnp.logical_or(b == 0, be_ref[b] != be_ref[jnp.maximum(b - 1, 0)])

    @pl.when(jnp.logical_and(used, changed))
    def _():
        slot = cnt_ref[0] & 1
        for cp in fetch(be_ref[b], c, slot):
            cp.wait()
        for m, wb in enumerate(wb_refs):
            wb[...] = wbuf_ref[slot, m].astype(BF16)
        nxt = nx_ref[b]
        same_chunk = nxt >= 0
        nxt_e = jnp.where(same_chunk, nxt, be_ref[0])
        nxt_c = jnp.where(same_chunk, c, c + 1)

        @pl.when(jnp.logical_or(same_chunk, c + 1 < nc))
        def _():
            for cp in fetch(nxt_e, nxt_c, 1 - slot):
                cp.start(priority=WEIGHT_DMA_PRIORITY)

        cnt_ref[0] = cnt_ref[0] + 1


def _gateup_kernel(be_ref, nx_ref, nu_ref, x_ref, bg_ref, bu_ref, wg_ref, wu_ref, o_ref,
                   wbuf_ref, wgb_ref, wub_ref, sem, cnt_ref, *, tf):
    used = pl.program_id(1) < nu_ref[0]
    _weight_stream([wg_ref, wu_ref], wbuf_ref, [wgb_ref, wub_ref], sem, cnt_ref, be_ref, nx_ref, used, tf)

    @pl.when(used)
    def _():
        x = _unpack_bf16_pairs(x_ref[...]).astype(BF16)
        gate = jnp.minimum(_dot(x, wgb_ref[...]) + bg_ref[0], SWIGLU_LIMIT)
        up = jnp.clip(_dot(x, wub_ref[...]) + bu_ref[0], -SWIGLU_LIMIT, SWIGLU_LIMIT)
        glu = gate * jax.nn.sigmoid(SWIGLU_ALPHA * gate)
        o_ref[...] = ((up + 1.0) * glu).astype(o_ref.dtype)

    @pl.when(jnp.logical_not(used))
    def _():
        o_ref[...] = jnp.zeros(o_ref.shape, o_ref.dtype)


def _moe_gateup(block_e, next_e, n_used, xg, w_gate, w_up, b_gate, b_up):
    ns, dh = xg.shape
    nexp, d, ff = w_gate.shape
    assert d == 2 * dh
    bm = MOE_BLOCK
    nb = ns // bm
    tf = _pick(ff, (1024, 512, 256, 128))
    bspec = pl.BlockSpec((1, 1, tf), lambda c, b, be, nx, nu: (be[b], 0, c))
    hbm = pl.BlockSpec(memory_space=pl.ANY)
    return pl.pallas_call(
        functools.partial(_gateup_kernel, tf=tf),
        out_shape=jax.ShapeDtypeStruct((ns, ff), BF16),
        grid_spec=pltpu.PrefetchScalarGridSpec(
            num_scalar_prefetch=3,
            grid=(ff // tf, nb),
            in_specs=[pl.BlockSpec((bm, dh), lambda c, b, be, nx, nu: (jnp.minimum(b, nu[0] - 1), 0)),
                      bspec, bspec, hbm, hbm],
            out_specs=pl.BlockSpec((bm, tf), lambda c, b, be, nx, nu: (b, c)),
            scratch_shapes=[pltpu.VMEM((2, 2, d, tf), F32), pltpu.VMEM((d, tf), BF16),
                            pltpu.VMEM((d, tf), BF16), pltpu.SemaphoreType.DMA((2,)),
                            pltpu.SMEM((1,), I32)]),
        compiler_params=_cp("arbitrary", "arbitrary"),
        name="moe_gateup",
    )(block_e, next_e, n_used, xg, b_gate.reshape(nexp, 1, ff), b_up.reshape(nexp, 1, ff), w_gate, w_up)


def _down_kernel(be_ref, nx_ref, nu_ref, g_ref, bd_ref, wd_ref, o_ref, wbuf_ref, wdb_ref, sem, cnt_ref, *, tn):
    used = pl.program_id(1) < nu_ref[0]
    _weight_stream([wd_ref], wbuf_ref, [wdb_ref], sem, cnt_ref, be_ref, nx_ref, used, tn)

    @pl.when(used)
    def _():
        o_ref[...] = _pack_bf16_pairs(_dot(g_ref[...], wdb_ref[...]) + bd_ref[0])

    @pl.when(jnp.logical_not(used))
    def _():
        o_ref[...] = jnp.zeros(o_ref.shape, o_ref.dtype)


def _moe_down(block_e, next_e, n_used, glu, w_down, b_down):
    ns, ff = glu.shape
    nexp, _, d = w_down.shape
    bm = MOE_BLOCK
    nb = ns // bm
    tn = d
    return pl.pallas_call(
        functools.partial(_down_kernel, tn=tn),
        out_shape=jax.ShapeDtypeStruct((ns, d // 2), jnp.uint32),
        grid_spec=pltpu.PrefetchScalarGridSpec(
            num_scalar_prefetch=3,
            grid=(d // tn, nb),
            in_specs=[pl.BlockSpec((bm, ff), lambda c, b, be, nx, nu: (jnp.minimum(b, nu[0] - 1), 0)),
                      pl.BlockSpec((1, 1, tn), lambda c, b, be, nx, nu: (be[b], 0, c)),
                      pl.BlockSpec(memory_space=pl.ANY)],
            out_specs=pl.BlockSpec((bm, tn // 2), lambda c, b, be, nx, nu: (b, c)),
            scratch_shapes=[pltpu.VMEM((2, 1, ff, tn), F32), pltpu.VMEM((ff, tn), BF16),
                            pltpu.SemaphoreType.DMA((2,)), pltpu.SMEM((1,), I32)]),
        compiler_params=_cp("arbitrary", "arbitrary"),
        name="moe_down",
    )(block_e, next_e, n_used, glu, b_down.reshape(nexp, 1, d), w_down)


def _combine_kernel(idx_ref, yb_ref, x1_ref, gt2_ref, p_ref, o_ref, buf_ref, idx_smem, isem, sem, *, tc):
    icp = pltpu.make_async_copy(idx_ref.at[0], idx_smem, isem)
    icp.start()
    icp.wait()

    def start(r, carry):
        for k in range(TOP_K):
            pltpu.make_async_copy(yb_ref.at[pl.ds(idx_smem[k, r], 1)],
                                  buf_ref.at[pl.ds(k * tc + r, 1)], sem).start(priority=k % 2)
        return carry

    lax.fori_loop(0, tc, start, 0, unroll=True)
    pltpu.make_async_copy(yb_ref.at[pl.ds(0, TOP_K * tc)], buf_ref, sem).wait()
    ff = p_ref[:, 0:1] * _unpack_bf16_pairs(buf_ref[0:tc, :])
    for k in range(1, TOP_K):
        ff = ff + p_ref[:, k:k + 1] * _unpack_bf16_pairs(buf_ref[k * tc:(k + 1) * tc, :])
    for ch in range(tc // CHUNK):
        rows = slice(ch * CHUNK, (ch + 1) * CHUNK)
        y = x1_ref[rows, :] + gt2_ref[ch] * ff[rows, :]
        if o_ref.shape[0] == 1:
            o_ref[0, rows, :] = y
        else:
            o_ref[ch] = y


def _moe_combine(dest3, yb, x1, gt2, top_p, row0, nseq, seqlen):
    d = 2 * yb.shape[1]
    tc = dest3.shape[2]
    nc = tc // CHUNK
    rb0 = row0 // tc
    nrows = nseq * seqlen
    if seqlen >= tc:
        per_seq = seqlen // tc
        out_spec = pl.BlockSpec((1, tc, d), lambda i: (i // per_seq, i % per_seq, 0))
    else:
        assert seqlen == CHUNK
        out_spec = pl.BlockSpec((nc, CHUNK, d), lambda i: (i, 0, 0))
    return pl.pallas_call(
        functools.partial(_combine_kernel, tc=tc),
        out_shape=jax.ShapeDtypeStruct((nseq, seqlen, d), F32),
        grid=(nrows // tc,),
        in_specs=[pl.BlockSpec((1, SUBLANES, tc), lambda i: (rb0 + i, 0, 0)),
                  pl.BlockSpec(memory_space=pl.ANY),
                  pl.BlockSpec((tc, d), lambda i: (rb0 + i, 0)),
                  pl.BlockSpec((nc, 1, d), lambda i: (rb0 + i, 0, 0)),
                  pl.BlockSpec((tc, LANES), lambda i: (rb0 + i, 0))],
        out_specs=out_spec,
        scratch_shapes=[pltpu.VMEM((TOP_K * tc, d // 2), jnp.uint32), pltpu.SMEM((SUBLANES, tc), I32),
                        pltpu.SemaphoreType.DMA, pltpu.SemaphoreType.DMA],
        compiler_params=_cp("arbitrary"),
        name="moe_combine",
    )(dest3, yb, x1, gt2, top_p)


def _block_layout(meta, nexp, bm, n_blocks):
    counts = meta[:nexp, 0]
    start_blk = meta[:nexp, 1].astype(I32)
    nblk = jnp.ceil(counts * (1.0 / bm)).astype(I32)
    end_blk = start_blk + nblk
    n_used = jnp.sum(nblk)
    blk = jnp.minimum(jnp.arange(n_blocks), n_used - 1)
    block_e = jnp.minimum(jnp.sum(end_blk[None, :] <= blk[:, None], axis=1), nexp - 1).astype(I32)
    onehot_e = block_e[:, None] == jnp.arange(nexp)[None, :]
    run_end = jnp.sum(jnp.where(onehot_e, end_blk[None, :], 0), axis=1)
    at_end = run_end[:, None] == jnp.arange(n_blocks)[None, :]
    next_e = jnp.where(run_end < n_used, jnp.sum(jnp.where(at_end, block_e[None, :], 0), axis=1), -1).astype(I32)
    last = jnp.where(nblk > 0, (end_blk - 1) * bm, -1)
    spare = n_used + jnp.arange(nexp)
    spare = jnp.where(spare < n_blocks, spare * bm, -1)
    zstart = jnp.concatenate([last, spare]).astype(I32)
    return block_e, next_e, n_used.reshape(1).astype(I32), zstart


def _rope_tables(pos):
    half = ROPE_DIM // 2
    inv = ROPE_THETA ** (-jnp.arange(half, dtype=F32) / half)
    ang = pos.astype(F32)[:, None] * inv[None, :]
    z = jnp.zeros((pos.shape[0], LANES - ROPE_DIM), F32)
    cos, sin = jnp.cos(ang), jnp.sin(ang)
    return jnp.concatenate([cos, cos, z], axis=1), jnp.concatenate([sin, sin, z], axis=1)


def _rot_half_cols(w):
    half = ROPE_DIM // 2
    return jnp.concatenate([-w[..., half:], w[..., :half]], axis=-1)


def _layer(x_prompt, x_sample, past_lat, past_kr, ssm_s0, conv_s0, c_prompt, c_sample,
           w_ada, b_ada, g_norm1, w_in, g_cq, g_ckv, w_uq, w_uk, w_uv, g_qn, g_kn, conv_w, conv_b, dt_bias,
           a_log, d_skip, g_ssm, w_pa, w_pb, w_out, g_norm2, w_router, b_router, w_gate, b_gate, w_up, b_up,
           w_down, b_down):
    bp, sp, d = x_prompt.shape
    bs, ss, _ = x_sample.shape
    assert ss == CHUNK and sp % CHUNK == 0
    past = past_lat.shape[1]
    ql, kvl = g_cq.shape[-1], g_ckv.shape[-1]
    heads = w_uq.shape[1]
    ci = g_ssm.shape[-1]
    cc = conv_w.shape[-1]
    nh = ci // M_HEADDIM
    groups = (cc - ci) // (2 * D_STATE)
    nexp = w_router.shape[-1]
    tp, ts = bp * sp, bs * ss
    t = tp + ts
    ncp, ncs = tp // CHUNK, ts // CHUNK

    ada = _ada(jnp.concatenate([c_prompt, c_sample], axis=0), w_ada, b_ada)
    per_chunk = jnp.concatenate([jnp.repeat(ada[:bp], sp // CHUNK, axis=0), ada[bp:]], axis=0)
    sh1, sc1, gt1, sh2, sc2, gt2 = [m[:, None, :] for m in jnp.split(per_chunk, 6, axis=-1)]

    xs3 = x_sample
    h3 = _norm1(x_prompt, xs3, g_norm1, sc1, sh1)
    h_all = h3.reshape(t, d)

    o = 0
    w_cq = w_in[:, o:o + ql]; o += ql
    w_ckv = w_in[:, o:o + kvl]; o += kvl
    w_kr = w_in[:, o:o + ROPE_DIM]; o += ROPE_DIM
    w_z = w_in[:, o:o + ci]; o += ci
    w_xbc = w_in[:, o:o + cc]; o += cc
    w_dt = w_in[:, o:o + nh]; o += nh
    w_gab = w_in[:, o:o + 2 * d]
    zc = lambda n_: jnp.zeros((d, n_), F32)
    w_lat = jnp.concatenate([w_cq, w_ckv, w_kr, zc(LANES - ROPE_DIM), _rot_half_cols(w_kr),
                             zc(LANES - ROPE_DIM), w_dt, zc(LANES - nh)], axis=1).astype(BF16)

    cos_p, sin_p = _rope_tables(jnp.arange(sp))
    cos_s, sin_s = _rope_tables(past + jnp.arange(ss))
    cos128 = jnp.concatenate([jnp.tile(cos_p, (bp, 1)), jnp.tile(cos_s, (bs, 1))], axis=0)
    sin128 = jnp.concatenate([jnp.tile(sin_p, (bp, 1)), jnp.tile(sin_s, (bs, 1))], axis=0)

    cqn, lat_all, latk, kr_all, dt_all, dtt_all = _lat(
        h_all, w_lat, w_dt.T.astype(BF16), g_cq, g_ckv, cos128, sin128, dt_bias)
    z_all = _mm(h_all, w_z.astype(BF16), BF16, "proj_z")
    w_xbc16 = w_xbc.astype(BF16)
    xbc_all = _mm(h_all, w_xbc16, BF16, "proj_xbc")
    gates = _mm(h_all, w_gab.astype(BF16), BF16, "proj_gates", act="sigmoid")

    tail = CONV_W - 1
    cps = sp // CHUNK
    h_tail = jnp.concatenate([h3[cps - 1:ncp:cps, CHUNK - tail:, :].reshape(bp * tail, d),
                              h3[ncp:, CHUNK - tail:, :].reshape(bs * tail, d)], axis=0)
    conv_tail = _mm(h_tail, w_xbc16, F32, "proj_conv_tail")
    conv_p = conv_tail[:bp * tail].reshape(bp, tail, cc)
    conv_s = conv_tail[bp * tail:].reshape(bs, tail, cc)

    wq_a = jnp.concatenate([w_uq, jnp.zeros((ql, heads, HEAD_PAD - QK_DIM), F32)], axis=-1)
    wq_a = wq_a.reshape(ql, heads * HEAD_PAD).astype(BF16)
    wq_b = jnp.concatenate([_rot_half_cols(w_uq[..., NOPE_DIM:]),
                            jnp.zeros((ql, heads, LANES - ROPE_DIM), F32)], axis=-1)
    wq_b = wq_b.reshape(ql, heads * LANES).astype(BF16)
    gq = g_qn * g_kn * (ATTN_SCALE * math.log2(math.e))
    g_nope = gq[:NOPE_DIM].reshape(1, LANES)
    g_rope = jnp.concatenate([gq[NOPE_DIM:], jnp.zeros((LANES - ROPE_DIM,), F32)]).reshape(1, LANES)
    q_all = _qproj(cqn, wq_a, wq_b, cos128, sin128, g_nope, g_rope, heads)

    w_uk2 = w_uk.reshape(kvl, heads * NOPE_DIM)
    w_uv2 = w_uv.reshape(kvl, heads * V_DIM).astype(BF16)
    w_kv = jnp.concatenate([w_uk2.astype(BF16), w_uv2], axis=1)
    k_p, v_p = _kvproj(latk, w_kv, tp, heads, kvl)
    a_p = _attn_prompt(q_all, k_p, v_p, bp, sp, heads)
    a_s = _attn_sample(past_lat, past_kr, latk, q_all, w_uk2.T.astype(BF16), w_uv2, tp, heads)

    eh = jnp.repeat(jnp.eye(nh, dtype=BF16), M_HEADDIM, axis=1)
    dsk = jnp.repeat(d_skip, M_HEADDIM).reshape(1, ci)
    lcp = _pick(sp, (256, 128))
    assert sp % lcp == 0 and lcp % LANES == 0
    ncq = sp // lcp
    dt3_p = dt_all[:tp].reshape(bp * ncq, lcp, nh)
    dtt3_p = dtt_all[:, :tp].reshape(nh, bp * ncq, lcp).transpose(1, 0, 2)
    lcs = CHUNK
    dt3_s = dt_all[tp:].reshape(bs, CHUNK, nh)
    dtt3_s = dtt_all[:, tp:].reshape(nh, bs, CHUNK).transpose(1, 0, 2)
    pad_conv = lambda c0: jnp.concatenate(
        [jnp.zeros((c0.shape[0], SUBLANES - tail, cc), F32), c0], axis=1)
    m_p, ssm_p = _ssd(z_all, xbc_all, dt3_p, dtt3_p, jnp.zeros((bp, SUBLANES, cc), F32),
                      jnp.zeros((bp, nh, M_HEADDIM, D_STATE), F32), conv_w, conv_b, a_log, dsk, g_ssm, eh,
                      0, bp, ncq, lcp, lcp, groups)
    m_s, ssm_s = _ssd(z_all, xbc_all, dt3_s, dtt3_s, pad_conv(conv_s0),
                      ssm_s0, conv_w, conv_b, a_log, dsk, g_ssm, eh,
                      tp, bs, 1, lcs, CHUNK, groups)

    mixed = _mix(a_p, a_s, m_p, m_s, gates, w_pa.astype(BF16), w_pb.astype(BF16))
    wr = jnp.concatenate([w_router, jnp.zeros((d, LANES - nexp), F32)], axis=1)
    wr_hi = wr.astype(BF16)
    wr_lo = (wr - wr_hi.astype(F32)).astype(BF16)
    br = jnp.concatenate([b_router, jnp.zeros((LANES - nexp,), F32)]).reshape(1, LANES)
    x1, h2, ti, tpr = _post(mixed, w_out.astype(BF16), x_prompt, xs3, gt1, sc2, sh2, g_norm2, wr_hi, wr_lo, br, nexp)

    assert nexp <= LANES
    bm = MOE_BLOCK
    n_blocks = -(-(t * TOP_K + nexp * (bm - 1)) // bm)
    tb = _pick(math.gcd(tp, ts), (256, 128))
    dest3, meta = _moe_slots(ti, bm, tb)
    block_e, next_e, n_used, zstart = _block_layout(meta, nexp, bm, n_blocks)
    xg = _moe_dispatch(zstart, dest3, h2, n_blocks * bm, bm)
    glu = _moe_gateup(block_e, next_e, n_used, xg, w_gate, w_up, b_gate, b_up)
    yb = _moe_down(block_e, next_e, n_used, glu, w_down, b_down)
    y_p = _moe_combine(dest3, yb, x1, gt2, tpr, 0, bp, sp)
    y_s = _moe_combine(dest3, yb, x1, gt2, tpr, tp, bs, ss)

    return (y_p, y_s,
            lat_all[:tp].reshape(bp, sp, kvl), kr_all[:tp].reshape(bp, sp, ROPE_DIM),
            ssm_p, conv_p,
            lat_all[tp:].reshape(bs, ss, kvl), kr_all[tp:].reshape(bs, ss, ROPE_DIM),
            ssm_s, conv_s)


def kernel(x_prompt, x_sample, cache_mla_latent, cache_mla_krope, state_ssm, state_conv, c_prompt, c_sample,
           w_ada, b_ada, g_norm1, w_in, g_cq, g_ckv, w_uq, w_uk, w_uv, g_qn, g_kn, conv_w, conv_b, dt_bias,
           a_log, d_skip, g_ssm, w_pa, w_pb, w_out, g_norm2, w_router, b_router, w_gate, b_gate, w_up, b_up,
           w_down, b_down):
    depth = w_ada.shape[0]
    assert depth == 1, "single-layer encoder"
    weights = (w_ada, b_ada, g_norm1, w_in, g_cq, g_ckv, w_uq, w_uk, w_uv, g_qn, g_kn, conv_w, conv_b, dt_bias,
               a_log, d_skip, g_ssm, w_pa, w_pb, w_out, g_norm2, w_router, b_router, w_gate, b_gate, w_up, b_up,
               w_down, b_down)
    drop = lambda a: a.reshape(a.shape[1:])
    outs = _layer(x_prompt, x_sample, drop(cache_mla_latent), drop(cache_mla_krope), drop(state_ssm),
                  drop(state_conv), c_prompt, c_sample, *[drop(w) for w in weights])
    return outs[:2] + tuple(o.reshape((1,) + o.shape) for o in outs[2:])
```
